```python
import math
import jax
import jax.numpy as jnp
from jax import lax
import numpy as np

D_MODEL = 2048
BATCH = 4
SEQ = 2048
DEPTH = 1
DEC_BATCH = 32
DEC_SEQ = 8
PAST_LEN = 8192
PAGE_SIZE = 128

A_HEADS = 8
A_DK = 128
A_DV = 128
A_WIDTH = A_HEADS * A_DK
HGRN_CHUNK = 64
B_HEADS = 8
B_HEAD_DIM = 128
B_WIDTH = B_HEADS * B_HEAD_DIM
MOBA_BLOCK = 256
MOBA_TOPK = 3
MOBA_Q_SWEEP = 16
PEER_HEADS = 8
PEER_NKEYS = 128
PEER_N_EXPERTS = PEER_NKEYS * PEER_NKEYS
PEER_DK = 256
PEER_TOPK = 16
PEER_TOKEN_CHUNK = 128
RMS_EPS = 1e-6
NEG_INF = -1e30
IN_WIDTHS = (A_WIDTH, A_WIDTH, A_WIDTH, A_WIDTH, B_WIDTH, B_WIDTH, B_WIDTH, D_MODEL, D_MODEL)
IN_WIDTH = 4 * A_WIDTH + 3 * B_WIDTH + 2 * D_MODEL

kernel_name = 'hgrn2_moba_peer_hybrid_step'


def rmsnorm(x, gain):
    xf = x.astype(jnp.float32)
    xf = xf * lax.rsqrt(jnp.mean(xf * xf, axis=-1, keepdims=True) + RMS_EPS)
    return (xf * gain.astype(jnp.float32)).astype(x.dtype)


def hgrn2_recurrence(q, log_f, k, v, s0):
    bsz, length, heads, _ = q.shape
    dv = v.shape[-1]
    c = math.gcd(length, HGRN_CHUNK)
    n = length // c

    def to_chunks(t):
        return t.astype(jnp.float32).reshape(bsz, n, c, heads, t.shape[-1]).transpose(1, 0, 3, 2, 4)

    qc, fc, kc, vc = to_chunks(q), to_chunks(log_f), to_chunks(k), to_chunks(v)
    causal = jnp.tril(jnp.ones((c, c), dtype=bool))

    def step(s, inp):
        qq, ff, kk, vv = inp
        b = jnp.cumsum(ff, axis=-2)
        o_inter = jnp.einsum('bhtk,bhkv->bhtv', qq * jnp.exp(b), s)
        diff = b[:, :, :, None, :] - b[:, :, None, :, :]
        decay = jnp.exp(jnp.where(causal[:, :, None], diff, -jnp.inf))
        att = jnp.einsum('bhtk,bhtsk,bhsk->bhts', qq, decay, kk)
        o = o_inter + jnp.einsum('bhts,bhsv->bhtv', att, vv)
        b_last = b[:, :, -1:, :]
        s_new = jnp.exp(b_last[:, :, 0, :])[..., None] * s + jnp.einsum('bhsk,bhsv->bhkv', kk * jnp.exp(b_last - b), vv)
        return s_new, o

    s_fin, o = lax.scan(step, s0.astype(jnp.float32), (qc, fc, kc, vc))
    o = o.transpose(1, 0, 3, 2, 4).reshape(bsz, length, heads, dv)
    return o, s_fin


def moba_sequence(q, k, v, q_pos):
    f32 = jnp.float32
    lq, heads, hd = q.shape
    nblk = k.shape[0] // MOBA_BLOCK
    kb = k.astype(f32).reshape(nblk, MOBA_BLOCK, heads, hd).transpose(2, 0, 1, 3)
    vb = v.astype(f32).reshape(nblk, MOBA_BLOCK, heads, hd).transpose(2, 0, 1, 3)
    k_mean = jnp.mean(kb, axis=2)
    n_sel = min(MOBA_TOPK, nblk)
    qc = math.gcd(lq, MOBA_Q_SWEEP)
    scale = hd ** -0.5
    head_idx = jnp.arange(heads)[:, None, None]

    def attend(args):
        qq, pp = args
        qq = qq.astype(f32)
        own = pp // MOBA_BLOCK
        gate = jnp.einsum('qhd,hnd->hqn', qq, k_mean)
        fully_past = jnp.arange(nblk)[None, :] < own[:, None]
        gate = jnp.where(fully_past[None], gate, NEG_INF)
        _, top_idx = lax.top_k(gate, n_sel)
        valid = jnp.arange(n_sel)[None, :] < own[:, None]
        sel = jnp.concatenate([top_idx, jnp.broadcast_to(own[None, :, None], (heads, qc, 1)).astype(top_idx.dtype)], axis=-1)
        sel_valid = jnp.concatenate([valid, jnp.ones((qc, 1), dtype=bool)], axis=-1)
        kg = kb[head_idx, sel]
        vg = vb[head_idx, sel]
        k_pos = sel[..., None] * MOBA_BLOCK + jnp.arange(MOBA_BLOCK)
        mask = sel_valid[None, :, :, None] & (k_pos <= pp[None, :, None, None])
        s = jnp.einsum('qhd,hqjpd->hqjp', qq, kg) * scale
        s = jnp.where(mask, s, NEG_INF).reshape(heads, qc, -1)
        w = jax.nn.softmax(s, axis=-1).reshape(heads, qc, n_sel + 1, MOBA_BLOCK)
        return jnp.einsum('hqjp,hqjpd->qhd', w, vg)

    out = lax.map(attend, (q.reshape(lq // qc, qc, heads, hd), q_pos.reshape(lq // qc, qc)))
    return out.reshape(lq, heads, hd)


def token_mixer(hn, k_past, v_past, s0, w_in, lb, a_gnorm, w_a_proj, w_b_proj, w_out):
    f32 = jnp.float32
    bsz, length, _ = hn.shape
    pos0 = k_past.shape[1]
    split_points = [int(p) for p in np.cumsum(IN_WIDTHS)[:-1]]
    a_q, a_f, a_i, a_g, b_q, b_k, b_v, g_a, g_b = jnp.split(hn @ w_in, split_points, axis=-1)
    forget = lb + (1.0 - lb) * jax.nn.sigmoid(a_f.astype(f32))
    a_shape = (bsz, length, A_HEADS, A_DK)
    o_a, s_new = hgrn2_recurrence(a_q.reshape(a_shape), jnp.log(forget).reshape(a_shape),
                                  (1.0 - forget).reshape(a_shape), a_i.reshape(bsz, length, A_HEADS, A_DV), s0)
    o_a = rmsnorm(o_a, a_gnorm.reshape(A_HEADS, A_DV)) * jax.nn.silu(a_g.astype(f32).reshape(bsz, length, A_HEADS, A_DV))
    y_a = o_a.reshape(bsz, length, A_WIDTH).astype(hn.dtype) @ w_a_proj
    b_shape = (bsz, length, B_HEADS, B_HEAD_DIM)
    q_b = b_q.reshape(b_shape)
    k_new = b_k.reshape(b_shape)
    v_new = b_v.reshape(b_shape)
    pad = (-(pos0 + length)) % MOBA_BLOCK
    pad_cfg = ((0, 0), (0, pad), (0, 0), (0, 0))
    k_all = jnp.pad(jnp.concatenate([k_past.astype(hn.dtype), k_new], axis=1), pad_cfg)
    v_all = jnp.pad(jnp.concatenate([v_past.astype(hn.dtype), v_new], axis=1), pad_cfg)
    q_pos = pos0 + jnp.arange(length, dtype=jnp.int32)
    o_b = lax.map(lambda t: moba_sequence(t[0], t[1], t[2], q_pos), (q_b, k_all, v_all))
    y_b = o_b.reshape(bsz, length, B_WIDTH).astype(hn.dtype) @ w_b_proj
    merged = jax.nn.sigmoid(g_a) * y_a + jax.nn.sigmoid(g_b) * y_b
    return merged @ w_out, s_new, k_new, v_new


def peer_ffn(hn, w_query, sub_keys, expert_u, expert_v):
    f32 = jnp.float32
    bsz, length, d = hn.shape
    n_tok = bsz * length
    c = math.gcd(n_tok, PEER_TOKEN_CHUNK)

    def retrieve(xc):
        q = (xc @ w_query).astype(f32).reshape(c, PEER_HEADS, 2, PEER_DK // 2)
        s = jnp.einsum('nhad,hakd->nhak', q, sub_keys.astype(f32))
        top_s, top_i = lax.top_k(s, PEER_TOPK)
        cand_s = (top_s[:, :, 0, :, None] + top_s[:, :, 1, None, :]).reshape(c, PEER_HEADS, -1)
        cand_i = (top_i[:, :, 0, :, None] * PEER_NKEYS + top_i[:, :, 1, None, :]).reshape(c, PEER_HEADS, -1)
        best_s, best_pos = lax.top_k(cand_s, PEER_TOPK)
        expert_idx = jnp.take_along_axis(cand_i, best_pos, axis=-1)
        g = jax.nn.softmax(best_s, axis=-1)
        u = expert_u[expert_idx]
        h = jax.nn.gelu(jnp.einsum('nd,nhed->nhe', xc, u).astype(f32))
        v = expert_v[expert_idx]
        return jnp.einsum('nhe,nhed->nd', (g * h).astype(xc.dtype), v)

    return lax.map(retrieve, hn.reshape(n_tok // c, c, d)).reshape(bsz, length, d)


def setup_inputs(seed: int = 0) -> dict:
    key = jax.random.key(seed)
    ks = jax.random.split(key, 20)
    f32 = jnp.float32
    n_pages = PAST_LEN // PAGE_SIZE
    n_used = DEC_BATCH * n_pages
    n_phys = n_used + max(1, n_used // 4)

    def nrm(k, shape, scale):
        return jax.random.normal(k, shape, f32) * scale

    page_table = jax.random.permutation(ks[5], n_phys)[:n_used].reshape(DEC_BATCH, n_pages).astype(jnp.int32)
    return {
        'x_prompt': nrm(ks[0], (BATCH, SEQ, D_MODEL), 1.0),
        'x_sample': nrm(ks[1], (DEC_BATCH, DEC_SEQ, D_MODEL), 1.0),
        'cache_k': nrm(ks[2], (DEPTH, n_phys, PAGE_SIZE, B_HEADS, B_HEAD_DIM), 1.0),
        'cache_v': nrm(ks[3], (DEPTH, n_phys, PAGE_SIZE, B_HEADS, B_HEAD_DIM), 1.0),
        'state_hgrn': nrm(ks[4], (DEPTH, DEC_BATCH, A_HEADS, A_DK, A_DV), 0.5),
        'page_table': page_table,
        'norm_mix': 1.0 + nrm(ks[6], (DEPTH, D_MODEL), 0.01),
        'w_in': nrm(ks[7], (DEPTH, D_MODEL, IN_WIDTH), D_MODEL ** -0.5),
        'hgrn_lb_logits': nrm(ks[8], (DEPTH + 1, A_WIDTH), 0.5),
        'hgrn_gnorm': 1.0 + nrm(ks[9], (DEPTH, A_WIDTH), 0.01),
        'w_a_proj': nrm(ks[10], (DEPTH, A_WIDTH, D_MODEL), A_WIDTH ** -0.5),
        'w_b_proj': nrm(ks[11], (DEPTH, B_WIDTH, D_MODEL), B_WIDTH ** -0.5),
        'w_out': nrm(ks[12], (DEPTH, D_MODEL, D_MODEL), D_MODEL ** -0.5),
        'norm_ffn': 1.0 + nrm(ks[13], (DEPTH, D_MODEL), 0.01),
        'peer_w_query': nrm(ks[14], (DEPTH, D_MODEL, PEER_HEADS * PEER_DK), D_MODEL ** -0.5),
        'peer_sub_keys': nrm(ks[15], (DEPTH, PEER_HEADS, 2, PEER_NKEYS, PEER_DK // 2), (PEER_DK // 2) ** -0.5),
        'peer_u': nrm(ks[16], (DEPTH, PEER_N_EXPERTS, D_MODEL), D_MODEL ** -0.5),
        'peer_v': nrm(ks[17], (DEPTH, PEER_N_EXPERTS, D_MODEL), 0.5),
        'norm_final': 1.0 + nrm(ks[18], (D_MODEL,), 0.01),
    }


def reference(x_prompt, x_sample, cache_k, cache_v, state_hgrn, page_table, norm_mix, w_in, hgrn_lb_logits,
              hgrn_gnorm, w_a_proj, w_b_proj, w_out, norm_ffn, peer_w_query, peer_sub_keys, peer_u, peer_v, norm_final):
    f32 = jnp.float32
    lb_all = jnp.cumsum(jax.nn.softmax(hgrn_lb_logits.astype(f32), axis=0), axis=0)
    n_pages = page_table.shape[1]
    dec_b = x_sample.shape[0]
    x_p = x_prompt
    x_s = x_sample
    sp_list, ss_list, kp_list, vp_list, ks_list, vs_list = [], [], [], [], [], []
    for layer in range(DEPTH):
        lw = (w_in[layer], lb_all[layer], hgrn_gnorm[layer], w_a_proj[layer], w_b_proj[layer], w_out[layer])
        empty = jnp.zeros((x_p.shape[0], 0, B_HEADS, B_HEAD_DIM), x_p.dtype)
        s0_p = jnp.zeros((x_p.shape[0], A_HEADS, A_DK, A_DV), f32)
        mix_p, s_p, k_p, v_p = token_mixer(rmsnorm(x_p, norm_mix[layer]), empty, empty, s0_p, *lw)
        x_p = x_p + mix_p
        x_p = x_p + peer_ffn(rmsnorm(x_p, norm_ffn[layer]), peer_w_query[layer], peer_sub_keys[layer],
                             peer_u[layer], peer_v[layer])
        k_past = cache_k[layer][page_table].reshape(dec_b, n_pages * PAGE_SIZE, B_HEADS, B_HEAD_DIM)
        v_past = cache_v[layer][page_table].reshape(dec_b, n_pages * PAGE_SIZE, B_HEADS, B_HEAD_DIM)
        mix_s, s_s, k_s, v_s = token_mixer(rmsnorm(x_s, norm_mix[layer]), k_past, v_past, state_hgrn[layer], *lw)
        x_s = x_s + mix_s
        x_s = x_s + peer_ffn(rmsnorm(x_s, norm_ffn[layer]), peer_w_query[layer], peer_sub_keys[layer],
                             peer_u[layer], peer_v[layer])
        sp_list.append(s_p)
        ss_list.append(s_s)
        kp_list.append(k_p)
        vp_list.append(v_p)
        ks_list.append(k_s)
        vs_list.append(v_s)
    y_prompt = rmsnorm(x_p, norm_final)
    y_sample = rmsnorm(x_s, norm_final)
    return (y_prompt, y_sample,
            jnp.stack(sp_list, axis=0).astype(state_hgrn.dtype), jnp.stack(ss_list, axis=0).astype(state_hgrn.dtype),
            jnp.stack(kp_list, axis=0).astype(cache_k.dtype), jnp.stack(vp_list, axis=0).astype(cache_v.dtype),
            jnp.stack(ks_list, axis=0).astype(cache_k.dtype), jnp.stack(vs_list, axis=0).astype(cache_v.dtype))
```

```python
import functools

import jax
import jax.numpy as jnp
from jax import lax
from jax.experimental import pallas as pl
from jax.experimental.pallas import tpu as pltpu

A_HEADS = 8
B_HEADS = 8
MOBA_BLOCK = 256
MOBA_TOPK = 3
PEER_HEADS = 8
PEER_TOPK = 16
RMS_EPS = 1e-6
NEG_INF = -1e30

LANES = 128
SUBLANES = 8
HGRN_CHUNK = 128
HGRN_SUB = 16
VMEM_LIMIT = 48 * 1024 * 1024

_F32 = jnp.float32
_BF16 = jnp.bfloat16
_NT = (((1,), (1,)), ((), ()))


def _params(*sem):
    return pltpu.CompilerParams(dimension_semantics=sem, vmem_limit_bytes=VMEM_LIMIT)


def _dot_nt(a, b):
    return lax.dot_general(a, b, _NT, preferred_element_type=_F32)


def _dot(a, b):
    return jnp.dot(a, b, preferred_element_type=_F32)


def _log2(n):
    assert n > 0 and n & (n - 1) == 0, n
    return n.bit_length() - 1


def _div(x, n):
    return lax.shift_right_logical(x, jnp.int32(_log2(n)))


def _mod(x, n):
    return lax.bitwise_and(x, jnp.int32((1 << _log2(n)) - 1))


def _tile_n(n, *cols):
    for tn in (512, 256, LANES):
        if all(v % tn == 0 for v in (n,) + cols):
            return tn
    raise ValueError((n, cols))


def _rmsnorm_kernel(x_ref, g_ref, o_ref):
    x = x_ref[...]
    y = x * lax.rsqrt(jnp.mean(x * x, axis=-1, keepdims=True) + RMS_EPS)
    o_ref[...] = (y * g_ref[...]).astype(o_ref.dtype)


def _rmsnorm(x, gain, out_dtype):
    t, d = x.shape
    tm = min(t, 512)
    return pl.pallas_call(
        _rmsnorm_kernel,
        grid=(t // tm,),
        in_specs=[pl.BlockSpec((tm, d), lambda i: (i, 0)), pl.BlockSpec((1, d), lambda i: (0, 0))],
        out_specs=pl.BlockSpec((tm, d), lambda i: (i, 0)),
        out_shape=jax.ShapeDtypeStruct((t, d), out_dtype),
        compiler_params=_params("parallel"),
        name="rmsnorm",
    )(x, gain.reshape(1, d).astype(_F32))


def _mm_kernel(a_ref, b_ref, o_ref):
    o_ref[...] = _dot(a_ref[...], b_ref[...]).astype(o_ref.dtype)


def _matmul(a, b, out_dtype=_F32):
    m, k = a.shape
    n = b.shape[1]
    tm = min(m, 1024)
    tn = _tile_n(n)
    return pl.pallas_call(
        _mm_kernel,
        grid=(m // tm, n // tn),
        in_specs=[pl.BlockSpec((tm, k), lambda i, j: (i, 0)), pl.BlockSpec((k, tn), lambda i, j: (0, j))],
        out_specs=pl.BlockSpec((tm, tn), lambda i, j: (i, j)),
        out_shape=jax.ShapeDtypeStruct((m, n), out_dtype),
        compiler_params=_params("parallel", "parallel"),
        name="matmul",
    )(a, b)


def _merge_kernel(oa_ref, ob_ref, wa_ref, wb_ref, ga_ref, gb_ref, o_ref):
    ya = _dot(oa_ref[...].astype(_BF16), wa_ref[...])
    yb = _dot(ob_ref[...].astype(_BF16), wb_ref[...])
    o_ref[...] = (jax.nn.sigmoid(ga_ref[...]) * ya + jax.nn.sigmoid(gb_ref[...]) * yb).astype(o_ref.dtype)


def _merge(oa, ob, wa, wb, proj, ga_col, gb_col, d_model):
    t, ka = oa.shape
    kb = ob.shape[1]
    tm = min(t, 512)
    tn = _tile_n(d_model, ga_col, gb_col)
    return pl.pallas_call(
        _merge_kernel,
        grid=(t // tm, d_model // tn),
        in_specs=[
            pl.BlockSpec((tm, ka), lambda i, j: (i, 0)),
            pl.BlockSpec((tm, kb), lambda i, j: (i, 0)),
            pl.BlockSpec((ka, tn), lambda i, j: (0, j)),
            pl.BlockSpec((kb, tn), lambda i, j: (0, j)),
            pl.BlockSpec((tm, tn), lambda i, j: (i, ga_col // tn + j)),
            pl.BlockSpec((tm, tn), lambda i, j: (i, gb_col // tn + j)),
        ],
        out_specs=pl.BlockSpec((tm, tn), lambda i, j: (i, j)),
        out_shape=jax.ShapeDtypeStruct((t, d_model), _BF16),
        compiler_params=_params("parallel", "parallel"),
        name="merge",
    )(oa, ob, wa, wb, proj, proj)


def _mm_residual_kernel(a_ref, b_ref, r_ref, o_ref):
    o_ref[...] = r_ref[...] + _dot(a_ref[...], b_ref[...])


def _matmul_residual(a, b, res):
    m, k = a.shape
    n = b.shape[1]
    tm = min(m, 1024)
    tn = _tile_n(n)
    return pl.pallas_call(
        _mm_residual_kernel,
        grid=(m // tm, n // tn),
        in_specs=[
            pl.BlockSpec((tm, k), lambda i, j: (i, 0)),
            pl.BlockSpec((k, tn), lambda i, j: (0, j)),
            pl.BlockSpec((tm, tn), lambda i, j: (i, j)),
        ],
        out_specs=pl.BlockSpec((tm, tn), lambda i, j: (i, j)),
        out_shape=jax.ShapeDtypeStruct((m, n), _F32),
        compiler_params=_params("parallel", "parallel"),
        name="matmul_residual",
    )(a, b, res)


def _hgrn_kernel(q_ref, f_ref, i_ref, g_ref, lbl_ref, gn_ref, s0_ref, o_ref, sfin_ref, st_ref, *, layer, rows):
    c_len, sub = HGRN_CHUNK, HGRN_SUB
    ci = pl.program_id(2)

    @pl.when(ci == 0)
    def _():
        st_ref[...] = s0_ref[0, 0].T

    def pad(x):
        if rows == c_len:
            return x
        return jnp.concatenate([x, jnp.zeros((c_len - rows, x.shape[1]), x.dtype)], axis=0)

    logits = lbl_ref[...]
    ex = jnp.exp(logits - jnp.max(logits, axis=0, keepdims=True))
    lb = jnp.sum(ex[: layer + 1], axis=0, keepdims=True) / jnp.sum(ex, axis=0, keepdims=True)

    forget = lb + (1.0 - lb) * jax.nn.sigmoid(f_ref[...])
    logf = pad(jnp.log(forget))
    kk = pad(1.0 - forget)
    q = pad(q_ref[...])
    v = pad(i_ref[...])

    row = lax.broadcasted_iota(jnp.int32, (c_len, LANES), 0)
    b = logf
    shift = 1
    while shift < c_len:
        b = b + jnp.where(row >= shift, pltpu.roll(b, shift, 0), 0.0)
        shift *= 2

    st = st_ref[...]
    o = _dot_nt((q * jnp.exp(b)).astype(_BF16), st.astype(_BF16))

    rowc = lax.broadcasted_iota(jnp.int32, (sub, LANES), 0)
    lane = lax.broadcasted_iota(jnp.int32, (sub, LANES), 1)
    att_rows = []
    for i in range(c_len // sub):
        lo = i * sub
        qi, bi, ki = q[lo:lo + sub], b[lo:lo + sub], kk[lo:lo + sub]
        att = jnp.zeros((sub, LANES), _F32)
        for s in range(sub):
            dec = jnp.exp(jnp.minimum(bi - bi[s:s + 1], 0.0))
            x = jnp.where(rowc >= s, qi * dec * ki[s:s + 1], 0.0)
            att = jnp.where(lane == lo + s, jnp.sum(x, axis=-1, keepdims=True), att)
        if i > 0:
            ref_b = b[lo - 1:lo]
            qt = (qi * jnp.exp(bi - ref_b)).astype(_BF16)
            kp = kk[:lo] * jnp.exp(ref_b - b[:lo])
            kp = jnp.concatenate([kp, jnp.zeros((c_len - lo, LANES), _F32)], axis=0).astype(_BF16)
            att = att + _dot_nt(qt, kp)
        att_rows.append(att)
    att = jnp.concatenate(att_rows, axis=0)
    o = o + _dot(att.astype(_BF16), v.astype(_BF16))

    b_last = b[c_len - 1:c_len]
    k_hat = (kk * jnp.exp(b_last - b)).astype(_BF16)
    st_new = st * jnp.exp(b_last) + _dot(v.T.astype(_BF16), k_hat)
    st_ref[...] = st_new

    @pl.when(ci == pl.num_programs(2) - 1)
    def _():
        sfin_ref[0, 0] = st_new.T

    oo = o[:rows]
    ag = g_ref[...]
    on = oo * lax.rsqrt(jnp.mean(oo * oo, axis=-1, keepdims=True) + RMS_EPS) * gn_ref[...]
    o_ref[...] = on * (ag * jax.nn.sigmoid(ag))


def _hgrn(proj, lb_logits, gnorm, s0, layer, bsz, length, width):
    heads = A_HEADS
    dk = width // heads
    assert dk == LANES
    if length >= HGRN_CHUNK:
        assert length % HGRN_CHUNK == 0
        rows, nc = HGRN_CHUNK, length // HGRN_CHUNK
    else:
        assert length % SUBLANES == 0
        rows, nc = length, 1
    sec = width // dk

    def col(section):
        return lambda b, h, c: (b * nc + c, section * sec + h)

    return pl.pallas_call(
        functools.partial(_hgrn_kernel, layer=layer, rows=rows),
        grid=(bsz, heads, nc),
        in_specs=[
            pl.BlockSpec((rows, dk), col(0)),
            pl.BlockSpec((rows, dk), col(1)),
            pl.BlockSpec((rows, dk), col(2)),
            pl.BlockSpec((rows, dk), col(3)),
            pl.BlockSpec((lb_logits.shape[0], dk), lambda b, h, c: (0, h)),
            pl.BlockSpec((1, dk), lambda b, h, c: (0, h)),
            pl.BlockSpec((1, 1, dk, dk), lambda b, h, c: (b, h, 0, 0)),
        ],
        out_specs=[
            pl.BlockSpec((rows, dk), lambda b, h, c: (b * nc + c, h)),
            pl.BlockSpec((1, 1, dk, dk), lambda b, h, c: (b, h, 0, 0)),
        ],
        out_shape=[
            jax.ShapeDtypeStruct((bsz * length, width), _F32),
            jax.ShapeDtypeStruct((bsz, heads, dk, dk), _F32),
        ],
        scratch_shapes=[pltpu.VMEM((dk, dk), _F32)],
        compiler_params=_params("parallel", "parallel", "arbitrary"),
        name="hgrn2",
    )(proj, proj, proj, proj, lb_logits.astype(_F32), gnorm.reshape(1, width).astype(_F32), s0.astype(_F32))


def _moba_prompt_kernel(q_ref, k_ref, v_ref, o_ref, *, nblk):
    blk = MOBA_BLOCK
    i = pl.program_id(2)
    hd = q_ref.shape[-1]
    q = q_ref[...].astype(_BF16)
    k = k_ref[...]
    vb = v_ref[...].astype(_BF16)
    s = _dot_nt(q, k.astype(_BF16)) * (hd ** -0.5)

    k_mean = jnp.mean(k.reshape(nblk, blk, hd), axis=1)
    k_mean = jnp.concatenate([k_mean, jnp.zeros((LANES - nblk, hd), _F32)], axis=0)
    gate = _dot_nt(q, k_mean.astype(_BF16))

    own = jnp.zeros((blk, 1), jnp.int32) + i
    g = [jnp.where(own > j, gate[:, j:j + 1], NEG_INF) for j in range(nblk)]
    n_sel = min(MOBA_TOPK, nblk)
    sel = []
    for j in range(nblk):
        rank = jnp.zeros((blk, 1), jnp.int32)
        for jp in range(nblk):
            if jp == j:
                continue
            ahead = (g[jp] >= g[j]) if jp < j else (g[jp] > g[j])
            rank = rank + ahead.astype(jnp.int32)
        sel.append((rank < n_sel) & (own > j))

    row = lax.broadcasted_iota(jnp.int32, (blk, blk), 0)
    col = lax.broadcasted_iota(jnp.int32, (blk, blk), 1)
    masked = []
    for j in range(nblk):
        delta = (i - j) * blk
        causal = col <= row + delta
        in_own = col >= delta
        ok = causal & (sel[j] | in_own)
        masked.append(jnp.where(ok, s[:, j * blk:(j + 1) * blk], NEG_INF))
    m = masked[0].max(axis=-1, keepdims=True)
    for j in range(1, nblk):
        m = jnp.maximum(m, masked[j].max(axis=-1, keepdims=True))
    l = jnp.zeros((blk, 1), _F32)
    acc = jnp.zeros((blk, hd), _F32)
    for j in range(nblk):
        p = jnp.exp(masked[j] - m)
        l = l + p.sum(axis=-1, keepdims=True)
        acc = acc + _dot(p.astype(_BF16), vb[j * blk:(j + 1) * blk])
    o_ref[...] = acc / l


def _moba_prompt(proj, bsz, length, q_col, k_col, v_col, width):
    heads = B_HEADS
    hd = width // heads
    assert hd == LANES and length % MOBA_BLOCK == 0
    nblk = length // MOBA_BLOCK
    return pl.pallas_call(
        functools.partial(_moba_prompt_kernel, nblk=nblk),
        grid=(bsz, heads, nblk),
        in_specs=[
            pl.BlockSpec((MOBA_BLOCK, hd), lambda b, h, i: (b * nblk + i, q_col // hd + h)),
            pl.BlockSpec((length, hd), lambda b, h, i: (b, k_col // hd + h)),
            pl.BlockSpec((length, hd), lambda b, h, i: (b, v_col // hd + h)),
        ],
        out_specs=pl.BlockSpec((MOBA_BLOCK, hd), lambda b, h, i: (b * nblk + i, h)),
        out_shape=jax.ShapeDtypeStruct((bsz * length, width), _F32),
        compiler_params=_params("parallel", "parallel", "arbitrary"),
        name="moba_prompt",
    )(proj, proj, proj)


def _moba_sample_kernel(pt_ref, q_ref, kn_ref, vn_ref, kc_ref, vc_ref, o_ref,
                        qbd_ref, s_ref, ks_ref, p_ref, acc_ref, l_ref, *, npages, heads, lq):
    p = pl.program_id(1)
    page, width = kc_ref.shape[1], kc_ref.shape[2]
    hd = width // heads
    rows = heads * lq
    blk_pages = MOBA_BLOCK // page
    nb = npages // blk_pages
    scale = hd ** -0.5

    @pl.when(p == 0)
    def _():
        qt = jnp.concatenate([q_ref[...]] * heads, axis=0)
        rh = _div(lax.broadcasted_iota(jnp.int32, (rows, width), 0), lq)
        ch = _div(lax.broadcasted_iota(jnp.int32, (rows, width), 1), hd)
        qbd_ref[...] = jnp.where(rh == ch, qt, 0.0).astype(_BF16)

    @pl.when(p < npages)
    def _():
        kp = kc_ref[0]
        s_ref[p] = _dot_nt(qbd_ref[...], kp.astype(_BF16))
        ks_ref[p] = kp.reshape(page // SUBLANES, SUBLANES, width).sum(axis=0)

    @pl.when(p == npages)
    def _():
        qbd = qbd_ref[...]
        k_sum = ks_ref[...].reshape(nb, blk_pages * SUBLANES, width).sum(axis=1)
        k_mean = k_sum * (1.0 / MOBA_BLOCK)
        gate = _dot_nt(qbd, k_mean.astype(_BF16))
        lane = lax.broadcasted_iota(jnp.int32, (rows, nb), 1)
        sel = jnp.zeros((rows, nb), _F32)
        for _ in range(MOBA_TOPK):
            best = jnp.max(gate, axis=1, keepdims=True)
            idx = jnp.min(jnp.where(gate == best, lane, nb), axis=1, keepdims=True)
            hit = lane == idx
            sel = jnp.where(hit, 1.0, sel)
            gate = jnp.where(hit, -jnp.inf, gate)

        pad = 2 * SUBLANES - lq
        kn = jnp.concatenate([kn_ref[...], jnp.zeros((pad, width), _F32)], axis=0).astype(_BF16)
        vn = jnp.concatenate([vn_ref[...], jnp.zeros((pad, width), _F32)], axis=0).astype(_BF16)
        s_own = _dot_nt(qbd, kn) * scale
        qi = _mod(lax.broadcasted_iota(jnp.int32, s_own.shape, 0), lq)
        kj = lax.broadcasted_iota(jnp.int32, s_own.shape, 1)
        s_own = jnp.where((kj <= qi) & (kj < lq), s_own, NEG_INF)

        m = jnp.max(s_own, axis=1, keepdims=True)
        for pg in range(npages):
            chosen = sel[:, pg // blk_pages:pg // blk_pages + 1] > 0.0
            m = jnp.maximum(m, jnp.max(jnp.where(chosen, s_ref[pg] * scale, NEG_INF), axis=1, keepdims=True))
        p_own = jnp.exp(s_own - m)
        l = jnp.sum(p_own, axis=1, keepdims=True)
        for pg in range(npages):
            chosen = sel[:, pg // blk_pages:pg // blk_pages + 1] > 0.0
            w = jnp.exp(jnp.where(chosen, s_ref[pg] * scale, NEG_INF) - m)
            l = l + jnp.sum(w, axis=1, keepdims=True)
            p_ref[pg] = w.astype(_BF16)
        l_ref[...] = l
        acc_ref[...] = _dot(p_own.astype(_BF16), vn)

    @pl.when(p >= npages)
    def _():
        acc_ref[...] += _dot(p_ref[p - npages], vc_ref[0].astype(_BF16))

    @pl.when(p == 2 * npages - 1)
    def _():
        out = acc_ref[...] / l_ref[...]
        for h in range(heads):
            o_ref[:, h * hd:(h + 1) * hd] = out[h * lq:(h + 1) * lq, h * hd:(h + 1) * hd]


def _moba_sample(proj, cache_k, cache_v, page_table, bsz, lq, q_col, k_col, v_col, width):
    heads = B_HEADS
    n_phys, page = cache_k.shape[0], cache_k.shape[1]
    npages = page_table.shape[1]
    past = npages * page
    assert MOBA_BLOCK % page == 0 and past % MOBA_BLOCK == 0 and lq % SUBLANES == 0 and lq <= 2 * SUBLANES
    assert past // MOBA_BLOCK >= MOBA_TOPK
    rows = heads * lq
    kc = cache_k.reshape(n_phys, page, width)
    vc = cache_v.reshape(n_phys, page, width)
    grid_spec = pltpu.PrefetchScalarGridSpec(
        num_scalar_prefetch=1,
        grid=(bsz, 2 * npages),
        in_specs=[
            pl.BlockSpec((lq, width), lambda b, p, pt: (b, q_col // width)),
            pl.BlockSpec((lq, width), lambda b, p, pt: (b, k_col // width)),
            pl.BlockSpec((lq, width), lambda b, p, pt: (b, v_col // width)),
            pl.BlockSpec((1, page, width), lambda b, p, pt: (pt[b, jnp.minimum(p, npages - 1)], 0, 0)),
            pl.BlockSpec((1, page, width), lambda b, p, pt: (pt[b, jnp.maximum(p - npages, 0)], 0, 0)),
        ],
        out_specs=pl.BlockSpec((lq, width), lambda b, p, pt: (b, 0)),
        scratch_shapes=[
            pltpu.VMEM((rows, width), _BF16),
            pltpu.VMEM((npages, rows, page), _F32),
            pltpu.VMEM((npages, SUBLANES, width), _F32),
            pltpu.VMEM((npages, rows, page), _BF16),
            pltpu.VMEM((rows, width), _F32),
            pltpu.VMEM((rows, 1), _F32),
        ],
    )
    return pl.pallas_call(
        functools.partial(_moba_sample_kernel, npages=npages, heads=heads, lq=lq),
        grid_spec=grid_spec,
        out_shape=jax.ShapeDtypeStruct((bsz * lq, width), _F32),
        compiler_params=_params("parallel", "arbitrary"),
        name="moba_sample",
    )(page_table.astype(jnp.int32), proj, proj, proj, kc, vc)


def _top_rows(x, k):
    r, t = x.shape
    rowi = lax.broadcasted_iota(jnp.int32, (r, t), 0)
    slot = lax.broadcasted_iota(jnp.int32, (k, t), 0)
    vals = jnp.zeros((k, t), _F32)
    idxs = jnp.zeros((k, t), jnp.int32)
    for n in range(k):
        best = jnp.max(x, axis=0, keepdims=True)
        idx = jnp.min(jnp.where(x == best, rowi, r), axis=0, keepdims=True)
        vals = jnp.where(slot == n, best, vals)
        idxs = jnp.where(slot == n, idx, idxs)
        x = jnp.where(rowi == idx, -jnp.inf, x)
    return vals, idxs


def _peer_retrieve_kernel(q_ref, sk_ref, g_ref, i1_ref, i2_ref):
    topk = PEER_TOPK
    dh = sk_ref.shape[-1]
    tops = []
    for a in range(2):
        qa = q_ref[:, a * dh:(a + 1) * dh].astype(_BF16)
        tops.append(_top_rows(_dot_nt(sk_ref[0, a], qa), topk))
    (s1, k1), (s2, k2) = tops
    cand = jnp.concatenate([s1[a:a + 1] + s2 for a in range(topk)], axis=0)
    best_s, best_pos = _top_rows(cand, topk)
    a_sel, b_sel = _div(best_pos, topk), _mod(best_pos, topk)
    i1 = jnp.zeros_like(best_pos)
    i2 = jnp.zeros_like(best_pos)
    for a in range(topk):
        i1 = jnp.where(a_sel == a, k1[a:a + 1], i1)
        i2 = jnp.where(b_sel == a, k2[a:a + 1], i2)
    e = jnp.exp(best_s - best_s[0:1])
    g_ref[...] = e / jnp.sum(e, axis=0, keepdims=True)
    i1_ref[...] = i1.astype(_F32)
    i2_ref[...] = i2.astype(_F32)


def _peer_retrieve(q, sub_keys, tm=256):
    t = q.shape[0]
    heads, _, nkeys, dh = sub_keys.shape
    tm = min(tm, t)
    spec = pl.BlockSpec((PEER_TOPK, tm), lambda i, h: (h, i))
    shape = jax.ShapeDtypeStruct((heads * PEER_TOPK, t), _F32)
    return pl.pallas_call(
        _peer_retrieve_kernel,
        grid=(t // tm, heads),
        in_specs=[
            pl.BlockSpec((tm, 2 * dh), lambda i, h: (i, h)),
            pl.BlockSpec((1, 2, nkeys, dh), lambda i, h: (h, 0, 0, 0)),
        ],
        out_specs=[spec, spec, spec],
        out_shape=[shape, shape, shape],
        compiler_params=_params("parallel", "parallel"),
        name="peer_retrieve",
    )(q, sub_keys.astype(_BF16))


def _peer_weights_kernel(g_ref, i1_ref, i2_ref, w_ref, gt_ref, i1t_ref, i2t_ref, scr_ref, *, tm, stride):
    nk = LANES
    gt_ref[...] = g_ref[...].T
    i1t_ref[...] = i1_ref[...].T
    i2t_ref[...] = i2_ref[...].T
    sub = lax.broadcasted_iota(jnp.int32, (nk, nk), 0).astype(_F32)

    def body(n, carry):
        r1 = jnp.broadcast_to(i1t_ref[pl.ds(n, 1), :], (nk, nk))
        r2 = jnp.broadcast_to(i2t_ref[pl.ds(n, 1), :], (nk, nk))
        rg = jnp.broadcast_to(gt_ref[pl.ds(n, 1), :], (nk, nk))
        m1 = jnp.where(sub == r1, 1.0, 0.0).astype(_BF16)
        m2 = jnp.where(sub == r2, rg, 0.0).astype(_BF16)
        scr_ref[pl.ds(n, nk, stride=stride), :] = _dot_nt(m1, m2)
        return carry

    lax.fori_loop(0, tm, body, 0)
    for j in range(nk):
        w_ref[:, j * nk:(j + 1) * nk] = scr_ref[j * stride:j * stride + tm, :].astype(w_ref.dtype)


def _peer_weights(g, i1, i2, nkeys, tm=256):
    picks, t = g.shape
    assert picks == LANES and nkeys == LANES
    tm = min(tm, t)
    stride = tm + SUBLANES
    spec = pl.BlockSpec((picks, tm), lambda i: (0, i))
    return pl.pallas_call(
        functools.partial(_peer_weights_kernel, tm=tm, stride=stride),
        grid=(t // tm,),
        in_specs=[spec, spec, spec],
        out_specs=pl.BlockSpec((tm, nkeys * nkeys), lambda i: (i, 0)),
        out_shape=jax.ShapeDtypeStruct((t, nkeys * nkeys), _BF16),
        scratch_shapes=[
            pltpu.VMEM((tm, picks), _F32),
            pltpu.VMEM((tm, picks), _F32),
            pltpu.VMEM((tm, picks), _F32),
            pltpu.VMEM((nkeys * stride, nkeys), _F32),
        ],
        compiler_params=_params("parallel"),
        name="peer_weights",
    )(g, i1, i2)


def _peer_ffn_kernel(xn_ref, w_ref, u_ref, v_ref, r_ref, gf_ref, o_ref, acc_ref, *, final):
    j = pl.program_id(1)

    @pl.when(j == 0)
    def _():
        acc_ref[...] = jnp.zeros_like(acc_ref)

    h = _dot_nt(xn_ref[...], u_ref[...])
    w = w_ref[...].astype(_F32)
    coef = jnp.where(w != 0.0, w * jax.nn.gelu(h), 0.0)
    acc_ref[...] += _dot(coef.astype(_BF16), v_ref[...])

    @pl.when(j == pl.num_programs(1) - 1)
    def _():
        x = r_ref[...] + acc_ref[...]
        if final:
            x = x * lax.rsqrt(jnp.mean(x * x, axis=-1, keepdims=True) + RMS_EPS) * gf_ref[...]
        o_ref[...] = x


def _peer_ffn(xn, w, u, v, res, gain_final, final, tm=512, te=512):
    t, d = xn.shape
    n_exp = u.shape[0]
    tm = min(tm, t)
    return pl.pallas_call(
        functools.partial(_peer_ffn_kernel, final=final),
        grid=(t // tm, n_exp // te),
        in_specs=[
            pl.BlockSpec((tm, d), lambda i, j: (i, 0)),
            pl.BlockSpec((tm, te), lambda i, j: (i, j)),
            pl.BlockSpec((te, d), lambda i, j: (j, 0)),
            pl.BlockSpec((te, d), lambda i, j: (j, 0)),
            pl.BlockSpec((tm, d), lambda i, j: (i, 0)),
            pl.BlockSpec((1, d), lambda i, j: (0, 0)),
        ],
        out_specs=pl.BlockSpec((tm, d), lambda i, j: (i, 0)),
        out_shape=jax.ShapeDtypeStruct((t, d), _F32),
        scratch_shapes=[pltpu.VMEM((tm, d), _F32)],
        compiler_params=_params("parallel", "arbitrary"),
        name="peer_ffn",
    )(xn, w, u, v, res, gain_final.reshape(1, d).astype(_F32))


def _layer(x, bsz, length, s0, past, lw, layer, final):
    d_model = x.shape[1]
    a_width = lw["gnorm"].shape[-1]
    b_width = lw["w_b"].shape[0]
    hn = _rmsnorm(x, lw["norm_mix"], _BF16)
    proj = _matmul(hn, lw["w_in"])
    b_q = 4 * a_width
    b_k, b_v = b_q + b_width, b_q + 2 * b_width
    g_a = b_q + 3 * b_width
    g_b = g_a + d_model

    o_a, s_new = _hgrn(proj, lw["lb_logits"], lw["gnorm"], s0, layer, bsz, length, a_width)
    if past is None:
        o_b = _moba_prompt(proj, bsz, length, b_q, b_k, b_v, b_width)
    else:
        o_b = _moba_sample(proj, past[0], past[1], past[2], bsz, length, b_q, b_k, b_v, b_width)
    merged = _merge(o_a, o_b, lw["w_a"], lw["w_b"], proj, g_a, g_b, d_model)
    x = _matmul_residual(merged, lw["w_out"], x)

    xn = _rmsnorm(x, lw["norm_ffn"], _BF16)
    q = _matmul(xn, lw["w_query"])
    gates, i1, i2 = _peer_retrieve(q, lw["sub_keys"])
    w = _peer_weights(gates, i1, i2, lw["sub_keys"].shape[2])
    x = _peer_ffn(xn, w, lw["peer_u"], lw["peer_v"], x, lw["norm_final"], final)

    k_new = proj[:, b_k:b_k + b_width].reshape(bsz, length, B_HEADS, b_width // B_HEADS)
    v_new = proj[:, b_v:b_v + b_width].reshape(bsz, length, B_HEADS, b_width // B_HEADS)
    return x, s_new, k_new, v_new


def kernel(x_prompt, x_sample, cache_k, cache_v, state_hgrn, page_table, norm_mix, w_in, hgrn_lb_logits,
           hgrn_gnorm, w_a_proj, w_b_proj, w_out, norm_ffn, peer_w_query, peer_sub_keys, peer_u, peer_v,
           norm_final):
    depth = w_in.shape[0]
    bp, lp, d_model = x_prompt.shape
    bs, ls, _ = x_sample.shape
    x_p = x_prompt.reshape(bp * lp, d_model)
    x_s = x_sample.reshape(bs * ls, d_model)
    outs = [[] for _ in range(6)]
    for layer in range(depth):
        lw = {
            "norm_mix": norm_mix[layer], "w_in": w_in[layer].astype(_BF16), "lb_logits": hgrn_lb_logits,
            "gnorm": hgrn_gnorm[layer], "w_a": w_a_proj[layer].astype(_BF16), "w_b": w_b_proj[layer].astype(_BF16),
            "w_out": w_out[layer].astype(_BF16), "norm_ffn": norm_ffn[layer],
            "w_query": peer_w_query[layer].astype(_BF16), "sub_keys": peer_sub_keys[layer],
            "peer_u": peer_u[layer].astype(_BF16), "peer_v": peer_v[layer].astype(_BF16), "norm_final": norm_final,
        }
        final = layer == depth - 1
        s0_p = jnp.zeros((bp,) + state_hgrn.shape[2:], _F32)
        x_p, s_p, k_p, v_p = _layer(x_p, bp, lp, s0_p, None, lw, layer, final)
        past = (cache_k[layer], cache_v[layer], page_table)
        x_s, s_s, k_s, v_s = _layer(x_s, bs, ls, state_hgrn[layer], past, lw, layer, final)
        for lst, val in zip(outs, (s_p, s_s, k_p, v_p, k_s, v_s)):
            lst.append(val)
    sp, ss, kp, vp, ks, vs = (jnp.stack(lst, axis=0) for lst in outs)
    return (x_p.reshape(bp, lp, d_model), x_s.reshape(bs, ls, d_model),
            sp.astype(state_hgrn.dtype), ss.astype(state_hgrn.dtype),
            kp.astype(cache_k.dtype), vp.astype(cache_v.dtype), ks.astype(cache_k.dtype), vs.astype(cache_v.dtype))
```

```python
import functools

import jax
import jax.numpy as jnp
from jax import lax
from jax.experimental import pallas as pl
from jax.experimental.pallas import tpu as pltpu

A_HEADS = 8
B_HEADS = 8
MOBA_BLOCK = 256
MOBA_TOPK = 3
PEER_HEADS = 8
PEER_TOPK = 16
RMS_EPS = 1e-6
NEG_INF = -1e30

LANES = 128
SUBLANES = 8
HGRN_CHUNK = 128
HGRN_SUB = 16
VMEM_LIMIT = 48 * 1024 * 1024

_F32 = jnp.float32
_BF16 = jnp.bfloat16
_NT = (((1,), (1,)), ((), ()))


def _params(*sem):
    return pltpu.CompilerParams(dimension_semantics=sem, vmem_limit_bytes=VMEM_LIMIT)


def _dot_nt(a, b):
    return lax.dot_general(a, b, _NT, preferred_element_type=_F32)


def _dot(a, b):
    return jnp.dot(a, b, preferred_element_type=_F32)


def _log2(n):
    assert n > 0 and n & (n - 1) == 0, n
    return n.bit_length() - 1


def _div(x, n):
    return lax.shift_right_logical(x, jnp.int32(_log2(n)))


def _mod(x, n):
    return lax.bitwise_and(x, jnp.int32((1 << _log2(n)) - 1))


def _tile_n(n, *cols):
    for tn in (512, 256, LANES):
        if all(v % tn == 0 for v in (n,) + cols):
            return tn
    raise ValueError((n, cols))


def _rmsnorm_kernel(x_ref, g_ref, o_ref):
    x = x_ref[...]
    y = x * lax.rsqrt(jnp.mean(x * x, axis=-1, keepdims=True) + RMS_EPS)
    o_ref[...] = (y * g_ref[...]).astype(o_ref.dtype)


def _rmsnorm(x, gain, out_dtype):
    t, d = x.shape
    tm = min(t, 512)
    return pl.pallas_call(
        _rmsnorm_kernel,
        grid=(t // tm,),
        in_specs=[pl.BlockSpec((tm, d), lambda i: (i, 0)), pl.BlockSpec((1, d), lambda i: (0, 0))],
        out_specs=pl.BlockSpec((tm, d), lambda i: (i, 0)),
        out_shape=jax.ShapeDtypeStruct((t, d), out_dtype),
        compiler_params=_params("parallel"),
        name="rmsnorm",
    )(x, gain.reshape(1, d).astype(_F32))


def _mm_kernel(a_ref, b_ref, o_ref):
    o_ref[...] = _dot(a_ref[...], b_ref[...]).astype(o_ref.dtype)


def _matmul(a, b, out_dtype=_F32):
    m, k = a.shape
    n = b.shape[1]
    tm = min(m, 1024)
    tn = _tile_n(n)
    return pl.pallas_call(
        _mm_kernel,
        grid=(m // tm, n // tn),
        in_specs=[pl.BlockSpec((tm, k), lambda i, j: (i, 0)), pl.BlockSpec((k, tn), lambda i, j: (0, j))],
        out_specs=pl.BlockSpec((tm, tn), lambda i, j: (i, j)),
        out_shape=jax.ShapeDtypeStruct((m, n), out_dtype),
        compiler_params=_params("parallel", "parallel"),
        name="matmul",
    )(a, b)


def _merge_kernel(oa_ref, ob_ref, wa_ref, wb_ref, ga_ref, gb_ref, o_ref):
    ya = _dot(oa_ref[...].astype(_BF16), wa_ref[...])
    yb = _dot(ob_ref[...].astype(_BF16), wb_ref[...])
    o_ref[...] = (jax.nn.sigmoid(ga_ref[...]) * ya + jax.nn.sigmoid(gb_ref[...]) * yb).astype(o_ref.dtype)


def _merge(oa, ob, wa, wb, proj, ga_col, gb_col, d_model):
    t, ka = oa.shape
    kb = ob.shape[1]
    tm = min(t, 512)
    tn = _tile_n(d_model, ga_col, gb_col)
    return pl.pallas_call(
        _merge_kernel,
        grid=(t // tm, d_model // tn),
        in_specs=[
            pl.BlockSpec((tm, ka), lambda i, j: (i, 0)),
            pl.BlockSpec((tm, kb), lambda i, j: (i, 0)),
            pl.BlockSpec((ka, tn), lambda i, j: (0, j)),
            pl.BlockSpec((kb, tn), lambda i, j: (0, j)),
            pl.BlockSpec((tm, tn), lambda i, j: (i, ga_col // tn + j)),
            pl.BlockSpec((tm, tn), lambda i, j: (i, gb_col // tn + j)),
        ],
        out_specs=pl.BlockSpec((tm, tn), lambda i, j: (i, j)),
        out_shape=jax.ShapeDtypeStruct((t, d_model), _BF16),
        compiler_params=_params("parallel", "parallel"),
        name="merge",
    )(oa, ob, wa, wb, proj, proj)


def _mm_residual_kernel(a_ref, b_ref, r_ref, o_ref):
    o_ref[...] = r_ref[...] + _dot(a_ref[...], b_ref[...])


def _matmul_residual(a, b, res):
    m, k = a.shape
    n = b.shape[1]
    tm = min(m, 1024)
    tn = _tile_n(n)
    return pl.pallas_call(
        _mm_residual_kernel,
        grid=(m // tm, n // tn),
        in_specs=[
            pl.BlockSpec((tm, k), lambda i, j: (i, 0)),
            pl.BlockSpec((k, tn), lambda i, j: (0, j)),
            pl.BlockSpec((tm, tn), lambda i, j: (i, j)),
        ],
        out_specs=pl.BlockSpec((tm, tn), lambda i, j: (i, j)),
        out_shape=jax.ShapeDtypeStruct((m, n), _F32),
        compiler_params=_params("parallel", "parallel"),
        name="matmul_residual",
    )(a, b, res)


def _hgrn_kernel(q_ref, f_ref, i_ref, g_ref, lbl_ref, gn_ref, s0_ref, o_ref, sfin_ref, st_ref, *, layer, rows):
    c_len, sub = HGRN_CHUNK, HGRN_SUB
    ci = pl.program_id(2)

    @pl.when(ci == 0)
    def _():
        st_ref[...] = s0_ref[0, 0].T

    def pad(x):
        if rows == c_len:
            return x
        return jnp.concatenate([x, jnp.zeros((c_len - rows, x.shape[1]), x.dtype)], axis=0)

    logits = lbl_ref[...]
    ex = jnp.exp(logits - jnp.max(logits, axis=0, keepdims=True))
    lb = jnp.sum(ex[: layer + 1], axis=0, keepdims=True) / jnp.sum(ex, axis=0, keepdims=True)

    forget = lb + (1.0 - lb) * jax.nn.sigmoid(f_ref[...])
    logf = pad(jnp.log(forget))
    kk = pad(1.0 - forget)
    q = pad(q_ref[...])
    v = pad(i_ref[...])

    row = lax.broadcasted_iota(jnp.int32, (c_len, LANES), 0)
    b = logf
    shift = 1
    while shift < c_len:
        b = b + jnp.where(row >= shift, pltpu.roll(b, shift, 0), 0.0)
        shift *= 2

    st = st_ref[...]
    o = _dot_nt((q * jnp.exp(b)).astype(_BF16), st.astype(_BF16))

    rowc = lax.broadcasted_iota(jnp.int32, (sub, LANES), 0)
    lane = lax.broadcasted_iota(jnp.int32, (sub, LANES), 1)
    att_rows = []
    for i in range(c_len // sub):
        lo = i * sub
        qi, bi, ki = q[lo:lo + sub], b[lo:lo + sub], kk[lo:lo + sub]
        att = jnp.zeros((sub, LANES), _F32)
        for s in range(sub):
            dec = jnp.exp(jnp.minimum(bi - bi[s:s + 1], 0.0))
            x = jnp.where(rowc >= s, qi * dec * ki[s:s + 1], 0.0)
            att = jnp.where(lane == lo + s, jnp.sum(x, axis=-1, keepdims=True), att)
        if i > 0:
            ref_b = b[lo - 1:lo]
            qt = (qi * jnp.exp(bi - ref_b)).astype(_BF16)
            kp = kk[:lo] * jnp.exp(ref_b - b[:lo])
            kp = jnp.concatenate([kp, jnp.zeros((c_len - lo, LANES), _F32)], axis=0).astype(_BF16)
            att = att + _dot_nt(qt, kp)
        att_rows.append(att)
    att = jnp.concatenate(att_rows, axis=0)
    o = o + _dot(att.astype(_BF16), v.astype(_BF16))

    b_last = b[c_len - 1:c_len]
    k_hat = (kk * jnp.exp(b_last - b)).astype(_BF16)
    st_new = st * jnp.exp(b_last) + _dot(v.T.astype(_BF16), k_hat)
    st_ref[...] = st_new

    @pl.when(ci == pl.num_programs(2) - 1)
    def _():
        sfin_ref[0, 0] = st_new.T

    oo = o[:rows]
    ag = g_ref[...]
    on = oo * lax.rsqrt(jnp.mean(oo * oo, axis=-1, keepdims=True) + RMS_EPS) * gn_ref[...]
    o_ref[...] = on * (ag * jax.nn.sigmoid(ag))


def _hgrn(proj, lb_logits, gnorm, s0, layer, bsz, length, width):
    heads = A_HEADS
    dk = width // heads
    assert dk == LANES
    if length >= HGRN_CHUNK:
        assert length % HGRN_CHUNK == 0
        rows, nc = HGRN_CHUNK, length // HGRN_CHUNK
    else:
        assert length % SUBLANES == 0
        rows, nc = length, 1
    sec = width // dk

    def col(section):
        return lambda b, h, c: (b * nc + c, section * sec + h)

    return pl.pallas_call(
        functools.partial(_hgrn_kernel, layer=layer, rows=rows),
        grid=(bsz, heads, nc),
        in_specs=[
            pl.BlockSpec((rows, dk), col(0)),
            pl.BlockSpec((rows, dk), col(1)),
            pl.BlockSpec((rows, dk), col(2)),
            pl.BlockSpec((rows, dk), col(3)),
            pl.BlockSpec((lb_logits.shape[0], dk), lambda b, h, c: (0, h)),
            pl.BlockSpec((1, dk), lambda b, h, c: (0, h)),
            pl.BlockSpec((1, 1, dk, dk), lambda b, h, c: (b, h, 0, 0)),
        ],
        out_specs=[
            pl.BlockSpec((rows, dk), lambda b, h, c: (b * nc + c, h)),
            pl.BlockSpec((1, 1, dk, dk), lambda b, h, c: (b, h, 0, 0)),
        ],
        out_shape=[
            jax.ShapeDtypeStruct((bsz * length, width), _F32),
            jax.ShapeDtypeStruct((bsz, heads, dk, dk), _F32),
        ],
        scratch_shapes=[pltpu.VMEM((dk, dk), _F32)],
        compiler_params=_params("parallel", "parallel", "arbitrary"),
        name="hgrn2",
    )(proj, proj, proj, proj, lb_logits.astype(_F32), gnorm.reshape(1, width).astype(_F32), s0.astype(_F32))


def _moba_prompt_kernel(q_ref, k_ref, v_ref, o_ref, *, nblk):
    blk = MOBA_BLOCK
    i = pl.program_id(2)
    hd = q_ref.shape[-1]
    q = q_ref[...].astype(_BF16)
    k = k_ref[...]
    vb = v_ref[...].astype(_BF16)
    s = _dot_nt(q, k.astype(_BF16)) * (hd ** -0.5)

    k_mean = jnp.mean(k.reshape(nblk, blk, hd), axis=1)
    k_mean = jnp.concatenate([k_mean, jnp.zeros((LANES - nblk, hd), _F32)], axis=0)
    gate = _dot_nt(k_mean.astype(_BF16), q)

    nrow = -(-nblk // SUBLANES) * SUBLANES
    blk_id = lax.broadcasted_iota(jnp.int32, (nrow, blk), 0)
    g = jnp.where(blk_id < i, gate[:nrow], NEG_INF)
    rank = jnp.zeros((nrow, blk), jnp.int32)
    for jp in range(nblk):
        other = g[jp:jp + 1]
        ahead = (other > g) | ((other == g) & (blk_id > jp))
        rank = rank + ahead.astype(jnp.int32)
    picked = jnp.where((rank < min(MOBA_TOPK, nblk)) & (blk_id < i), 1.0, 0.0)
    picked = jnp.concatenate([picked, jnp.zeros((LANES - nrow, blk), _F32)], axis=0).T
    sel = [picked[:, j:j + 1] > 0.0 for j in range(nblk)]

    row = lax.broadcasted_iota(jnp.int32, (blk, blk), 0)
    col = lax.broadcasted_iota(jnp.int32, (blk, blk), 1)
    masked = []
    for j in range(nblk):
        delta = (i - j) * blk
        causal = col <= row + delta
        in_own = col >= delta
        ok = causal & (sel[j] | in_own)
        masked.append(jnp.where(ok, s[:, j * blk:(j + 1) * blk], NEG_INF))
    m = masked[0].max(axis=-1, keepdims=True)
    for j in range(1, nblk):
        m = jnp.maximum(m, masked[j].max(axis=-1, keepdims=True))
    l = jnp.zeros((blk, 1), _F32)
    acc = jnp.zeros((blk, hd), _F32)
    for j in range(nblk):
        p = jnp.exp(masked[j] - m)
        l = l + p.sum(axis=-1, keepdims=True)
        acc = acc + _dot(p.astype(_BF16), vb[j * blk:(j + 1) * blk])
    o_ref[...] = acc / l


def _moba_prompt(proj, bsz, length, q_col, k_col, v_col, width):
    heads = B_HEADS
    hd = width // heads
    assert hd == LANES and length % MOBA_BLOCK == 0
    nblk = length // MOBA_BLOCK
    assert nblk <= LANES
    return pl.pallas_call(
        functools.partial(_moba_prompt_kernel, nblk=nblk),
        grid=(bsz, heads, nblk),
        in_specs=[
            pl.BlockSpec((MOBA_BLOCK, hd), lambda b, h, i: (b * nblk + i, q_col // hd + h)),
            pl.BlockSpec((length, hd), lambda b, h, i: (b, k_col // hd + h)),
            pl.BlockSpec((length, hd), lambda b, h, i: (b, v_col // hd + h)),
        ],
        out_specs=pl.BlockSpec((MOBA_BLOCK, hd), lambda b, h, i: (b * nblk + i, h)),
        out_shape=jax.ShapeDtypeStruct((bsz * length, width), _F32),
        compiler_params=_params("parallel", "parallel", "arbitrary"),
        name="moba_prompt",
    )(proj, proj, proj)


def _moba_sample_kernel(pt_ref, q_ref, kn_ref, vn_ref, *refs, npages, heads, lq, group):
    kc_refs, vc_refs = refs[:group], refs[group:2 * group]
    o_ref, qbd_ref, s_ref, ks_ref, km_ref, p_ref, acc_ref, l_ref = refs[2 * group:]
    p = pl.program_id(1)
    nsteps = npages // group
    hd = kc_refs[0].shape[2]
    page = kc_refs[0].shape[1] // heads
    width = heads * hd
    rows = heads * lq
    blk_pages = MOBA_BLOCK // page
    nb = npages // blk_pages
    scale = hd ** -0.5

    def by_token(ref):
        return jnp.concatenate([ref[0, pl.ds(h, page, stride=heads), :] for h in range(heads)], axis=1)

    @pl.when(p == 0)
    def _():
        qt = jnp.concatenate([q_ref[...]] * heads, axis=0)
        rh = _div(lax.broadcasted_iota(jnp.int32, (rows, width), 0), lq)
        ch = _div(lax.broadcasted_iota(jnp.int32, (rows, width), 1), hd)
        qbd_ref[...] = jnp.where(rh == ch, qt, 0.0).astype(_BF16)

    @pl.when(p < nsteps)
    def _():
        for g in range(group):
            pg = p * group + g
            s_ref[pg] = _dot_nt(qbd_ref[...], by_token(kc_refs[g]).astype(_BF16))
            ks_ref[pg] = kc_refs[g][0].reshape(page, heads, hd).sum(axis=0)

    @pl.when(p == nsteps)
    def _():
        qbd = qbd_ref[...]
        k_sum = ks_ref[...].reshape(nb, blk_pages, heads, hd).sum(axis=1)
        km_ref[...] = k_sum.reshape(nb * heads, hd) * (1.0 / MOBA_BLOCK)
        k_mean = jnp.concatenate([km_ref[pl.ds(h, nb, stride=heads), :] for h in range(heads)], axis=1)
        gate = _dot_nt(qbd, k_mean.astype(_BF16))
        lane = lax.broadcasted_iota(jnp.int32, (rows, nb), 1)
        sel = jnp.zeros((rows, nb), _F32)
        for _ in range(MOBA_TOPK):
            best = jnp.max(gate, axis=1, keepdims=True)
            idx = jnp.min(jnp.where(gate == best, lane, nb), axis=1, keepdims=True)
            hit = lane == idx
            sel = jnp.where(hit, 1.0, sel)
            gate = jnp.where(hit, -jnp.inf, gate)

        pad = 2 * SUBLANES - lq
        kn = jnp.concatenate([kn_ref[...], jnp.zeros((pad, width), _F32)], axis=0).astype(_BF16)
        vn = jnp.concatenate([vn_ref[...], jnp.zeros((pad, width), _F32)], axis=0).astype(_BF16)
        s_own = _dot_nt(qbd, kn) * scale
        qi = _mod(lax.broadcasted_iota(jnp.int32, s_own.shape, 0), lq)
        kj = lax.broadcasted_iota(jnp.int32, s_own.shape, 1)
        s_own = jnp.where((kj <= qi) & (kj < lq), s_own, NEG_INF)

        m = jnp.max(s_own, axis=1, keepdims=True)
        for pg in range(npages):
            chosen = sel[:, pg // blk_pages:pg // blk_pages + 1] > 0.0
            m = jnp.maximum(m, jnp.max(jnp.where(chosen, s_ref[pg] * scale, NEG_INF), axis=1, keepdims=True))
        p_own = jnp.exp(s_own - m)
        l = jnp.sum(p_own, axis=1, keepdims=True)
        for pg in range(npages):
            chosen = sel[:, pg // blk_pages:pg // blk_pages + 1] > 0.0
            w = jnp.exp(jnp.where(chosen, s_ref[pg] * scale, NEG_INF) - m)
            l = l + jnp.sum(w, axis=1, keepdims=True)
            p_ref[pg] = w.astype(_BF16)
        l_ref[...] = l
        acc_ref[...] = _dot(p_own.astype(_BF16), vn)

    @pl.when(p >= nsteps)
    def _():
        acc = acc_ref[...]
        for g in range(group):
            acc = acc + _dot(p_ref[(p - nsteps) * group + g], by_token(vc_refs[g]).astype(_BF16))
        acc_ref[...] = acc

    @pl.when(p == 2 * nsteps - 1)
    def _():
        out = acc_ref[...] / l_ref[...]
        for h in range(heads):
            o_ref[:, h * hd:(h + 1) * hd] = out[h * lq:(h + 1) * lq, h * hd:(h + 1) * hd]


def _moba_sample(proj, cache_k, cache_v, page_table, bsz, lq, q_col, k_col, v_col, width):
    heads = B_HEADS
    hd = width // heads
    n_phys, page = cache_k.shape[0], cache_k.shape[1]
    npages = page_table.shape[1]
    past = npages * page
    assert MOBA_BLOCK % page == 0 and past % MOBA_BLOCK == 0 and lq % SUBLANES == 0 and lq <= 2 * SUBLANES
    assert past // MOBA_BLOCK >= MOBA_TOPK and heads % SUBLANES == 0
    group = max(g for g in (1, 2, 4, 8) if npages % g == 0)
    nsteps = npages // group
    rows = heads * lq
    kc = cache_k.reshape(n_phys, page * heads, hd)
    vc = cache_v.reshape(n_phys, page * heads, hd)

    def k_page(g):
        return lambda b, p, pt: (pt[b, jnp.minimum(p, nsteps - 1) * group + g], 0, 0)

    def v_page(g):
        return lambda b, p, pt: (pt[b, jnp.maximum(p - nsteps, 0) * group + g], 0, 0)

    page_specs = [pl.BlockSpec((1, page * heads, hd), k_page(g)) for g in range(group)]
    page_specs += [pl.BlockSpec((1, page * heads, hd), v_page(g)) for g in range(group)]
    grid_spec = pltpu.PrefetchScalarGridSpec(
        num_scalar_prefetch=1,
        grid=(bsz, 2 * nsteps),
        in_specs=[
            pl.BlockSpec((lq, width), lambda b, p, pt: (b, q_col // width)),
            pl.BlockSpec((lq, width), lambda b, p, pt: (b, k_col // width)),
            pl.BlockSpec((lq, width), lambda b, p, pt: (b, v_col // width)),
        ] + page_specs,
        out_specs=pl.BlockSpec((lq, width), lambda b, p, pt: (b, 0)),
        scratch_shapes=[
            pltpu.VMEM((rows, width), _BF16),
            pltpu.VMEM((npages, rows, page), _F32),
            pltpu.VMEM((npages, heads, hd), _F32),
            pltpu.VMEM((past // MOBA_BLOCK * heads, hd), _F32),
            pltpu.VMEM((npages, rows, page), _BF16),
            pltpu.VMEM((rows, width), _F32),
            pltpu.VMEM((rows, 1), _F32),
        ],
    )
    return pl.pallas_call(
        functools.partial(_moba_sample_kernel, npages=npages, heads=heads, lq=lq, group=group),
        grid_spec=grid_spec,
        out_shape=jax.ShapeDtypeStruct((bsz * lq, width), _F32),
        compiler_params=_params("parallel", "arbitrary"),
        name="moba_sample",
    )(page_table.astype(jnp.int32), proj, proj, proj, *([kc] * group), *([vc] * group))


def _top_rows(x, k):
    r, t = x.shape
    rowi = lax.broadcasted_iota(jnp.int32, (r, t), 0).astype(_F32)
    slot = lax.broadcasted_iota(jnp.int32, (k, t), 0)
    vals = jnp.zeros((k, t), _F32)
    idxs = jnp.zeros((k, t), _F32)
    for n in range(k):
        best = jnp.max(x, axis=0, keepdims=True)
        idx = jnp.min(jnp.where(x == best, rowi, float(r)), axis=0, keepdims=True)
        vals = jnp.where(slot == n, best, vals)
        idxs = jnp.where(slot == n, idx, idxs)
        x = jnp.where(rowi == idx, -jnp.inf, x)
    return vals, idxs


def _candidate_pieces(topk):
    pieces, start = [], 0
    a = 0
    while topk // (a + 1) > 1:
        nb = topk // (a + 1)
        rows = -(-nb // SUBLANES) * SUBLANES
        pieces.append((start, rows, a, nb))
        start += rows
        a += 1
    return pieces, (start, a)


def _peer_retrieve_kernel(q_ref, sk_ref, g_ref, i1_ref, i2_ref):
    topk = PEER_TOPK
    dh = sk_ref.shape[-1]
    tm = q_ref.shape[0]
    tops = []
    for a in range(2):
        qa = q_ref[:, a * dh:(a + 1) * dh].astype(_BF16)
        tops.append(_top_rows(_dot_nt(sk_ref[0, a], qa), topk))
    (s1, k1), (s2, k2) = tops

    pieces, (tail_start, tail_a) = _candidate_pieces(topk)
    parts = []
    for _, rows, a, nb in pieces:
        part = s1[a:a + 1] + s2[:rows]
        if nb < rows:
            part = jnp.where(lax.broadcasted_iota(jnp.int32, (rows, tm), 0) < nb, part, -jnp.inf)
        parts.append(part)
    tail = s1[tail_a:] + s2[0:1]
    tail_rows = -(-(topk - tail_a) // SUBLANES) * SUBLANES
    if tail_rows > topk - tail_a:
        tail = jnp.concatenate([tail, jnp.full((tail_rows - (topk - tail_a), tm), -jnp.inf, _F32)], axis=0)
    cand = jnp.concatenate(parts + [tail], axis=0)
    best_s, pos = _top_rows(cand, topk)

    a_sel = pos - float(tail_start - tail_a)
    b_sel = jnp.zeros_like(pos)
    for start, rows, a, _ in pieces:
        inside = (pos >= float(start)) & (pos < float(start + rows))
        a_sel = jnp.where(inside, float(a), a_sel)
        b_sel = jnp.where(inside, pos - float(start), b_sel)
    i1 = jnp.zeros_like(pos)
    i2 = jnp.zeros_like(pos)
    for a in range(topk):
        i1 = jnp.where(a_sel == float(a), k1[a:a + 1], i1)
        i2 = jnp.where(b_sel == float(a), k2[a:a + 1], i2)
    e = jnp.exp(best_s - best_s[0:1])
    g_ref[...] = e / jnp.sum(e, axis=0, keepdims=True)
    i1_ref[...] = i1
    i2_ref[...] = i2


def _peer_retrieve(q, sub_keys, tm=256):
    t = q.shape[0]
    heads, _, nkeys, dh = sub_keys.shape
    tm = min(tm, t)
    spec = pl.BlockSpec((PEER_TOPK, tm), lambda i, h: (h, i))
    shape = jax.ShapeDtypeStruct((heads * PEER_TOPK, t), _F32)
    return pl.pallas_call(
        _peer_retrieve_kernel,
        grid=(t // tm, heads),
        in_specs=[
            pl.BlockSpec((tm, 2 * dh), lambda i, h: (i, h)),
            pl.BlockSpec((1, 2, nkeys, dh), lambda i, h: (h, 0, 0, 0)),
        ],
        out_specs=[spec, spec, spec],
        out_shape=[shape, shape, shape],
        compiler_params=_params("parallel", "parallel"),
        name="peer_retrieve",
    )(q, sub_keys.astype(_BF16))


def _peer_weights_kernel(g_ref, i1_ref, i2_ref, w_ref, gt_ref, i1t_ref, i2t_ref, scr_ref, *, tm, stride):
    nk = LANES
    gt_ref[...] = g_ref[...].T
    i1t_ref[...] = i1_ref[...].T
    i2t_ref[...] = i2_ref[...].T
    sub = lax.broadcasted_iota(jnp.int32, (nk, nk), 0).astype(_F32)

    def body(n, carry):
        r1 = jnp.broadcast_to(i1t_ref[pl.ds(n, 1), :], (nk, nk))
        r2 = jnp.broadcast_to(i2t_ref[pl.ds(n, 1), :], (nk, nk))
        rg = jnp.broadcast_to(gt_ref[pl.ds(n, 1), :], (nk, nk))
        m1 = jnp.where(sub == r1, 1.0, 0.0).astype(_BF16)
        m2 = jnp.where(sub == r2, rg, 0.0).astype(_BF16)
        scr_ref[pl.ds(n, nk, stride=stride), :] = _dot_nt(m1, m2)
        return carry

    lax.fori_loop(0, tm, body, 0, unroll=SUBLANES)
    for j in range(nk):
        w_ref[:, j * nk:(j + 1) * nk] = scr_ref[j * stride:j * stride + tm, :].astype(w_ref.dtype)


def _peer_weights(g, i1, i2, nkeys, tm=256):
    picks, t = g.shape
    assert picks == LANES and nkeys == LANES
    tm = min(tm, t)
    stride = tm + SUBLANES
    spec = pl.BlockSpec((picks, tm), lambda i: (0, i))
    return pl.pallas_call(
        functools.partial(_peer_weights_kernel, tm=tm, stride=stride),
        grid=(t // tm,),
        in_specs=[spec, spec, spec],
        out_specs=pl.BlockSpec((tm, nkeys * nkeys), lambda i: (i, 0)),
        out_shape=jax.ShapeDtypeStruct((t, nkeys * nkeys), _BF16),
        scratch_shapes=[
            pltpu.VMEM((tm, picks), _F32),
            pltpu.VMEM((tm, picks), _F32),
            pltpu.VMEM((tm, picks), _F32),
            pltpu.VMEM((nkeys * stride, nkeys), _F32),
        ],
        compiler_params=_params("parallel"),
        name="peer_weights",
    )(g, i1, i2)


def _peer_ffn_kernel(xn_ref, w_ref, u_ref, v_ref, r_ref, gf_ref, o_ref, acc_ref, *, final):
    j = pl.program_id(1)

    @pl.when(j == 0)
    def _():
        acc_ref[...] = jnp.zeros_like(acc_ref)

    h = _dot_nt(xn_ref[...], u_ref[...])
    w = w_ref[...].astype(_F32)
    coef = jnp.where(w != 0.0, w * jax.nn.gelu(h), 0.0)
    acc_ref[...] += _dot(coef.astype(_BF16), v_ref[...])

    @pl.when(j == pl.num_programs(1) - 1)
    def _():
        x = r_ref[...] + acc_ref[...]
        if final:
            x = x * lax.rsqrt(jnp.mean(x * x, axis=-1, keepdims=True) + RMS_EPS) * gf_ref[...]
        o_ref[...] = x


def _peer_ffn(xn, w, u, v, res, gain_final, final, tm=512, te=512):
    t, d = xn.shape
    n_exp = u.shape[0]
    tm = min(tm, t)
    return pl.pallas_call(
        functools.partial(_peer_ffn_kernel, final=final),
        grid=(t // tm, n_exp // te),
        in_specs=[
            pl.BlockSpec((tm, d), lambda i, j: (i, 0)),
            pl.BlockSpec((tm, te), lambda i, j: (i, j)),
            pl.BlockSpec((te, d), lambda i, j: (j, 0)),
            pl.BlockSpec((te, d), lambda i, j: (j, 0)),
            pl.BlockSpec((tm, d), lambda i, j: (i, 0)),
            pl.BlockSpec((1, d), lambda i, j: (0, 0)),
        ],
        out_specs=pl.BlockSpec((tm, d), lambda i, j: (i, 0)),
        out_shape=jax.ShapeDtypeStruct((t, d), _F32),
        scratch_shapes=[pltpu.VMEM((tm, d), _F32)],
        compiler_params=_params("parallel", "arbitrary"),
        name="peer_ffn",
    )(xn, w, u, v, res, gain_final.reshape(1, d).astype(_F32))


def _layer(x, bsz, length, s0, past, lw, layer, final):
    d_model = x.shape[1]
    a_width = lw["gnorm"].shape[-1]
    b_width = lw["w_b"].shape[0]
    hn = _rmsnorm(x, lw["norm_mix"], _BF16)
    proj = _matmul(hn, lw["w_in"])
    b_q = 4 * a_width
    b_k, b_v = b_q + b_width, b_q + 2 * b_width
    g_a = b_q + 3 * b_width
    g_b = g_a + d_model

    o_a, s_new = _hgrn(proj, lw["lb_logits"], lw["gnorm"], s0, layer, bsz, length, a_width)
    if past is None:
        o_b = _moba_prompt(proj, bsz, length, b_q, b_k, b_v, b_width)
    else:
        o_b = _moba_sample(proj, past[0], past[1], past[2], bsz, length, b_q, b_k, b_v, b_width)
    merged = _merge(o_a, o_b, lw["w_a"], lw["w_b"], proj, g_a, g_b, d_model)
    x = _matmul_residual(merged, lw["w_out"], x)

    xn = _rmsnorm(x, lw["norm_ffn"], _BF16)
    q = _matmul(xn, lw["w_query"])
    gates, i1, i2 = _peer_retrieve(q, lw["sub_keys"])
    w = _peer_weights(gates, i1, i2, lw["sub_keys"].shape[2])
    x = _peer_ffn(xn, w, lw["peer_u"], lw["peer_v"], x, lw["norm_final"], final)

    k_new = proj[:, b_k:b_k + b_width].reshape(bsz, length, B_HEADS, b_width // B_HEADS)
    v_new = proj[:, b_v:b_v + b_width].reshape(bsz, length, B_HEADS, b_width // B_HEADS)
    return x, s_new, k_new, v_new


def kernel(x_prompt, x_sample, cache_k, cache_v, state_hgrn, page_table, norm_mix, w_in, hgrn_lb_logits,
           hgrn_gnorm, w_a_proj, w_b_proj, w_out, norm_ffn, peer_w_query, peer_sub_keys, peer_u, peer_v,
           norm_final):
    depth = w_in.shape[0]
    bp, lp, d_model = x_prompt.shape
    bs, ls, _ = x_sample.shape
    x_p = x_prompt.reshape(bp * lp, d_model)
    x_s = x_sample.reshape(bs * ls, d_model)
    outs = [[] for _ in range(6)]
    for layer in range(depth):
        lw = {
            "norm_mix": norm_mix[layer], "w_in": w_in[layer].astype(_BF16), "lb_logits": hgrn_lb_logits,
            "gnorm": hgrn_gnorm[layer], "w_a": w_a_proj[layer].astype(_BF16), "w_b": w_b_proj[layer].astype(_BF16),
            "w_out": w_out[layer].astype(_BF16), "norm_ffn": norm_ffn[layer],
            "w_query": peer_w_query[layer].astype(_BF16), "sub_keys": peer_sub_keys[layer],
            "peer_u": peer_u[layer].astype(_BF16), "peer_v": peer_v[layer].astype(_BF16), "norm_final": norm_final,
        }
        final = layer == depth - 1
        s0_p = jnp.zeros((bp,) + state_hgrn.shape[2:], _F32)
        x_p, s_p, k_p, v_p = _layer(x_p, bp, lp, s0_p, None, lw, layer, final)
        past = (cache_k[layer], cache_v[layer], page_table)
        x_s, s_s, k_s, v_s = _layer(x_s, bs, ls, state_hgrn[layer], past, lw, layer, final)
        for lst, val in zip(outs, (s_p, s_s, k_p, v_p, k_s, v_s)):
            lst.append(val)
    sp, ss, kp, vp, ks, vs = (jnp.stack(lst, axis=0) for lst in outs)
    return (x_p.reshape(bp, lp, d_model), x_s.reshape(bs, ls, d_model),
            sp.astype(state_hgrn.dtype), ss.astype(state_hgrn.dtype),
            kp.astype(cache_k.dtype), vp.astype(cache_v.dtype), ks.astype(cache_k.dtype), vs.astype(cache_v.dtype))
```

```python
import functools

import jax
import jax.numpy as jnp
from jax import lax
from jax.experimental import pallas as pl
from jax.experimental.pallas import tpu as pltpu

A_HEADS = 8
B_HEADS = 8
MOBA_BLOCK = 256
MOBA_TOPK = 3
PEER_HEADS = 8
PEER_TOPK = 16
RMS_EPS = 1e-6
NEG_INF = -1e30

LANES = 128
SUBLANES = 8
HGRN_CHUNK = 128
HGRN_SUB = 16
VMEM_LIMIT = 48 * 1024 * 1024

_F32 = jnp.float32
_BF16 = jnp.bfloat16
_NT = (((1,), (1,)), ((), ()))


def _params(*sem, vmem=VMEM_LIMIT):
    return pltpu.CompilerParams(dimension_semantics=sem, vmem_limit_bytes=vmem)


def _dot_nt(a, b):
    return lax.dot_general(a, b, _NT, preferred_element_type=_F32)


def _dot(a, b):
    return jnp.dot(a, b, preferred_element_type=_F32)


def _log2(n):
    assert n > 0 and n & (n - 1) == 0, n
    return n.bit_length() - 1


def _div(x, n):
    return lax.shift_right_logical(x, jnp.int32(_log2(n)))


def _mod(x, n):
    return lax.bitwise_and(x, jnp.int32((1 << _log2(n)) - 1))


def _tile_n(n, *cols, cap=512):
    for tn in (1024, 512, 256, LANES):
        if tn <= cap and all(v % tn == 0 for v in (n,) + cols):
            return tn
    raise ValueError((n, cols))


def _rmsnorm_kernel(x_ref, g_ref, o_ref):
    x = x_ref[...]
    y = x * lax.rsqrt(jnp.mean(x * x, axis=-1, keepdims=True) + RMS_EPS)
    o_ref[...] = (y * g_ref[...]).astype(o_ref.dtype)


def _rmsnorm(x, gain, out_dtype):
    t, d = x.shape
    tm = min(t, 512)
    return pl.pallas_call(
        _rmsnorm_kernel,
        grid=(t // tm,),
        in_specs=[pl.BlockSpec((tm, d), lambda i: (i, 0)), pl.BlockSpec((1, d), lambda i: (0, 0))],
        out_specs=pl.BlockSpec((tm, d), lambda i: (i, 0)),
        out_shape=jax.ShapeDtypeStruct((t, d), out_dtype),
        compiler_params=_params("parallel"),
        name="rmsnorm",
    )(x, gain.reshape(1, d).astype(_F32))


def _mm_kernel(a_ref, b_ref, o_ref):
    o_ref[...] = _dot(a_ref[...], b_ref[...]).astype(o_ref.dtype)


def _matmul(a, b, out_dtype=_F32):
    m, k = a.shape
    n = b.shape[1]
    tm = min(m, 1024)
    tn = _tile_n(n, cap=1024)
    return pl.pallas_call(
        _mm_kernel,
        grid=(m // tm, n // tn),
        in_specs=[pl.BlockSpec((tm, k), lambda i, j: (i, 0)), pl.BlockSpec((k, tn), lambda i, j: (0, j))],
        out_specs=pl.BlockSpec((tm, tn), lambda i, j: (i, j)),
        out_shape=jax.ShapeDtypeStruct((m, n), out_dtype),
        compiler_params=_params("parallel", "parallel"),
        name="matmul",
    )(a, b)


def _merge_kernel(oa_ref, ob_ref, wa_ref, wb_ref, ga_ref, gb_ref, o_ref):
    ya = _dot(oa_ref[...].astype(_BF16), wa_ref[...])
    yb = _dot(ob_ref[...].astype(_BF16), wb_ref[...])
    o_ref[...] = (jax.nn.sigmoid(ga_ref[...]) * ya + jax.nn.sigmoid(gb_ref[...]) * yb).astype(o_ref.dtype)


def _merge(oa, ob, wa, wb, proj, ga_col, gb_col, d_model):
    t, ka = oa.shape
    kb = ob.shape[1]
    tm = min(t, 512)
    tn = _tile_n(d_model, ga_col, gb_col)
    return pl.pallas_call(
        _merge_kernel,
        grid=(t // tm, d_model // tn),
        in_specs=[
            pl.BlockSpec((tm, ka), lambda i, j: (i, 0)),
            pl.BlockSpec((tm, kb), lambda i, j: (i, 0)),
            pl.BlockSpec((ka, tn), lambda i, j: (0, j)),
            pl.BlockSpec((kb, tn), lambda i, j: (0, j)),
            pl.BlockSpec((tm, tn), lambda i, j: (i, ga_col // tn + j)),
            pl.BlockSpec((tm, tn), lambda i, j: (i, gb_col // tn + j)),
        ],
        out_specs=pl.BlockSpec((tm, tn), lambda i, j: (i, j)),
        out_shape=jax.ShapeDtypeStruct((t, d_model), _BF16),
        compiler_params=_params("parallel", "parallel"),
        name="merge",
    )(oa, ob, wa, wb, proj, proj)


def _mm_residual_kernel(a_ref, b_ref, r_ref, o_ref):
    o_ref[...] = r_ref[...] + _dot(a_ref[...], b_ref[...])


def _matmul_residual(a, b, res):
    m, k = a.shape
    n = b.shape[1]
    tm = min(m, 1024)
    tn = _tile_n(n)
    return pl.pallas_call(
        _mm_residual_kernel,
        grid=(m // tm, n // tn),
        in_specs=[
            pl.BlockSpec((tm, k), lambda i, j: (i, 0)),
            pl.BlockSpec((k, tn), lambda i, j: (0, j)),
            pl.BlockSpec((tm, tn), lambda i, j: (i, j)),
        ],
        out_specs=pl.BlockSpec((tm, tn), lambda i, j: (i, j)),
        out_shape=jax.ShapeDtypeStruct((m, n), _F32),
        compiler_params=_params("parallel", "parallel"),
        name="matmul_residual",
    )(a, b, res)


def _hgrn_head(q, fpre, v, ag, logits, gn, st, *, layer, rows):
    c_len, sub = HGRN_CHUNK, HGRN_SUB

    def pad(x):
        if rows == c_len:
            return x
        return jnp.concatenate([x, jnp.zeros((c_len - rows, x.shape[1]), x.dtype)], axis=0)

    ex = jnp.exp(logits - jnp.max(logits, axis=0, keepdims=True))
    lb = jnp.sum(ex[: layer + 1], axis=0, keepdims=True) / jnp.sum(ex, axis=0, keepdims=True)

    forget = lb + (1.0 - lb) * jax.nn.sigmoid(fpre)
    logf = pad(jnp.log(forget))
    kk = pad(1.0 - forget)
    q = pad(q)
    v = pad(v)

    row = lax.broadcasted_iota(jnp.int32, (c_len, LANES), 0)
    b = logf
    shift = 1
    while shift < c_len:
        b = b + jnp.where(row >= shift, pltpu.roll(b, shift, 0), 0.0)
        shift *= 2

    o = _dot_nt((q * jnp.exp(b)).astype(_BF16), st.astype(_BF16))

    rowc = lax.broadcasted_iota(jnp.int32, (sub, LANES), 0)
    lane = lax.broadcasted_iota(jnp.int32, (sub, LANES), 1)
    n_sub = -(-rows // sub)
    att_rows = []
    for i in range(n_sub):
        lo = i * sub
        qi, bi, ki = q[lo:lo + sub], b[lo:lo + sub], kk[lo:lo + sub]
        att = jnp.zeros((sub, LANES), _F32)
        for s in range(sub):
            dec = jnp.exp(jnp.minimum(bi - bi[s:s + 1], 0.0))
            x = jnp.where(rowc >= s, qi * dec * ki[s:s + 1], 0.0)
            att = jnp.where(lane == lo + s, jnp.sum(x, axis=-1, keepdims=True), att)
        if i > 0:
            ref_b = b[lo - 1:lo]
            qt = (qi * jnp.exp(bi - ref_b)).astype(_BF16)
            kp = kk[:lo] * jnp.exp(ref_b - b[:lo])
            kp = jnp.concatenate([kp, jnp.zeros((c_len - lo, LANES), _F32)], axis=0).astype(_BF16)
            att = att + _dot_nt(qt, kp)
        att_rows.append(att)
    if n_sub * sub < c_len:
        att_rows.append(jnp.zeros((c_len - n_sub * sub, LANES), _F32))
    att = jnp.concatenate(att_rows, axis=0)
    o = o + _dot(att.astype(_BF16), v.astype(_BF16))

    b_last = b[c_len - 1:c_len]
    k_hat = (kk * jnp.exp(b_last - b)).astype(_BF16)
    st_new = st * jnp.exp(b_last) + _dot(v.T.astype(_BF16), k_hat)

    oo = o[:rows]
    on = oo * lax.rsqrt(jnp.mean(oo * oo, axis=-1, keepdims=True) + RMS_EPS) * gn
    return on * (ag * jax.nn.sigmoid(ag)), st_new


def _hgrn_kernel(q_ref, f_ref, i_ref, g_ref, lbl_ref, gn_ref, s0_ref, o_ref, sfin_ref, st_ref, *,
                 layer, rows, hb):
    ci = pl.program_id(2)

    @pl.when(ci == 0)
    def _():
        for h in range(hb):
            st_ref[h] = s0_ref[0, h].T

    for h in range(hb):
        cols = slice(h * LANES, (h + 1) * LANES)
        out, st_new = _hgrn_head(q_ref[:, cols], f_ref[:, cols], i_ref[:, cols], g_ref[:, cols],
                                 lbl_ref[:, cols], gn_ref[:, cols], st_ref[h], layer=layer, rows=rows)
        o_ref[:, cols] = out
        st_ref[h] = st_new

    @pl.when(ci == pl.num_programs(2) - 1)
    def _():
        for h in range(hb):
            sfin_ref[0, h] = st_ref[h].T


def _hgrn(proj, lb_logits, gnorm, s0, layer, bsz, length, width, hb=4):
    heads = A_HEADS
    dk = width // heads
    hb = min(hb, heads)
    assert dk == LANES and heads % hb == 0
    if length >= HGRN_CHUNK:
        assert length % HGRN_CHUNK == 0
        rows, nc = HGRN_CHUNK, length // HGRN_CHUNK
    else:
        assert length % SUBLANES == 0
        rows, nc = length, 1
    sec = heads // hb
    wb = hb * dk

    def col(section):
        return lambda b, h, c: (b * nc + c, section * sec + h)

    return pl.pallas_call(
        functools.partial(_hgrn_kernel, layer=layer, rows=rows, hb=hb),
        grid=(bsz, heads // hb, nc),
        in_specs=[
            pl.BlockSpec((rows, wb), col(0)),
            pl.BlockSpec((rows, wb), col(1)),
            pl.BlockSpec((rows, wb), col(2)),
            pl.BlockSpec((rows, wb), col(3)),
            pl.BlockSpec((lb_logits.shape[0], wb), lambda b, h, c: (0, h)),
            pl.BlockSpec((1, wb), lambda b, h, c: (0, h)),
            pl.BlockSpec((1, hb, dk, dk), lambda b, h, c: (b, h, 0, 0)),
        ],
        out_specs=[
            pl.BlockSpec((rows, wb), lambda b, h, c: (b * nc + c, h)),
            pl.BlockSpec((1, hb, dk, dk), lambda b, h, c: (b, h, 0, 0)),
        ],
        out_shape=[
            jax.ShapeDtypeStruct((bsz * length, width), _F32),
            jax.ShapeDtypeStruct((bsz, heads, dk, dk), _F32),
        ],
        scratch_shapes=[pltpu.VMEM((hb, dk, dk), _F32)],
        compiler_params=_params("parallel", "parallel", "arbitrary"),
        name="hgrn2",
    )(proj, proj, proj, proj, lb_logits.astype(_F32), gnorm.reshape(1, width).astype(_F32), s0.astype(_F32))


def _moba_prompt_kernel(q_ref, k_ref, v_ref, o_ref, ms_ref, m_ref, l_ref, acc_ref, *, nblk):
    blk = MOBA_BLOCK
    i = pl.program_id(2)
    hd = q_ref.shape[-1]
    q = q_ref[...].astype(_BF16)
    k = k_ref[...]
    s = _dot_nt(q, k.astype(_BF16)) * (hd ** -0.5)

    k_mean = jnp.mean(k.reshape(nblk, blk, hd), axis=1)
    k_mean = jnp.concatenate([k_mean, jnp.zeros((LANES - nblk, hd), _F32)], axis=0)
    gate = _dot_nt(k_mean.astype(_BF16), q)

    nrow = -(-nblk // SUBLANES) * SUBLANES
    blk_id = lax.broadcasted_iota(jnp.int32, (nrow, blk), 0)
    g = jnp.where(blk_id < i, gate[:nrow], NEG_INF)
    rank = jnp.zeros((nrow, blk), jnp.int32)
    for jp in range(nblk):
        other = g[jp:jp + 1]
        ahead = (other > g) | ((other == g) & (blk_id > jp))
        rank = rank + ahead.astype(jnp.int32)
    picked = jnp.where((rank < min(MOBA_TOPK, nblk)) & (blk_id < i), 1.0, 0.0)
    picked = jnp.concatenate([picked, jnp.zeros((LANES - nrow, blk), _F32)], axis=0).T
    sel = [picked[:, j:j + 1] > 0.0 for j in range(nblk)]

    row = lax.broadcasted_iota(jnp.int32, (blk, blk), 0)
    col = lax.broadcasted_iota(jnp.int32, (blk, blk), 1)
    m_ref[...] = jnp.full((blk, 1), NEG_INF, _F32)
    l_ref[...] = jnp.zeros((blk, 1), _F32)
    acc_ref[...] = jnp.zeros((blk, hd), _F32)
    for j in range(nblk):

        @pl.when(j <= i)
        def _(j=j):
            delta = (i - j) * blk
            ok = (col <= row + delta) & (sel[j] | (col >= delta))
            ms = jnp.where(ok, s[:, j * blk:(j + 1) * blk], NEG_INF)
            ms_ref[j] = ms
            m_ref[...] = jnp.maximum(m_ref[...], ms.max(axis=-1, keepdims=True))

    for j in range(nblk):

        @pl.when(j <= i)
        def _(j=j):
            p = jnp.exp(ms_ref[j] - m_ref[...])
            l_ref[...] += p.sum(axis=-1, keepdims=True)
            acc_ref[...] += _dot(p.astype(_BF16), v_ref[j * blk:(j + 1) * blk, :].astype(_BF16))

    o_ref[...] = acc_ref[...] / l_ref[...]


def _moba_prompt(proj, bsz, length, q_col, k_col, v_col, width):
    heads = B_HEADS
    hd = width // heads
    assert hd == LANES and length % MOBA_BLOCK == 0
    nblk = length // MOBA_BLOCK
    assert nblk <= LANES
    return pl.pallas_call(
        functools.partial(_moba_prompt_kernel, nblk=nblk),
        grid=(bsz, heads, nblk),
        in_specs=[
            pl.BlockSpec((MOBA_BLOCK, hd), lambda b, h, i: (b * nblk + i, q_col // hd + h)),
            pl.BlockSpec((length, hd), lambda b, h, i: (b, k_col // hd + h)),
            pl.BlockSpec((length, hd), lambda b, h, i: (b, v_col // hd + h)),
        ],
        out_specs=pl.BlockSpec((MOBA_BLOCK, hd), lambda b, h, i: (b * nblk + i, h)),
        out_shape=jax.ShapeDtypeStruct((bsz * length, width), _F32),
        scratch_shapes=[
            pltpu.VMEM((nblk, MOBA_BLOCK, MOBA_BLOCK), _F32),
            pltpu.VMEM((MOBA_BLOCK, 1), _F32),
            pltpu.VMEM((MOBA_BLOCK, 1), _F32),
            pltpu.VMEM((MOBA_BLOCK, hd), _F32),
        ],
        compiler_params=_params("parallel", "parallel", "arbitrary"),
        name="moba_prompt",
    )(proj, proj, proj)


def _moba_sample_kernel(pt_ref, q_ref, kn_ref, vn_ref, *refs, npages, heads, lq, group):
    kc_refs, vc_refs = refs[:group], refs[group:2 * group]
    o_ref, qbd_ref, ks_ref, km_ref, m_ref, l_ref, acc_ref = refs[2 * group:]
    p = pl.program_id(1)
    nsteps = npages // group
    hd = kc_refs[0].shape[2]
    page = kc_refs[0].shape[1] // heads
    width = heads * hd
    rows = heads * lq
    blk_pages = MOBA_BLOCK // page
    nb = npages // blk_pages
    scale = hd ** -0.5

    def by_token(ref):
        return jnp.concatenate([ref[0, pl.ds(h, page, stride=heads), :] for h in range(heads)], axis=1)

    @pl.when(p == 0)
    def _():
        qt = jnp.concatenate([q_ref[...]] * heads, axis=0)
        rh = _div(lax.broadcasted_iota(jnp.int32, (rows, width), 0), lq)
        ch = _div(lax.broadcasted_iota(jnp.int32, (rows, width), 1), hd)
        qbd_ref[...] = jnp.where(rh == ch, qt, 0.0).astype(_BF16)

    qbd = qbd_ref[...]
    for g0 in range(0, group, blk_pages):
        blk = (p * group + g0) // blk_pages
        s = [_dot_nt(qbd, by_token(kc_refs[g0 + g]).astype(_BF16)) * scale for g in range(blk_pages)]
        m = s[0].max(axis=1, keepdims=True)
        for sg in s[1:]:
            m = jnp.maximum(m, sg.max(axis=1, keepdims=True))
        l = jnp.zeros((rows, 1), _F32)
        acc = jnp.zeros((rows, width), _F32)
        for g in range(blk_pages):
            w = jnp.exp(s[g] - m)
            l = l + w.sum(axis=1, keepdims=True)
            acc = acc + _dot(w.astype(_BF16), by_token(vc_refs[g0 + g]).astype(_BF16))
            ks_ref[p * group + g0 + g] = kc_refs[g0 + g][0].reshape(page, heads, hd).sum(axis=0)
        m_ref[blk] = m
        l_ref[blk] = l
        acc_ref[blk] = acc

    @pl.when(p == nsteps - 1)
    def _():
        k_sum = ks_ref[...].reshape(nb, blk_pages, heads, hd).sum(axis=1)
        km_ref[...] = k_sum.reshape(nb * heads, hd) * (1.0 / MOBA_BLOCK)
        k_mean = jnp.concatenate([km_ref[pl.ds(h, nb, stride=heads), :] for h in range(heads)], axis=1)
        gate = _dot_nt(qbd, k_mean.astype(_BF16))
        lane = lax.broadcasted_iota(jnp.int32, (rows, nb), 1)
        sel = jnp.zeros((rows, nb), _F32)
        for _ in range(MOBA_TOPK):
            best = jnp.max(gate, axis=1, keepdims=True)
            idx = jnp.min(jnp.where(gate == best, lane, nb), axis=1, keepdims=True)
            hit = lane == idx
            sel = jnp.where(hit, 1.0, sel)
            gate = jnp.where(hit, -jnp.inf, gate)

        pad = 2 * SUBLANES - lq
        kn = jnp.concatenate([kn_ref[...], jnp.zeros((pad, width), _F32)], axis=0).astype(_BF16)
        vn = jnp.concatenate([vn_ref[...], jnp.zeros((pad, width), _F32)], axis=0).astype(_BF16)
        s_own = _dot_nt(qbd, kn) * scale
        qi = _mod(lax.broadcasted_iota(jnp.int32, s_own.shape, 0), lq)
        kj = lax.broadcasted_iota(jnp.int32, s_own.shape, 1)
        s_own = jnp.where((kj <= qi) & (kj < lq), s_own, NEG_INF)

        picked = [sel[:, b:b + 1] > 0.0 for b in range(nb)]
        top = jnp.max(s_own, axis=1, keepdims=True)
        for b in range(nb):
            top = jnp.maximum(top, jnp.where(picked[b], m_ref[b], NEG_INF))
        p_own = jnp.exp(s_own - top)
        l = jnp.sum(p_own, axis=1, keepdims=True)
        out = _dot(p_own.astype(_BF16), vn)
        for b in range(nb):
            c = jnp.where(picked[b], jnp.exp(jnp.minimum(m_ref[b] - top, 0.0)), 0.0)
            l = l + c * l_ref[b]
            out = out + c * acc_ref[b]
        out = out / l
        for h in range(heads):
            o_ref[:, h * hd:(h + 1) * hd] = out[h * lq:(h + 1) * lq, h * hd:(h + 1) * hd]


def _moba_sample(proj, cache_k, cache_v, page_table, bsz, lq, q_col, k_col, v_col, width):
    heads = B_HEADS
    hd = width // heads
    n_phys, page = cache_k.shape[0], cache_k.shape[1]
    npages = page_table.shape[1]
    past = npages * page
    assert MOBA_BLOCK % page == 0 and past % MOBA_BLOCK == 0 and lq % SUBLANES == 0 and lq <= 2 * SUBLANES
    assert past // MOBA_BLOCK >= MOBA_TOPK and heads % SUBLANES == 0
    blk_pages = MOBA_BLOCK // page
    nb = past // MOBA_BLOCK
    group = blk_pages * max(g for g in (1, 2, 4) if nb % g == 0)
    nsteps = npages // group
    rows = heads * lq
    kc = cache_k.reshape(n_phys, page * heads, hd)
    vc = cache_v.reshape(n_phys, page * heads, hd)

    def nth_page(g):
        return lambda b, p, pt: (pt[b, p * group + g], 0, 0)

    page_specs = [pl.BlockSpec((1, page * heads, hd), nth_page(g)) for g in range(group)] * 2
    grid_spec = pltpu.PrefetchScalarGridSpec(
        num_scalar_prefetch=1,
        grid=(bsz, nsteps),
        in_specs=[
            pl.BlockSpec((lq, width), lambda b, p, pt: (b, q_col // width)),
            pl.BlockSpec((lq, width), lambda b, p, pt: (b, k_col // width)),
            pl.BlockSpec((lq, width), lambda b, p, pt: (b, v_col // width)),
        ] + page_specs,
        out_specs=pl.BlockSpec((lq, width), lambda b, p, pt: (b, 0)),
        scratch_shapes=[
            pltpu.VMEM((rows, width), _BF16),
            pltpu.VMEM((npages, heads, hd), _F32),
            pltpu.VMEM((nb * heads, hd), _F32),
            pltpu.VMEM((nb, rows, 1), _F32),
            pltpu.VMEM((nb, rows, 1), _F32),
            pltpu.VMEM((nb, rows, width), _F32),
        ],
    )
    return pl.pallas_call(
        functools.partial(_moba_sample_kernel, npages=npages, heads=heads, lq=lq, group=group),
        grid_spec=grid_spec,
        out_shape=jax.ShapeDtypeStruct((bsz * lq, width), _F32),
        compiler_params=_params("parallel", "arbitrary"),
        name="moba_sample",
    )(page_table.astype(jnp.int32), proj, proj, proj, *([kc] * group), *([vc] * group))


def _top_rows(x, k):
    r, t = x.shape
    rowi = lax.broadcasted_iota(jnp.int32, (r, t), 0).astype(_F32)
    slot = lax.broadcasted_iota(jnp.int32, (k, t), 0)
    vals = jnp.zeros((k, t), _F32)
    idxs = jnp.zeros((k, t), _F32)
    for n in range(k):
        best = jnp.max(x, axis=0, keepdims=True)
        idx = jnp.min(jnp.where(x == best, rowi, float(r)), axis=0, keepdims=True)
        vals = jnp.where(slot == n, best, vals)
        idxs = jnp.where(slot == n, idx, idxs)
        x = jnp.where(rowi == idx, -jnp.inf, x)
    return vals, idxs


def _candidate_pieces(topk):
    pieces, start = [], 0
    a = 0
    while topk // (a + 1) > 1:
        nb = topk // (a + 1)
        rows = -(-nb // SUBLANES) * SUBLANES
        pieces.append((start, rows, a, nb))
        start += rows
        a += 1
    return pieces, (start, a)


def _peer_retrieve_kernel(q_ref, sk_ref, g_ref, i1_ref, i2_ref):
    topk = PEER_TOPK
    dh = sk_ref.shape[-1]
    tm = q_ref.shape[0]
    tops = []
    for a in range(2):
        qa = q_ref[:, a * dh:(a + 1) * dh].astype(_BF16)
        tops.append(_top_rows(_dot_nt(sk_ref[0, a], qa), topk))
    (s1, k1), (s2, k2) = tops

    pieces, (tail_start, tail_a) = _candidate_pieces(topk)
    parts = []
    for _, rows, a, nb in pieces:
        part = s1[a:a + 1] + s2[:rows]
        if nb < rows:
            part = jnp.where(lax.broadcasted_iota(jnp.int32, (rows, tm), 0) < nb, part, -jnp.inf)
        parts.append(part)
    tail = s1[tail_a:] + s2[0:1]
    tail_rows = -(-(topk - tail_a) // SUBLANES) * SUBLANES
    if tail_rows > topk - tail_a:
        tail = jnp.concatenate([tail, jnp.full((tail_rows - (topk - tail_a), tm), -jnp.inf, _F32)], axis=0)
    cand = jnp.concatenate(parts + [tail], axis=0)
    best_s, pos = _top_rows(cand, topk)

    a_sel = pos - float(tail_start - tail_a)
    b_sel = jnp.zeros_like(pos)
    for start, rows, a, _ in pieces:
        inside = (pos >= float(start)) & (pos < float(start + rows))
        a_sel = jnp.where(inside, float(a), a_sel)
        b_sel = jnp.where(inside, pos - float(start), b_sel)
    i1 = jnp.zeros_like(pos)
    i2 = jnp.zeros_like(pos)
    for a in range(topk):
        i1 = jnp.where(a_sel == float(a), k1[a:a + 1], i1)
        i2 = jnp.where(b_sel == float(a), k2[a:a + 1], i2)
    e = jnp.exp(best_s - best_s[0:1])
    g_ref[...] = e / jnp.sum(e, axis=0, keepdims=True)
    i1_ref[...] = i1
    i2_ref[...] = i2


def _peer_retrieve(q, sub_keys, tm=256):
    t = q.shape[0]
    heads, _, nkeys, dh = sub_keys.shape
    tm = min(tm, t)
    spec = pl.BlockSpec((PEER_TOPK, tm), lambda i, h: (h, i))
    shape = jax.ShapeDtypeStruct((heads * PEER_TOPK, t), _F32)
    return pl.pallas_call(
        _peer_retrieve_kernel,
        grid=(t // tm, heads),
        in_specs=[
            pl.BlockSpec((tm, 2 * dh), lambda i, h: (i, h)),
            pl.BlockSpec((1, 2, nkeys, dh), lambda i, h: (h, 0, 0, 0)),
        ],
        out_specs=[spec, spec, spec],
        out_shape=[shape, shape, shape],
        compiler_params=_params("parallel", "parallel"),
        name="peer_retrieve",
    )(q, sub_keys.astype(_BF16))


def _peer_weights_kernel(g_ref, i1_ref, i2_ref, w_ref, gt_ref, i1t_ref, i2t_ref, scr_ref, *, tm, stride):
    nk = LANES
    gt_ref[...] = g_ref[...].T
    i1t_ref[...] = i1_ref[...].T
    i2t_ref[...] = i2_ref[...].T
    sub = lax.broadcasted_iota(jnp.int32, (nk, nk), 0).astype(_F32)

    def body(n, carry):
        r1 = jnp.broadcast_to(i1t_ref[pl.ds(n, 1), :], (nk, nk))
        r2 = jnp.broadcast_to(i2t_ref[pl.ds(n, 1), :], (nk, nk))
        rg = jnp.broadcast_to(gt_ref[pl.ds(n, 1), :], (nk, nk))
        m1 = jnp.where(sub == r1, 1.0, 0.0).astype(_BF16)
        m2 = jnp.where(sub == r2, rg, 0.0).astype(_BF16)
        scr_ref[pl.ds(n, nk, stride=stride), :] = _dot_nt(m1, m2)
        return carry

    lax.fori_loop(0, tm, body, 0, unroll=SUBLANES)
    for j in range(nk):
        w_ref[:, j * nk:(j + 1) * nk] = scr_ref[j * stride:j * stride + tm, :].astype(w_ref.dtype)


def _peer_weights(g, i1, i2, nkeys, tm=256):
    picks, t = g.shape
    assert picks == LANES and nkeys == LANES
    tm = min(tm, t)
    stride = tm + SUBLANES
    spec = pl.BlockSpec((picks, tm), lambda i: (0, i))
    return pl.pallas_call(
        functools.partial(_peer_weights_kernel, tm=tm, stride=stride),
        grid=(t // tm,),
        in_specs=[spec, spec, spec],
        out_specs=pl.BlockSpec((tm, nkeys * nkeys), lambda i: (i, 0)),
        out_shape=jax.ShapeDtypeStruct((t, nkeys * nkeys), _BF16),
        scratch_shapes=[
            pltpu.VMEM((tm, picks), _F32),
            pltpu.VMEM((tm, picks), _F32),
            pltpu.VMEM((tm, picks), _F32),
            pltpu.VMEM((nkeys * stride, nkeys), _F32),
        ],
        compiler_params=_params("parallel"),
        name="peer_weights",
    )(g, i1, i2)


def _peer_ffn_kernel(xn_ref, w_ref, u_ref, v_ref, o_ref, *, tn):
    j = pl.program_id(1)

    @pl.when(j == 0)
    def _():
        o_ref[...] = jnp.zeros_like(o_ref)

    xn = xn_ref[...]
    parts = []
    for e in range(0, u_ref.shape[0], tn):
        h = _dot_nt(xn, u_ref[e:e + tn, :])
        w = w_ref[:, e:e + tn].astype(_F32)
        parts.append(jnp.where(w != 0.0, w * jax.nn.gelu(h), 0.0).astype(_BF16))
    coef = jnp.concatenate(parts, axis=1)
    for n in range(0, o_ref.shape[1], tn):
        o_ref[:, n:n + tn] += _dot(coef, v_ref[:, n:n + tn])


def _peer_ffn(xn, w, u, v, tm=1024, te=1024):
    t, d = xn.shape
    n_exp = u.shape[0]
    tm = min(tm, t)
    tn = _tile_n(d)
    blocks = 2 * (2 * (tm * d + tm * te + 2 * te * d) + 4 * tm * d)
    temps = tm * te * 2 + 4 * tm * tn * 4
    return pl.pallas_call(
        functools.partial(_peer_ffn_kernel, tn=tn),
        grid=(t // tm, n_exp // te),
        in_specs=[
            pl.BlockSpec((tm, d), lambda i, j: (i, 0)),
            pl.BlockSpec((tm, te), lambda i, j: (i, j)),
            pl.BlockSpec((te, d), lambda i, j: (j, 0)),
            pl.BlockSpec((te, d), lambda i, j: (j, 0)),
        ],
        out_specs=pl.BlockSpec((tm, d), lambda i, j: (i, 0)),
        out_shape=jax.ShapeDtypeStruct((t, d), _F32),
        compiler_params=_params("parallel", "arbitrary", vmem=max(VMEM_LIMIT, blocks + temps)),
        name="peer_ffn",
    )(xn, w, u, v)


def _residual_norm_kernel(x_ref, y_ref, g_ref, o_ref, *, final):
    x = x_ref[...] + y_ref[...]
    if final:
        x = x * lax.rsqrt(jnp.mean(x * x, axis=-1, keepdims=True) + RMS_EPS) * g_ref[...]
    o_ref[...] = x


def _residual_norm(x, y, gain, final):
    t, d = x.shape
    tm = min(t, 512)
    spec = pl.BlockSpec((tm, d), lambda i: (i, 0))
    return pl.pallas_call(
        functools.partial(_residual_norm_kernel, final=final),
        grid=(t // tm,),
        in_specs=[spec, spec, pl.BlockSpec((1, d), lambda i: (0, 0))],
        out_specs=spec,
        out_shape=jax.ShapeDtypeStruct((t, d), _F32),
        compiler_params=_params("parallel"),
        name="residual_norm",
    )(x, y, gain.reshape(1, d).astype(_F32))


def _layer(x, bsz, length, s0, past, lw, layer, final):
    d_model = x.shape[1]
    a_width = lw["gnorm"].shape[-1]
    b_width = lw["w_b"].shape[0]
    hn = _rmsnorm(x, lw["norm_mix"], _BF16)
    proj = _matmul(hn, lw["w_in"])
    b_q = 4 * a_width
    b_k, b_v = b_q + b_width, b_q + 2 * b_width
    g_a = b_q + 3 * b_width
    g_b = g_a + d_model

    o_a, s_new = _hgrn(proj, lw["lb_logits"], lw["gnorm"], s0, layer, bsz, length, a_width)
    if past is None:
        o_b = _moba_prompt(proj, bsz, length, b_q, b_k, b_v, b_width)
    else:
        o_b = _moba_sample(proj, past[0], past[1], past[2], bsz, length, b_q, b_k, b_v, b_width)
    merged = _merge(o_a, o_b, lw["w_a"], lw["w_b"], proj, g_a, g_b, d_model)
    x = _matmul_residual(merged, lw["w_out"], x)

    xn = _rmsnorm(x, lw["norm_ffn"], _BF16)
    q = _matmul(xn, lw["w_query"])
    gates, i1, i2 = _peer_retrieve(q, lw["sub_keys"])
    w = _peer_weights(gates, i1, i2, lw["sub_keys"].shape[2])
    x = _residual_norm(x, _peer_ffn(xn, w, lw["peer_u"], lw["peer_v"]), lw["norm_final"], final)

    k_new = proj[:, b_k:b_k + b_width].reshape(bsz, length, B_HEADS, b_width // B_HEADS)
    v_new = proj[:, b_v:b_v + b_width].reshape(bsz, length, B_HEADS, b_width // B_HEADS)
    return x, s_new, k_new, v_new


def kernel(x_prompt, x_sample, cache_k, cache_v, state_hgrn, page_table, norm_mix, w_in, hgrn_lb_logits,
           hgrn_gnorm, w_a_proj, w_b_proj, w_out, norm_ffn, peer_w_query, peer_sub_keys, peer_u, peer_v,
           norm_final):
    depth = w_in.shape[0]
    bp, lp, d_model = x_prompt.shape
    bs, ls, _ = x_sample.shape
    x_p = x_prompt.reshape(bp * lp, d_model)
    x_s = x_sample.reshape(bs * ls, d_model)
    outs = [[] for _ in range(6)]
    for layer in range(depth):
        lw = {
            "norm_mix": norm_mix[layer], "w_in": w_in[layer].astype(_BF16), "lb_logits": hgrn_lb_logits,
            "gnorm": hgrn_gnorm[layer], "w_a": w_a_proj[layer].astype(_BF16), "w_b": w_b_proj[layer].astype(_BF16),
            "w_out": w_out[layer].astype(_BF16), "norm_ffn": norm_ffn[layer],
            "w_query": peer_w_query[layer].astype(_BF16), "sub_keys": peer_sub_keys[layer],
            "peer_u": peer_u[layer].astype(_BF16), "peer_v": peer_v[layer].astype(_BF16), "norm_final": norm_final,
        }
        final = layer == depth - 1
        s0_p = jnp.zeros((bp,) + state_hgrn.shape[2:], _F32)
        x_p, s_p, k_p, v_p = _layer(x_p, bp, lp, s0_p, None, lw, layer, final)
        past = (cache_k[layer], cache_v[layer], page_table)
        x_s, s_s, k_s, v_s = _layer(x_s, bs, ls, state_hgrn[layer], past, lw, layer, final)
        for lst, val in zip(outs, (s_p, s_s, k_p, v_p, k_s, v_s)):
            lst.append(val)
    sp, ss, kp, vp, ks, vs = (jnp.stack(lst, axis=0) for lst in outs)
    return (x_p.reshape(bp, lp, d_model), x_s.reshape(bs, ls, d_model),
            sp.astype(state_hgrn.dtype), ss.astype(state_hgrn.dtype),
            kp.astype(cache_k.dtype), vp.astype(cache_v.dtype), ks.astype(cache_k.dtype), vs.astype(cache_v.dtype))
```

```python
import functools

import jax
import jax.numpy as jnp
from jax import lax
from jax.experimental import pallas as pl
from jax.experimental.pallas import tpu as pltpu

A_HEADS = 8
B_HEADS = 8
MOBA_BLOCK = 256
MOBA_TOPK = 3
PEER_HEADS = 8
PEER_TOPK = 16
RMS_EPS = 1e-6
NEG_INF = -1e30

LANES = 128
SUBLANES = 8
HGRN_CHUNK = 128
HGRN_SUB = 16
VMEM_LIMIT = 48 * 1024 * 1024

_F32 = jnp.float32
_BF16 = jnp.bfloat16
_NT = (((1,), (1,)), ((), ()))


def _params(*sem, vmem=VMEM_LIMIT):
    return pltpu.CompilerParams(dimension_semantics=sem, vmem_limit_bytes=vmem)


def _dot_nt(a, b):
    return lax.dot_general(a, b, _NT, preferred_element_type=_F32)


def _dot(a, b):
    return jnp.dot(a, b, preferred_element_type=_F32)


def _log2(n):
    assert n > 0 and n & (n - 1) == 0, n
    return n.bit_length() - 1


def _div(x, n):
    return lax.shift_right_logical(x, jnp.int32(_log2(n)))


def _mod(x, n):
    return lax.bitwise_and(x, jnp.int32((1 << _log2(n)) - 1))


def _tile_n(n, *cols, cap=512):
    for tn in (1024, 512, 256, LANES):
        if tn <= cap and all(v % tn == 0 for v in (n,) + cols):
            return tn
    raise ValueError((n, cols))


def _rmsnorm_kernel(x_ref, g_ref, o_ref):
    x = x_ref[...]
    y = x * lax.rsqrt(jnp.mean(x * x, axis=-1, keepdims=True) + RMS_EPS)
    o_ref[...] = (y * g_ref[...]).astype(o_ref.dtype)


def _rmsnorm(x, gain, out_dtype):
    t, d = x.shape
    tm = min(t, 512)
    return pl.pallas_call(
        _rmsnorm_kernel,
        grid=(t // tm,),
        in_specs=[pl.BlockSpec((tm, d), lambda i: (i, 0)), pl.BlockSpec((1, d), lambda i: (0, 0))],
        out_specs=pl.BlockSpec((tm, d), lambda i: (i, 0)),
        out_shape=jax.ShapeDtypeStruct((t, d), out_dtype),
        compiler_params=_params("parallel"),
        name="rmsnorm",
    )(x, gain.reshape(1, d).astype(_F32))


def _mm_kernel(a_ref, b_ref, o_ref):
    o_ref[...] = _dot(a_ref[...], b_ref[...]).astype(o_ref.dtype)


def _matmul(a, b, out_dtype=_F32):
    m, k = a.shape
    n = b.shape[1]
    tm = min(m, 1024)
    tn = _tile_n(n, cap=1024)
    return pl.pallas_call(
        _mm_kernel,
        grid=(m // tm, n // tn),
        in_specs=[pl.BlockSpec((tm, k), lambda i, j: (i, 0)), pl.BlockSpec((k, tn), lambda i, j: (0, j))],
        out_specs=pl.BlockSpec((tm, tn), lambda i, j: (i, j)),
        out_shape=jax.ShapeDtypeStruct((m, n), out_dtype),
        compiler_params=_params("parallel", "parallel"),
        name="matmul",
    )(a, b)


def _merge_kernel(oa_ref, ob_ref, wa_ref, wb_ref, ga_ref, gb_ref, o_ref):
    ya = _dot(oa_ref[...].astype(_BF16), wa_ref[...])
    yb = _dot(ob_ref[...].astype(_BF16), wb_ref[...])
    o_ref[...] = (jax.nn.sigmoid(ga_ref[...]) * ya + jax.nn.sigmoid(gb_ref[...]) * yb).astype(o_ref.dtype)


def _merge(oa, ob, wa, wb, proj, ga_col, gb_col, d_model):
    t, ka = oa.shape
    kb = ob.shape[1]
    tm = min(t, 1024)
    tn = _tile_n(d_model, ga_col, gb_col)
    return pl.pallas_call(
        _merge_kernel,
        grid=(t // tm, d_model // tn),
        in_specs=[
            pl.BlockSpec((tm, ka), lambda i, j: (i, 0)),
            pl.BlockSpec((tm, kb), lambda i, j: (i, 0)),
            pl.BlockSpec((ka, tn), lambda i, j: (0, j)),
            pl.BlockSpec((kb, tn), lambda i, j: (0, j)),
            pl.BlockSpec((tm, tn), lambda i, j: (i, ga_col // tn + j)),
            pl.BlockSpec((tm, tn), lambda i, j: (i, gb_col // tn + j)),
        ],
        out_specs=pl.BlockSpec((tm, tn), lambda i, j: (i, j)),
        out_shape=jax.ShapeDtypeStruct((t, d_model), _BF16),
        compiler_params=_params("parallel", "parallel"),
        name="merge",
    )(oa, ob, wa, wb, proj, proj)


def _mm_residual_kernel(a_ref, b_ref, r_ref, o_ref):
    o_ref[...] = r_ref[...] + _dot(a_ref[...], b_ref[...])


def _matmul_residual(a, b, res):
    m, k = a.shape
    n = b.shape[1]
    tm = min(m, 1024)
    tn = _tile_n(n)
    return pl.pallas_call(
        _mm_residual_kernel,
        grid=(m // tm, n // tn),
        in_specs=[
            pl.BlockSpec((tm, k), lambda i, j: (i, 0)),
            pl.BlockSpec((k, tn), lambda i, j: (0, j)),
            pl.BlockSpec((tm, tn), lambda i, j: (i, j)),
        ],
        out_specs=pl.BlockSpec((tm, tn), lambda i, j: (i, j)),
        out_shape=jax.ShapeDtypeStruct((m, n), _F32),
        compiler_params=_params("parallel", "parallel"),
        name="matmul_residual",
    )(a, b, res)


def _hgrn_head(q, fpre, v, ag, logits, gn, st, *, layer, rows):
    c_len, sub = HGRN_CHUNK, HGRN_SUB

    def pad(x):
        if rows == c_len:
            return x
        return jnp.concatenate([x, jnp.zeros((c_len - rows, x.shape[1]), x.dtype)], axis=0)

    ex = jnp.exp(logits - jnp.max(logits, axis=0, keepdims=True))
    lb = jnp.sum(ex[: layer + 1], axis=0, keepdims=True) / jnp.sum(ex, axis=0, keepdims=True)

    forget = lb + (1.0 - lb) * jax.nn.sigmoid(fpre)
    logf = pad(jnp.log(forget))
    kk = pad(1.0 - forget)
    q = pad(q)
    v = pad(v)

    row = lax.broadcasted_iota(jnp.int32, (c_len, LANES), 0)
    b = logf
    shift = 1
    while shift < c_len:
        b = b + jnp.where(row >= shift, pltpu.roll(b, shift, 0), 0.0)
        shift *= 2

    o = _dot_nt((q * jnp.exp(b)).astype(_BF16), st.astype(_BF16))

    rowc = lax.broadcasted_iota(jnp.int32, (sub, LANES), 0)
    lane = lax.broadcasted_iota(jnp.int32, (sub, LANES), 1)
    n_sub = -(-rows // sub)
    att_rows = []
    for i in range(n_sub):
        lo = i * sub
        qi, bi, ki = q[lo:lo + sub], b[lo:lo + sub], kk[lo:lo + sub]
        att = jnp.zeros((sub, LANES), _F32)
        for s in range(sub):
            dec = jnp.exp(jnp.minimum(bi - bi[s:s + 1], 0.0))
            x = jnp.where(rowc >= s, qi * dec * ki[s:s + 1], 0.0)
            att = jnp.where(lane == lo + s, jnp.sum(x, axis=-1, keepdims=True), att)
        if i > 0:
            ref_b = b[lo - 1:lo]
            qt = (qi * jnp.exp(bi - ref_b)).astype(_BF16)
            kp = kk[:lo] * jnp.exp(ref_b - b[:lo])
            kp = jnp.concatenate([kp, jnp.zeros((c_len - lo, LANES), _F32)], axis=0).astype(_BF16)
            att = att + _dot_nt(qt, kp)
        att_rows.append(att)
    if n_sub * sub < c_len:
        att_rows.append(jnp.zeros((c_len - n_sub * sub, LANES), _F32))
    att = jnp.concatenate(att_rows, axis=0)
    o = o + _dot(att.astype(_BF16), v.astype(_BF16))

    b_last = b[c_len - 1:c_len]
    k_hat = (kk * jnp.exp(b_last - b)).astype(_BF16)
    st_new = st * jnp.exp(b_last) + _dot(v.T.astype(_BF16), k_hat)

    oo = o[:rows]
    on = oo * lax.rsqrt(jnp.mean(oo * oo, axis=-1, keepdims=True) + RMS_EPS) * gn
    return on * (ag * jax.nn.sigmoid(ag)), st_new


def _hgrn_kernel(q_ref, f_ref, i_ref, g_ref, lbl_ref, gn_ref, s0_ref, o_ref, sfin_ref, st_ref, *,
                 layer, rows, hb):
    ci = pl.program_id(2)

    @pl.when(ci == 0)
    def _():
        for h in range(hb):
            st_ref[h] = s0_ref[0, h].T

    for h in range(hb):
        cols = slice(h * LANES, (h + 1) * LANES)
        out, st_new = _hgrn_head(q_ref[:, cols], f_ref[:, cols], i_ref[:, cols], g_ref[:, cols],
                                 lbl_ref[:, cols], gn_ref[:, cols], st_ref[h], layer=layer, rows=rows)
        o_ref[:, cols] = out.astype(o_ref.dtype)
        st_ref[h] = st_new

    @pl.when(ci == pl.num_programs(2) - 1)
    def _():
        for h in range(hb):
            sfin_ref[0, h] = st_ref[h].T


def _hgrn(proj, lb_logits, gnorm, s0, layer, bsz, length, width, hb=4):
    heads = A_HEADS
    dk = width // heads
    hb = min(hb, heads)
    assert dk == LANES and heads % hb == 0
    if length >= HGRN_CHUNK:
        assert length % HGRN_CHUNK == 0
        rows, nc = HGRN_CHUNK, length // HGRN_CHUNK
    else:
        assert length % SUBLANES == 0
        rows, nc = length, 1
    sec = heads // hb
    wb = hb * dk

    def col(section):
        return lambda b, h, c: (b * nc + c, section * sec + h)

    return pl.pallas_call(
        functools.partial(_hgrn_kernel, layer=layer, rows=rows, hb=hb),
        grid=(bsz, heads // hb, nc),
        in_specs=[
            pl.BlockSpec((rows, wb), col(0)),
            pl.BlockSpec((rows, wb), col(1)),
            pl.BlockSpec((rows, wb), col(2)),
            pl.BlockSpec((rows, wb), col(3)),
            pl.BlockSpec((lb_logits.shape[0], wb), lambda b, h, c: (0, h)),
            pl.BlockSpec((1, wb), lambda b, h, c: (0, h)),
            pl.BlockSpec((1, hb, dk, dk), lambda b, h, c: (b, h, 0, 0)),
        ],
        out_specs=[
            pl.BlockSpec((rows, wb), lambda b, h, c: (b * nc + c, h)),
            pl.BlockSpec((1, hb, dk, dk), lambda b, h, c: (b, h, 0, 0)),
        ],
        out_shape=[
            jax.ShapeDtypeStruct((bsz * length, width), _BF16 if rows % (2 * SUBLANES) == 0 else _F32),
            jax.ShapeDtypeStruct((bsz, heads, dk, dk), _F32),
        ],
        scratch_shapes=[pltpu.VMEM((hb, dk, dk), _F32)],
        compiler_params=_params("parallel", "parallel", "arbitrary"),
        name="hgrn2",
    )(proj, proj, proj, proj, lb_logits.astype(_F32), gnorm.reshape(1, width).astype(_F32), s0.astype(_F32))


def _moba_prompt_block(q_ref, k_ref, v_ref, o_ref, *, own, nblk):
    blk = MOBA_BLOCK
    hd = q_ref.shape[-1]
    n_sel = min(MOBA_TOPK, nblk, own)
    q = q_ref[...].astype(_BF16)
    k = k_ref[:(own + 1) * blk, :]
    s = _dot_nt(q, k.astype(_BF16)) * (hd ** -0.5)

    sel = [None] * own
    if own > n_sel:
        k_mean = jnp.mean(k[:own * blk].reshape(own, blk, hd), axis=1)
        k_mean = jnp.concatenate([k_mean, jnp.zeros((LANES - own, hd), _F32)], axis=0)
        gate = _dot_nt(k_mean.astype(_BF16), q)
        nrow = -(-own // SUBLANES) * SUBLANES
        blk_id = lax.broadcasted_iota(jnp.int32, (nrow, blk), 0)
        g = jnp.where(blk_id < own, gate[:nrow], NEG_INF)
        rank = jnp.zeros((nrow, blk), jnp.int32)
        for jp in range(own):
            other = g[jp:jp + 1]
            ahead = (other > g) | ((other == g) & (blk_id > jp))
            rank = rank + ahead.astype(jnp.int32)
        picked = jnp.where((rank < n_sel) & (blk_id < own), 1.0, 0.0)
        picked = jnp.concatenate([picked, jnp.zeros((LANES - nrow, blk), _F32)], axis=0).T
        sel = [picked[:, j:j + 1] > 0.0 for j in range(own)]

    row = lax.broadcasted_iota(jnp.int32, (blk, blk), 0)
    col = lax.broadcasted_iota(jnp.int32, (blk, blk), 1)
    masked = []
    for j in range(own + 1):
        sj = s[:, j * blk:(j + 1) * blk]
        if j == own:
            sj = jnp.where(col <= row, sj, NEG_INF)
        elif sel[j] is not None:
            sj = jnp.where(sel[j], sj, NEG_INF)
        masked.append(sj)
    m = masked[0].max(axis=-1, keepdims=True)
    for sj in masked[1:]:
        m = jnp.maximum(m, sj.max(axis=-1, keepdims=True))
    l = jnp.zeros((blk, 1), _F32)
    acc = jnp.zeros((blk, hd), _F32)
    for j, sj in enumerate(masked):
        p = jnp.exp(sj - m)
        l = l + p.sum(axis=-1, keepdims=True)
        acc = acc + _dot(p.astype(_BF16), v_ref[j * blk:(j + 1) * blk, :].astype(_BF16))
    o_ref[...] = (acc / l).astype(o_ref.dtype)


def _moba_prompt_kernel(q_ref, k_ref, v_ref, o_ref, *, nblk):
    i = pl.program_id(2)
    for own in range(nblk):
        pl.when(i == own)(functools.partial(_moba_prompt_block, q_ref, k_ref, v_ref, o_ref, own=own, nblk=nblk))


def _moba_prompt(proj, bsz, length, q_col, k_col, v_col, width):
    heads = B_HEADS
    hd = width // heads
    assert hd == LANES and length % MOBA_BLOCK == 0
    nblk = length // MOBA_BLOCK
    assert nblk <= LANES
    return pl.pallas_call(
        functools.partial(_moba_prompt_kernel, nblk=nblk),
        grid=(bsz, heads, nblk),
        in_specs=[
            pl.BlockSpec((MOBA_BLOCK, hd), lambda b, h, i: (b * nblk + i, q_col // hd + h)),
            pl.BlockSpec((length, hd), lambda b, h, i: (b, k_col // hd + h)),
            pl.BlockSpec((length, hd), lambda b, h, i: (b, v_col // hd + h)),
        ],
        out_specs=pl.BlockSpec((MOBA_BLOCK, hd), lambda b, h, i: (b * nblk + i, h)),
        out_shape=jax.ShapeDtypeStruct((bsz * length, width), _BF16),
        compiler_params=_params("parallel", "parallel", "arbitrary"),
        name="moba_prompt",
    )(proj, proj, proj)


def _moba_sample_kernel(pt_ref, q_ref, kn_ref, vn_ref, *refs, npages, heads, lq, group):
    kc_refs, vc_refs = refs[:group], refs[group:2 * group]
    o_ref, qbd_ref, ks_ref, km_ref, m_ref, l_ref, acc_ref = refs[2 * group:]
    p = pl.program_id(1)
    nsteps = npages // group
    hd = kc_refs[0].shape[2]
    page = kc_refs[0].shape[1] // heads
    width = heads * hd
    rows = heads * lq
    blk_pages = MOBA_BLOCK // page
    nb = npages // blk_pages
    scale = hd ** -0.5

    def by_token(ref):
        return jnp.concatenate([ref[0, pl.ds(h, page, stride=heads), :] for h in range(heads)], axis=1)

    @pl.when(p == 0)
    def _():
        qt = jnp.concatenate([q_ref[...]] * heads, axis=0)
        rh = _div(lax.broadcasted_iota(jnp.int32, (rows, width), 0), lq)
        ch = _div(lax.broadcasted_iota(jnp.int32, (rows, width), 1), hd)
        qbd_ref[...] = jnp.where(rh == ch, qt, 0.0).astype(_BF16)

    qbd = qbd_ref[...]
    for g0 in range(0, group, blk_pages):
        blk = (p * group + g0) // blk_pages
        s = [_dot_nt(qbd, by_token(kc_refs[g0 + g]).astype(_BF16)) * scale for g in range(blk_pages)]
        m = s[0].max(axis=1, keepdims=True)
        for sg in s[1:]:
            m = jnp.maximum(m, sg.max(axis=1, keepdims=True))
        l = jnp.zeros((rows, 1), _F32)
        acc = jnp.zeros((rows, width), _F32)
        for g in range(blk_pages):
            w = jnp.exp(s[g] - m)
            l = l + w.sum(axis=1, keepdims=True)
            acc = acc + _dot(w.astype(_BF16), by_token(vc_refs[g0 + g]).astype(_BF16))
            ks_ref[p * group + g0 + g] = kc_refs[g0 + g][0].reshape(page, heads, hd).sum(axis=0)
        m_ref[blk] = m
        l_ref[blk] = l
        acc_ref[blk] = acc

    @pl.when(p == nsteps - 1)
    def _():
        k_sum = ks_ref[...].reshape(nb, blk_pages, heads, hd).sum(axis=1)
        km_ref[...] = k_sum.reshape(nb * heads, hd) * (1.0 / MOBA_BLOCK)
        k_mean = jnp.concatenate([km_ref[pl.ds(h, nb, stride=heads), :] for h in range(heads)], axis=1)
        gate = _dot_nt(qbd, k_mean.astype(_BF16))
        lane = lax.broadcasted_iota(jnp.int32, (rows, nb), 1)
        sel = jnp.zeros((rows, nb), _F32)
        for _ in range(MOBA_TOPK):
            best = jnp.max(gate, axis=1, keepdims=True)
            idx = jnp.min(jnp.where(gate == best, lane, nb), axis=1, keepdims=True)
            hit = lane == idx
            sel = jnp.where(hit, 1.0, sel)
            gate = jnp.where(hit, -jnp.inf, gate)

        pad = 2 * SUBLANES - lq
        kn = jnp.concatenate([kn_ref[...], jnp.zeros((pad, width), _F32)], axis=0).astype(_BF16)
        vn = jnp.concatenate([vn_ref[...], jnp.zeros((pad, width), _F32)], axis=0).astype(_BF16)
        s_own = _dot_nt(qbd, kn) * scale
        qi = _mod(lax.broadcasted_iota(jnp.int32, s_own.shape, 0), lq)
        kj = lax.broadcasted_iota(jnp.int32, s_own.shape, 1)
        s_own = jnp.where((kj <= qi) & (kj < lq), s_own, NEG_INF)

        picked = [sel[:, b:b + 1] > 0.0 for b in range(nb)]
        top = jnp.max(s_own, axis=1, keepdims=True)
        for b in range(nb):
            top = jnp.maximum(top, jnp.where(picked[b], m_ref[b], NEG_INF))
        p_own = jnp.exp(s_own - top)
        l = jnp.sum(p_own, axis=1, keepdims=True)
        out = _dot(p_own.astype(_BF16), vn)
        for b in range(nb):
            c = jnp.where(picked[b], jnp.exp(jnp.minimum(m_ref[b] - top, 0.0)), 0.0)
            l = l + c * l_ref[b]
            out = out + c * acc_ref[b]
        out = out / l
        for h in range(heads):
            o_ref[:, h * hd:(h + 1) * hd] = out[h * lq:(h + 1) * lq, h * hd:(h + 1) * hd]


def _moba_sample(proj, cache_k, cache_v, page_table, bsz, lq, q_col, k_col, v_col, width):
    heads = B_HEADS
    hd = width // heads
    n_phys, page = cache_k.shape[0], cache_k.shape[1]
    npages = page_table.shape[1]
    past = npages * page
    assert MOBA_BLOCK % page == 0 and past % MOBA_BLOCK == 0 and lq % SUBLANES == 0 and lq <= 2 * SUBLANES
    assert past // MOBA_BLOCK >= MOBA_TOPK and heads % SUBLANES == 0
    blk_pages = MOBA_BLOCK // page
    nb = past // MOBA_BLOCK
    group = blk_pages * max(g for g in (1, 2, 4) if nb % g == 0)
    nsteps = npages // group
    rows = heads * lq
    kc = cache_k.reshape(n_phys, page * heads, hd)
    vc = cache_v.reshape(n_phys, page * heads, hd)

    def nth_page(g):
        return lambda b, p, pt: (pt[b, p * group + g], 0, 0)

    page_specs = [pl.BlockSpec((1, page * heads, hd), nth_page(g)) for g in range(group)] * 2
    grid_spec = pltpu.PrefetchScalarGridSpec(
        num_scalar_prefetch=1,
        grid=(bsz, nsteps),
        in_specs=[
            pl.BlockSpec((lq, width), lambda b, p, pt: (b, q_col // width)),
            pl.BlockSpec((lq, width), lambda b, p, pt: (b, k_col // width)),
            pl.BlockSpec((lq, width), lambda b, p, pt: (b, v_col // width)),
        ] + page_specs,
        out_specs=pl.BlockSpec((lq, width), lambda b, p, pt: (b, 0)),
        scratch_shapes=[
            pltpu.VMEM((rows, width), _BF16),
            pltpu.VMEM((npages, heads, hd), _F32),
            pltpu.VMEM((nb * heads, hd), _F32),
            pltpu.VMEM((nb, rows, 1), _F32),
            pltpu.VMEM((nb, rows, 1), _F32),
            pltpu.VMEM((nb, rows, width), _F32),
        ],
    )
    return pl.pallas_call(
        functools.partial(_moba_sample_kernel, npages=npages, heads=heads, lq=lq, group=group),
        grid_spec=grid_spec,
        out_shape=jax.ShapeDtypeStruct((bsz * lq, width), _F32),
        compiler_params=_params("parallel", "arbitrary"),
        name="moba_sample",
    )(page_table.astype(jnp.int32), proj, proj, proj, *([kc] * group), *([vc] * group))


def _top_rows(x, k):
    r, t = x.shape
    rowi = lax.broadcasted_iota(jnp.int32, (r, t), 0).astype(_F32)
    slot = lax.broadcasted_iota(jnp.int32, (k, t), 0)
    vals = jnp.zeros((k, t), _F32)
    idxs = jnp.zeros((k, t), _F32)
    for n in range(k):
        best = jnp.max(x, axis=0, keepdims=True)
        idx = jnp.min(jnp.where(x == best, rowi, float(r)), axis=0, keepdims=True)
        vals = jnp.where(slot == n, best, vals)
        idxs = jnp.where(slot == n, idx, idxs)
        x = jnp.where(rowi == idx, -jnp.inf, x)
    return vals, idxs


def _candidate_pieces(topk):
    pieces, start = [], 0
    a = 0
    while topk // (a + 1) > 1:
        nb = topk // (a + 1)
        rows = -(-nb // SUBLANES) * SUBLANES
        pieces.append((start, rows, a, nb))
        start += rows
        a += 1
    return pieces, (start, a)


def _peer_retrieve_kernel(q_ref, sk_ref, g_ref, i1_ref, i2_ref):
    topk = PEER_TOPK
    dh = sk_ref.shape[-1]
    tm = q_ref.shape[0]
    tops = []
    for a in range(2):
        qa = q_ref[:, a * dh:(a + 1) * dh].astype(_BF16)
        tops.append(_top_rows(_dot_nt(sk_ref[0, a], qa), topk))
    (s1, k1), (s2, k2) = tops

    pieces, (tail_start, tail_a) = _candidate_pieces(topk)
    parts = []
    for _, rows, a, nb in pieces:
        part = s1[a:a + 1] + s2[:rows]
        if nb < rows:
            part = jnp.where(lax.broadcasted_iota(jnp.int32, (rows, tm), 0) < nb, part, -jnp.inf)
        parts.append(part)
    tail = s1[tail_a:] + s2[0:1]
    tail_rows = -(-(topk - tail_a) // SUBLANES) * SUBLANES
    if tail_rows > topk - tail_a:
        tail = jnp.concatenate([tail, jnp.full((tail_rows - (topk - tail_a), tm), -jnp.inf, _F32)], axis=0)
    cand = jnp.concatenate(parts + [tail], axis=0)
    best_s, pos = _top_rows(cand, topk)

    a_sel = pos - float(tail_start - tail_a)
    b_sel = jnp.zeros_like(pos)
    for start, rows, a, _ in pieces:
        inside = (pos >= float(start)) & (pos < float(start + rows))
        a_sel = jnp.where(inside, float(a), a_sel)
        b_sel = jnp.where(inside, pos - float(start), b_sel)
    i1 = jnp.zeros_like(pos)
    i2 = jnp.zeros_like(pos)
    for a in range(topk):
        i1 = jnp.where(a_sel == float(a), k1[a:a + 1], i1)
        i2 = jnp.where(b_sel == float(a), k2[a:a + 1], i2)
    e = jnp.exp(best_s - best_s[0:1])
    g_ref[...] = e / jnp.sum(e, axis=0, keepdims=True)
    i1_ref[...] = i1
    i2_ref[...] = i2


def _peer_retrieve(q, sub_keys, tm=512):
    t = q.shape[0]
    heads, _, nkeys, dh = sub_keys.shape
    tm = min(tm, t)
    spec = pl.BlockSpec((PEER_TOPK, tm), lambda i, h: (h, i))
    shape = jax.ShapeDtypeStruct((heads * PEER_TOPK, t), _F32)
    return pl.pallas_call(
        _peer_retrieve_kernel,
        grid=(t // tm, heads),
        in_specs=[
            pl.BlockSpec((tm, 2 * dh), lambda i, h: (i, h)),
            pl.BlockSpec((1, 2, nkeys, dh), lambda i, h: (h, 0, 0, 0)),
        ],
        out_specs=[spec, spec, spec],
        out_shape=[shape, shape, shape],
        compiler_params=_params("parallel", "parallel"),
        name="peer_retrieve",
    )(q, sub_keys.astype(_BF16))


def _peer_weights_kernel(g_ref, i1_ref, i2_ref, w_ref, gt_ref, i1t_ref, i2t_ref, scr_ref, *, tm, stride):
    nk = LANES
    gt_ref[...] = g_ref[...].T
    i1t_ref[...] = i1_ref[...].T
    i2t_ref[...] = i2_ref[...].T
    sub = lax.broadcasted_iota(jnp.int32, (nk, nk), 0).astype(_F32)

    zeros = jnp.zeros((nk, nk), _F32)

    def one_hots(n):
        r1 = jnp.broadcast_to(i1t_ref[pl.ds(n, 1), :], (nk, nk))
        r2 = jnp.broadcast_to(i2t_ref[pl.ds(n, 1), :], (nk, nk))
        rg = jnp.broadcast_to(gt_ref[pl.ds(n, 1), :], (nk, nk))
        return jnp.where(sub == r1, 1.0, 0.0), jnp.where(sub == r2, rg, 0.0)

    def body(t, carry):
        n = 2 * t
        a1, a2 = one_hots(n)
        b1, b2 = one_hots(n + 1)
        m1 = jnp.concatenate([a1, b1], axis=1).astype(_BF16)
        m2 = jnp.concatenate([jnp.concatenate([a2, zeros], axis=1),
                              jnp.concatenate([zeros, b2], axis=1)], axis=0).astype(_BF16)
        planes = _dot_nt(m1, m2)
        scr_ref[pl.ds(n, nk, stride=stride), :] = planes[:, :nk]
        scr_ref[pl.ds(n + 1, nk, stride=stride), :] = planes[:, nk:]
        return carry

    lax.fori_loop(0, tm // 2, body, 0, unroll=4 * SUBLANES)
    for j in range(nk):
        w_ref[:, j * nk:(j + 1) * nk] = scr_ref[j * stride:j * stride + tm, :].astype(w_ref.dtype)


def _peer_weights(g, i1, i2, nkeys, tm=256):
    picks, t = g.shape
    assert picks == LANES and nkeys == LANES
    tm = min(tm, t)
    stride = tm + SUBLANES
    spec = pl.BlockSpec((picks, tm), lambda i: (0, i))
    return pl.pallas_call(
        functools.partial(_peer_weights_kernel, tm=tm, stride=stride),
        grid=(t // tm,),
        in_specs=[spec, spec, spec],
        out_specs=pl.BlockSpec((tm, nkeys * nkeys), lambda i: (i, 0)),
        out_shape=jax.ShapeDtypeStruct((t, nkeys * nkeys), _BF16),
        scratch_shapes=[
            pltpu.VMEM((tm, picks), _F32),
            pltpu.VMEM((tm, picks), _F32),
            pltpu.VMEM((tm, picks), _F32),
            pltpu.VMEM((nkeys * stride, nkeys), _F32),
        ],
        compiler_params=_params("parallel"),
        name="peer_weights",
    )(g, i1, i2)


def _peer_ffn_kernel(xn_ref, w_ref, u_ref, v_ref, o_ref, *, tn):
    j = pl.program_id(1)

    @pl.when(j == 0)
    def _():
        o_ref[...] = jnp.zeros_like(o_ref)

    xn = xn_ref[...]
    parts = []
    for e in range(0, u_ref.shape[0], tn):
        h = _dot_nt(xn, u_ref[e:e + tn, :])
        w = w_ref[:, e:e + tn].astype(_F32)
        parts.append(jnp.where(w != 0.0, w * jax.nn.gelu(h), 0.0).astype(_BF16))
    coef = jnp.concatenate(parts, axis=1)
    for n in range(0, o_ref.shape[1], tn):
        o_ref[:, n:n + tn] += _dot(coef, v_ref[:, n:n + tn])


def _peer_ffn(xn, w, u, v, tm=1024, te=1024):
    t, d = xn.shape
    n_exp = u.shape[0]
    tm = min(tm, t)
    tn = _tile_n(d)
    blocks = 2 * (2 * (tm * d + tm * te + 2 * te * d) + 4 * tm * d)
    temps = tm * te * 2 + 4 * tm * tn * 4
    return pl.pallas_call(
        functools.partial(_peer_ffn_kernel, tn=tn),
        grid=(t // tm, n_exp // te),
        in_specs=[
            pl.BlockSpec((tm, d), lambda i, j: (i, 0)),
            pl.BlockSpec((tm, te), lambda i, j: (i, j)),
            pl.BlockSpec((te, d), lambda i, j: (j, 0)),
            pl.BlockSpec((te, d), lambda i, j: (j, 0)),
        ],
        out_specs=pl.BlockSpec((tm, d), lambda i, j: (i, 0)),
        out_shape=jax.ShapeDtypeStruct((t, d), _F32),
        compiler_params=_params("parallel", "arbitrary", vmem=max(VMEM_LIMIT, blocks + temps)),
        name="peer_ffn",
    )(xn, w, u, v)


def _residual_norm_kernel(x_ref, y_ref, g_ref, o_ref, *, final):
    x = x_ref[...] + y_ref[...]
    if final:
        x = x * lax.rsqrt(jnp.mean(x * x, axis=-1, keepdims=True) + RMS_EPS) * g_ref[...]
    o_ref[...] = x


def _residual_norm(x, y, gain, final):
    t, d = x.shape
    tm = min(t, 512)
    spec = pl.BlockSpec((tm, d), lambda i: (i, 0))
    return pl.pallas_call(
        functools.partial(_residual_norm_kernel, final=final),
        grid=(t // tm,),
        in_specs=[spec, spec, pl.BlockSpec((1, d), lambda i: (0, 0))],
        out_specs=spec,
        out_shape=jax.ShapeDtypeStruct((t, d), _F32),
        compiler_params=_params("parallel"),
        name="residual_norm",
    )(x, y, gain.reshape(1, d).astype(_F32))


def _layer(x, bsz, length, s0, past, lw, layer, final):
    d_model = x.shape[1]
    a_width = lw["gnorm"].shape[-1]
    b_width = lw["w_b"].shape[0]
    hn = _rmsnorm(x, lw["norm_mix"], _BF16)
    proj = _matmul(hn, lw["w_in"])
    b_q = 4 * a_width
    b_k, b_v = b_q + b_width, b_q + 2 * b_width
    g_a = b_q + 3 * b_width
    g_b = g_a + d_model

    o_a, s_new = _hgrn(proj, lw["lb_logits"], lw["gnorm"], s0, layer, bsz, length, a_width)
    if past is None:
        o_b = _moba_prompt(proj, bsz, length, b_q, b_k, b_v, b_width)
    else:
        o_b = _moba_sample(proj, past[0], past[1], past[2], bsz, length, b_q, b_k, b_v, b_width)
    merged = _merge(o_a, o_b, lw["w_a"], lw["w_b"], proj, g_a, g_b, d_model)
    x = _matmul_residual(merged, lw["w_out"], x)

    xn = _rmsnorm(x, lw["norm_ffn"], _BF16)
    q = _matmul(xn, lw["w_query"])
    gates, i1, i2 = _peer_retrieve(q, lw["sub_keys"])
    w = _peer_weights(gates, i1, i2, lw["sub_keys"].shape[2])
    x = _residual_norm(x, _peer_ffn(xn, w, lw["peer_u"], lw["peer_v"]), lw["norm_final"], final)

    k_new = proj[:, b_k:b_k + b_width].reshape(bsz, length, B_HEADS, b_width // B_HEADS)
    v_new = proj[:, b_v:b_v + b_width].reshape(bsz, length, B_HEADS, b_width // B_HEADS)
    return x, s_new, k_new, v_new


def kernel(x_prompt, x_sample, cache_k, cache_v, state_hgrn, page_table, norm_mix, w_in, hgrn_lb_logits,
           hgrn_gnorm, w_a_proj, w_b_proj, w_out, norm_ffn, peer_w_query, peer_sub_keys, peer_u, peer_v,
           norm_final):
    depth = w_in.shape[0]
    bp, lp, d_model = x_prompt.shape
    bs, ls, _ = x_sample.shape
    x_p = x_prompt.reshape(bp * lp, d_model)
    x_s = x_sample.reshape(bs * ls, d_model)
    outs = [[] for _ in range(6)]
    for layer in range(depth):
        lw = {
            "norm_mix": norm_mix[layer], "w_in": w_in[layer].astype(_BF16), "lb_logits": hgrn_lb_logits,
            "gnorm": hgrn_gnorm[layer], "w_a": w_a_proj[layer].astype(_BF16), "w_b": w_b_proj[layer].astype(_BF16),
            "w_out": w_out[layer].astype(_BF16), "norm_ffn": norm_ffn[layer],
            "w_query": peer_w_query[layer].astype(_BF16), "sub_keys": peer_sub_keys[layer],
            "peer_u": peer_u[layer].astype(_BF16), "peer_v": peer_v[layer].astype(_BF16), "norm_final": norm_final,
        }
        final = layer == depth - 1
        s0_p = jnp.zeros((bp,) + state_hgrn.shape[2:], _F32)
        x_p, s_p, k_p, v_p = _layer(x_p, bp, lp, s0_p, None, lw, layer, final)
        past = (cache_k[layer], cache_v[layer], page_table)
        x_s, s_s, k_s, v_s = _layer(x_s, bs, ls, state_hgrn[layer], past, lw, layer, final)
        for lst, val in zip(outs, (s_p, s_s, k_p, v_p, k_s, v_s)):
            lst.append(val)
    sp, ss, kp, vp, ks, vs = (jnp.stack(lst, axis=0) for lst in outs)
    return (x_p.reshape(bp, lp, d_model), x_s.reshape(bs, ls, d_model),
            sp.astype(state_hgrn.dtype), ss.astype(state_hgrn.dtype),
            kp.astype(cache_k.dtype), vp.astype(cache_v.dtype), ks.astype(cache_k.dtype), vs.astype(cache_v.dtype))
```

```python
import functools

import jax
import jax.numpy as jnp
from jax import lax
from jax.experimental import pallas as pl
from jax.experimental.pallas import tpu as pltpu

A_HEADS = 8
B_HEADS = 8
MOBA_BLOCK = 256
MOBA_TOPK = 3
PEER_HEADS = 8
PEER_TOPK = 16
RMS_EPS = 1e-6
NEG_INF = -1e30

LANES = 128
SUBLANES = 8
HGRN_CHUNK = 128
HGRN_SUB = 16
HGRN_SAFE_SPAN = 40.0
VMEM_LIMIT = 48 * 1024 * 1024

_F32 = jnp.float32
_BF16 = jnp.bfloat16
_NT = (((1,), (1,)), ((), ()))


def _params(*sem, vmem=VMEM_LIMIT):
    return pltpu.CompilerParams(dimension_semantics=sem, vmem_limit_bytes=vmem)


def _dot_nt(a, b):
    return lax.dot_general(a, b, _NT, preferred_element_type=_F32)


def _dot(a, b):
    return jnp.dot(a, b, preferred_element_type=_F32)


def _log2(n):
    assert n > 0 and n & (n - 1) == 0, n
    return n.bit_length() - 1


def _div(x, n):
    return lax.shift_right_logical(x, jnp.int32(_log2(n)))


def _mod(x, n):
    return lax.bitwise_and(x, jnp.int32((1 << _log2(n)) - 1))


def _tile_n(n, *cols, cap=512):
    for tn in (1024, 512, 256, LANES):
        if tn <= cap and all(v % tn == 0 for v in (n,) + cols):
            return tn
    raise ValueError((n, cols))


def _rmsnorm_kernel(x_ref, g_ref, o_ref):
    x = x_ref[...]
    y = x * lax.rsqrt(jnp.mean(x * x, axis=-1, keepdims=True) + RMS_EPS)
    o_ref[...] = (y * g_ref[...]).astype(o_ref.dtype)


def _rmsnorm(x, gain, out_dtype):
    t, d = x.shape
    tm = min(t, 512)
    return pl.pallas_call(
        _rmsnorm_kernel,
        grid=(t // tm,),
        in_specs=[pl.BlockSpec((tm, d), lambda i: (i, 0)), pl.BlockSpec((1, d), lambda i: (0, 0))],
        out_specs=pl.BlockSpec((tm, d), lambda i: (i, 0)),
        out_shape=jax.ShapeDtypeStruct((t, d), out_dtype),
        compiler_params=_params("parallel"),
        name="rmsnorm",
    )(x, gain.reshape(1, d).astype(_F32))


def _mm_kernel(a_ref, b_ref, o_ref):
    o_ref[...] = _dot(a_ref[...], b_ref[...]).astype(o_ref.dtype)


def _matmul(a, b, out_dtype=_F32):
    m, k = a.shape
    n = b.shape[1]
    tm = min(m, 2048)
    tn = _tile_n(n, cap=512 if tm > 1024 else 1024)
    return pl.pallas_call(
        _mm_kernel,
        grid=(m // tm, n // tn),
        in_specs=[pl.BlockSpec((tm, k), lambda i, j: (i, 0)), pl.BlockSpec((k, tn), lambda i, j: (0, j))],
        out_specs=pl.BlockSpec((tm, tn), lambda i, j: (i, j)),
        out_shape=jax.ShapeDtypeStruct((m, n), out_dtype),
        compiler_params=_params("parallel", "parallel"),
        name="matmul",
    )(a, b)


def _merge_kernel(oa_ref, ob_ref, wa_ref, wb_ref, ga_ref, gb_ref, o_ref):
    ya = _dot(oa_ref[...].astype(_BF16), wa_ref[...])
    yb = _dot(ob_ref[...].astype(_BF16), wb_ref[...])
    o_ref[...] = (jax.nn.sigmoid(ga_ref[...]) * ya + jax.nn.sigmoid(gb_ref[...]) * yb).astype(o_ref.dtype)


def _merge(oa, ob, wa, wb, proj, ga_col, gb_col, d_model):
    t, ka = oa.shape
    kb = ob.shape[1]
    tm = min(t, 1024)
    tn = _tile_n(d_model, ga_col, gb_col)
    return pl.pallas_call(
        _merge_kernel,
        grid=(t // tm, d_model // tn),
        in_specs=[
            pl.BlockSpec((tm, ka), lambda i, j: (i, 0)),
            pl.BlockSpec((tm, kb), lambda i, j: (i, 0)),
            pl.BlockSpec((ka, tn), lambda i, j: (0, j)),
            pl.BlockSpec((kb, tn), lambda i, j: (0, j)),
            pl.BlockSpec((tm, tn), lambda i, j: (i, ga_col // tn + j)),
            pl.BlockSpec((tm, tn), lambda i, j: (i, gb_col // tn + j)),
        ],
        out_specs=pl.BlockSpec((tm, tn), lambda i, j: (i, j)),
        out_shape=jax.ShapeDtypeStruct((t, d_model), _BF16),
        compiler_params=_params("parallel", "parallel"),
        name="merge",
    )(oa, ob, wa, wb, proj, proj)


def _mm_residual_kernel(a_ref, b_ref, r_ref, o_ref):
    o_ref[...] = r_ref[...] + _dot(a_ref[...], b_ref[...])


def _matmul_residual(a, b, res):
    m, k = a.shape
    n = b.shape[1]
    tm = min(m, 1024)
    tn = _tile_n(n)
    return pl.pallas_call(
        _mm_residual_kernel,
        grid=(m // tm, n // tn),
        in_specs=[
            pl.BlockSpec((tm, k), lambda i, j: (i, 0)),
            pl.BlockSpec((k, tn), lambda i, j: (0, j)),
            pl.BlockSpec((tm, tn), lambda i, j: (i, j)),
        ],
        out_specs=pl.BlockSpec((tm, tn), lambda i, j: (i, j)),
        out_shape=jax.ShapeDtypeStruct((m, n), _F32),
        compiler_params=_params("parallel", "parallel"),
        name="matmul_residual",
    )(a, b, res)


def _hgrn_inputs(q, fpre, v, logits, *, layer, rows):
    c_len = HGRN_CHUNK

    def pad(x):
        if rows == c_len:
            return x
        return jnp.concatenate([x, jnp.zeros((c_len - rows, x.shape[1]), x.dtype)], axis=0)

    ex = jnp.exp(logits - jnp.max(logits, axis=0, keepdims=True))
    lb = jnp.sum(ex[: layer + 1], axis=0, keepdims=True) / jnp.sum(ex, axis=0, keepdims=True)

    forget = lb + (1.0 - lb) * jax.nn.sigmoid(fpre)
    b = pad(jnp.log(forget))
    row = lax.broadcasted_iota(jnp.int32, (c_len, LANES), 0)
    shift = 1
    while shift < c_len:
        b = b + jnp.where(row >= shift, pltpu.roll(b, shift, 0), 0.0)
        shift *= 2
    return pad(q), pad(1.0 - forget), pad(v), b


def _hgrn_start(b, i):
    return b[i * HGRN_SUB - 1:i * HGRN_SUB] if i > 0 else jnp.zeros((1, LANES), _F32)


def _hgrn_att_pairwise(q, kk, b, n_sub):
    c_len, sub = HGRN_CHUNK, HGRN_SUB
    rowc = lax.broadcasted_iota(jnp.int32, (sub, LANES), 0)
    lane = lax.broadcasted_iota(jnp.int32, (sub, LANES), 1)
    att_rows = []
    for i in range(n_sub):
        lo = i * sub
        qi, bi, ki = q[lo:lo + sub], b[lo:lo + sub], kk[lo:lo + sub]
        att = jnp.zeros((sub, LANES), _F32)
        for s in range(sub):
            dec = jnp.exp(jnp.minimum(bi - bi[s:s + 1], 0.0))
            x = jnp.where(rowc >= s, qi * dec * ki[s:s + 1], 0.0)
            att = jnp.where(lane == lo + s, jnp.sum(x, axis=-1, keepdims=True), att)
        if i > 0:
            ref_b = _hgrn_start(b, i)
            qt = (qi * jnp.exp(bi - ref_b)).astype(_BF16)
            kp = kk[:lo] * jnp.exp(ref_b - b[:lo])
            kp = jnp.concatenate([kp, jnp.zeros((c_len - lo, LANES), _F32)], axis=0).astype(_BF16)
            att = att + _dot_nt(qt, kp)
        att_rows.append(att)
    if n_sub * sub < c_len:
        att_rows.append(jnp.zeros((c_len - n_sub * sub, LANES), _F32))
    return jnp.concatenate(att_rows, axis=0)


def _hgrn_att_factored(q, kk, b, n_sub):
    c_len, sub = HGRN_CHUNK, HGRN_SUB
    rowc = lax.broadcasted_iota(jnp.int32, (sub, LANES), 0)
    lane = lax.broadcasted_iota(jnp.int32, (sub, LANES), 1)
    att_rows = []
    for i in range(n_sub):
        lo, hi = i * sub, (i + 1) * sub
        ref_b = _hgrn_start(b, i)
        qt = (q[lo:hi] * jnp.exp(b[lo:hi] - ref_b)).astype(_BF16)
        kp = kk[:hi] * jnp.exp(ref_b - b[:hi])
        if hi < c_len:
            kp = jnp.concatenate([kp, jnp.zeros((c_len - hi, LANES), _F32)], axis=0)
        att_rows.append(jnp.where(lane <= rowc + lo, _dot_nt(qt, kp.astype(_BF16)), 0.0))
    if n_sub * sub < c_len:
        att_rows.append(jnp.zeros((c_len - n_sub * sub, LANES), _F32))
    return jnp.concatenate(att_rows, axis=0)


def _hgrn_outputs(q, kk, v, b, att, st, ag, gn, rows):
    c_len = HGRN_CHUNK
    o = _dot_nt((q * jnp.exp(b)).astype(_BF16), st.astype(_BF16))
    o = o + _dot(att.astype(_BF16), v.astype(_BF16))
    b_last = b[c_len - 1:c_len]
    k_hat = (kk * jnp.exp(b_last - b)).astype(_BF16)
    st_new = st * jnp.exp(b_last) + _dot(v.T.astype(_BF16), k_hat)
    oo = o[:rows]
    on = oo * lax.rsqrt(jnp.mean(oo * oo, axis=-1, keepdims=True) + RMS_EPS) * gn
    return on * (ag * jax.nn.sigmoid(ag)), st_new


def _hgrn_kernel(q_ref, f_ref, i_ref, g_ref, lbl_ref, gn_ref, s0_ref, o_ref, sfin_ref, st_ref, *,
                 layer, rows, hb):
    ci = pl.program_id(2)
    sub = HGRN_SUB
    n_sub = -(-rows // sub)

    @pl.when(ci == 0)
    def _():
        for h in range(hb):
            st_ref[h] = s0_ref[0, h].T

    cols = [slice(h * LANES, (h + 1) * LANES) for h in range(hb)]
    heads = [_hgrn_inputs(q_ref[:, c], f_ref[:, c], i_ref[:, c], lbl_ref[:, c], layer=layer, rows=rows)
             for c in cols]
    spans = [_hgrn_start(b, i) - b[(i + 1) * sub - 1:(i + 1) * sub] for _, _, _, b in heads for i in range(n_sub)]
    worst = jnp.max(jnp.concatenate(spans, axis=0))
    atts = lax.cond(worst < HGRN_SAFE_SPAN,
                    lambda: tuple(_hgrn_att_factored(q, kk, b, n_sub) for q, kk, _, b in heads),
                    lambda: tuple(_hgrn_att_pairwise(q, kk, b, n_sub) for q, kk, _, b in heads))
    for h, c in enumerate(cols):
        q, kk, v, b = heads[h]
        out, st_new = _hgrn_outputs(q, kk, v, b, atts[h], st_ref[h], g_ref[:, c], gn_ref[:, c], rows)
        o_ref[:, c] = out.astype(o_ref.dtype)
        st_ref[h] = st_new

    @pl.when(ci == pl.num_programs(2) - 1)
    def _():
        for h in range(hb):
            sfin_ref[0, h] = st_ref[h].T


def _hgrn(proj, lb_logits, gnorm, s0, layer, bsz, length, width, hb=4):
    heads = A_HEADS
    dk = width // heads
    hb = min(hb, heads)
    assert dk == LANES and heads % hb == 0
    if length >= HGRN_CHUNK:
        assert length % HGRN_CHUNK == 0
        rows, nc = HGRN_CHUNK, length // HGRN_CHUNK
    else:
        assert length % SUBLANES == 0
        rows, nc = length, 1
    sec = heads // hb
    wb = hb * dk

    def col(section):
        return lambda b, h, c: (b * nc + c, section * sec + h)

    return pl.pallas_call(
        functools.partial(_hgrn_kernel, layer=layer, rows=rows, hb=hb),
        grid=(bsz, heads // hb, nc),
        in_specs=[
            pl.BlockSpec((rows, wb), col(0)),
            pl.BlockSpec((rows, wb), col(1)),
            pl.BlockSpec((rows, wb), col(2)),
            pl.BlockSpec((rows, wb), col(3)),
            pl.BlockSpec((lb_logits.shape[0], wb), lambda b, h, c: (0, h)),
            pl.BlockSpec((1, wb), lambda b, h, c: (0, h)),
            pl.BlockSpec((1, hb, dk, dk), lambda b, h, c: (b, h, 0, 0)),
        ],
        out_specs=[
            pl.BlockSpec((rows, wb), lambda b, h, c: (b * nc + c, h)),
            pl.BlockSpec((1, hb, dk, dk), lambda b, h, c: (b, h, 0, 0)),
        ],
        out_shape=[
            jax.ShapeDtypeStruct((bsz * length, width), _BF16 if rows % (2 * SUBLANES) == 0 else _F32),
            jax.ShapeDtypeStruct((bsz, heads, dk, dk), _F32),
        ],
        scratch_shapes=[pltpu.VMEM((hb, dk, dk), _F32)],
        compiler_params=_params("parallel", "parallel", "arbitrary"),
        name="hgrn2",
    )(proj, proj, proj, proj, lb_logits.astype(_F32), gnorm.reshape(1, width).astype(_F32), s0.astype(_F32))


def _moba_prompt_block(q_ref, k_ref, v_ref, o_ref, *, own, nblk):
    blk = MOBA_BLOCK
    hd = q_ref.shape[-1]
    n_sel = min(MOBA_TOPK, nblk, own)
    q = q_ref[...].astype(_BF16)
    k = k_ref[:(own + 1) * blk, :]
    s = _dot_nt(q, k.astype(_BF16)) * (hd ** -0.5)

    sel = [None] * own
    if own > n_sel:
        k_mean = jnp.mean(k[:own * blk].reshape(own, blk, hd), axis=1)
        k_mean = jnp.concatenate([k_mean, jnp.zeros((LANES - own, hd), _F32)], axis=0)
        gate = _dot_nt(k_mean.astype(_BF16), q)
        nrow = -(-own // SUBLANES) * SUBLANES
        blk_id = lax.broadcasted_iota(jnp.int32, (nrow, blk), 0)
        g = jnp.where(blk_id < own, gate[:nrow], NEG_INF)
        rank = jnp.zeros((nrow, blk), jnp.int32)
        for jp in range(own):
            other = g[jp:jp + 1]
            ahead = (other > g) | ((other == g) & (blk_id > jp))
            rank = rank + ahead.astype(jnp.int32)
        picked = jnp.where((rank < n_sel) & (blk_id < own), 1.0, 0.0)
        picked = jnp.concatenate([picked, jnp.zeros((LANES - nrow, blk), _F32)], axis=0).T
        sel = [picked[:, j:j + 1] > 0.0 for j in range(own)]

    row = lax.broadcasted_iota(jnp.int32, (blk, blk), 0)
    col = lax.broadcasted_iota(jnp.int32, (blk, blk), 1)
    masked = []
    for j in range(own + 1):
        sj = s[:, j * blk:(j + 1) * blk]
        if j == own:
            sj = jnp.where(col <= row, sj, NEG_INF)
        elif sel[j] is not None:
            sj = jnp.where(sel[j], sj, NEG_INF)
        masked.append(sj)
    m = masked[0].max(axis=-1, keepdims=True)
    for sj in masked[1:]:
        m = jnp.maximum(m, sj.max(axis=-1, keepdims=True))
    l = jnp.zeros((blk, 1), _F32)
    acc = jnp.zeros((blk, hd), _F32)
    for j, sj in enumerate(masked):
        p = jnp.exp(sj - m)
        l = l + p.sum(axis=-1, keepdims=True)
        acc = acc + _dot(p.astype(_BF16), v_ref[j * blk:(j + 1) * blk, :].astype(_BF16))
    o_ref[...] = (acc / l).astype(o_ref.dtype)


def _moba_prompt_kernel(q_ref, k_ref, v_ref, o_ref, *, nblk):
    i = pl.program_id(2)
    for own in range(nblk):
        pl.when(i == own)(functools.partial(_moba_prompt_block, q_ref, k_ref, v_ref, o_ref, own=own, nblk=nblk))


def _moba_prompt(proj, bsz, length, q_col, k_col, v_col, width):
    heads = B_HEADS
    hd = width // heads
    assert hd == LANES and length % MOBA_BLOCK == 0
    nblk = length // MOBA_BLOCK
    assert nblk <= LANES
    return pl.pallas_call(
        functools.partial(_moba_prompt_kernel, nblk=nblk),
        grid=(bsz, heads, nblk),
        in_specs=[
            pl.BlockSpec((MOBA_BLOCK, hd), lambda b, h, i: (b * nblk + i, q_col // hd + h)),
            pl.BlockSpec((length, hd), lambda b, h, i: (b, k_col // hd + h)),
            pl.BlockSpec((length, hd), lambda b, h, i: (b, v_col // hd + h)),
        ],
        out_specs=pl.BlockSpec((MOBA_BLOCK, hd), lambda b, h, i: (b * nblk + i, h)),
        out_shape=jax.ShapeDtypeStruct((bsz * length, width), _BF16),
        compiler_params=_params("parallel", "parallel", "arbitrary"),
        name="moba_prompt",
    )(proj, proj, proj)


def _moba_sample_kernel(pt_ref, q_ref, kn_ref, vn_ref, *refs, npages, heads, lq, group):
    kc_refs, vc_refs = refs[:group], refs[group:2 * group]
    o_ref, qbd_ref, ks_ref, km_ref, m_ref, l_ref, acc_ref = refs[2 * group:]
    p = pl.program_id(1)
    nsteps = npages // group
    hd = kc_refs[0].shape[2]
    page = kc_refs[0].shape[1] // heads
    width = heads * hd
    rows = heads * lq
    blk_pages = MOBA_BLOCK // page
    nb = npages // blk_pages
    scale = hd ** -0.5

    def by_token(ref):
        return jnp.concatenate([ref[0, pl.ds(h, page, stride=heads), :] for h in range(heads)], axis=1)

    @pl.when(p == 0)
    def _():
        qt = jnp.concatenate([q_ref[...]] * heads, axis=0)
        rh = _div(lax.broadcasted_iota(jnp.int32, (rows, width), 0), lq)
        ch = _div(lax.broadcasted_iota(jnp.int32, (rows, width), 1), hd)
        qbd_ref[...] = jnp.where(rh == ch, qt, 0.0).astype(_BF16)

    qbd = qbd_ref[...]
    for g0 in range(0, group, blk_pages):
        blk = (p * group + g0) // blk_pages
        s = [_dot_nt(qbd, by_token(kc_refs[g0 + g]).astype(_BF16)) * scale for g in range(blk_pages)]
        m = s[0].max(axis=1, keepdims=True)
        for sg in s[1:]:
            m = jnp.maximum(m, sg.max(axis=1, keepdims=True))
        l = jnp.zeros((rows, 1), _F32)
        acc = jnp.zeros((rows, width), _F32)
        for g in range(blk_pages):
            w = jnp.exp(s[g] - m)
            l = l + w.sum(axis=1, keepdims=True)
            acc = acc + _dot(w.astype(_BF16), by_token(vc_refs[g0 + g]).astype(_BF16))
            ks_ref[p * group + g0 + g] = kc_refs[g0 + g][0].reshape(page, heads, hd).sum(axis=0)
        m_ref[blk] = m
        l_ref[blk] = l
        acc_ref[blk] = acc

    @pl.when(p == nsteps - 1)
    def _():
        k_sum = ks_ref[...].reshape(nb, blk_pages, heads, hd).sum(axis=1)
        km_ref[...] = k_sum.reshape(nb * heads, hd) * (1.0 / MOBA_BLOCK)
        k_mean = jnp.concatenate([km_ref[pl.ds(h, nb, stride=heads), :] for h in range(heads)], axis=1)
        gate = _dot_nt(qbd, k_mean.astype(_BF16))
        lane = lax.broadcasted_iota(jnp.int32, (rows, nb), 1)
        sel = jnp.zeros((rows, nb), _F32)
        for _ in range(MOBA_TOPK):
            best = jnp.max(gate, axis=1, keepdims=True)
            idx = jnp.min(jnp.where(gate == best, lane, nb), axis=1, keepdims=True)
            hit = lane == idx
            sel = jnp.where(hit, 1.0, sel)
            gate = jnp.where(hit, -jnp.inf, gate)

        pad = 2 * SUBLANES - lq
        kn = jnp.concatenate([kn_ref[...], jnp.zeros((pad, width), _F32)], axis=0).astype(_BF16)
        vn = jnp.concatenate([vn_ref[...], jnp.zeros((pad, width), _F32)], axis=0).astype(_BF16)
        s_own = _dot_nt(qbd, kn) * scale
        qi = _mod(lax.broadcasted_iota(jnp.int32, s_own.shape, 0), lq)
        kj = lax.broadcasted_iota(jnp.int32, s_own.shape, 1)
        s_own = jnp.where((kj <= qi) & (kj < lq), s_own, NEG_INF)

        picked = [sel[:, b:b + 1] > 0.0 for b in range(nb)]
        top = jnp.max(s_own, axis=1, keepdims=True)
        for b in range(nb):
            top = jnp.maximum(top, jnp.where(picked[b], m_ref[b], NEG_INF))
        p_own = jnp.exp(s_own - top)
        l = jnp.sum(p_own, axis=1, keepdims=True)
        out = _dot(p_own.astype(_BF16), vn)
        for b in range(nb):
            c = jnp.where(picked[b], jnp.exp(jnp.minimum(m_ref[b] - top, 0.0)), 0.0)
            l = l + c * l_ref[b]
            out = out + c * acc_ref[b]
        out = out / l
        for h in range(heads):
            o_ref[:, h * hd:(h + 1) * hd] = out[h * lq:(h + 1) * lq, h * hd:(h + 1) * hd]


def _moba_sample(proj, cache_k, cache_v, page_table, bsz, lq, q_col, k_col, v_col, width):
    heads = B_HEADS
    hd = width // heads
    n_phys, page = cache_k.shape[0], cache_k.shape[1]
    npages = page_table.shape[1]
    past = npages * page
    assert MOBA_BLOCK % page == 0 and past % MOBA_BLOCK == 0 and lq % SUBLANES == 0 and lq <= 2 * SUBLANES
    assert past // MOBA_BLOCK >= MOBA_TOPK and heads % SUBLANES == 0
    blk_pages = MOBA_BLOCK // page
    nb = past // MOBA_BLOCK
    group = blk_pages * max(g for g in (1, 2, 4) if nb % g == 0)
    nsteps = npages // group
    rows = heads * lq
    kc = cache_k.reshape(n_phys, page * heads, hd)
    vc = cache_v.reshape(n_phys, page * heads, hd)

    def nth_page(g):
        return lambda b, p, pt: (pt[b, p * group + g], 0, 0)

    page_specs = [pl.BlockSpec((1, page * heads, hd), nth_page(g)) for g in range(group)] * 2
    grid_spec = pltpu.PrefetchScalarGridSpec(
        num_scalar_prefetch=1,
        grid=(bsz, nsteps),
        in_specs=[
            pl.BlockSpec((lq, width), lambda b, p, pt: (b, q_col // width)),
            pl.BlockSpec((lq, width), lambda b, p, pt: (b, k_col // width)),
            pl.BlockSpec((lq, width), lambda b, p, pt: (b, v_col // width)),
        ] + page_specs,
        out_specs=pl.BlockSpec((lq, width), lambda b, p, pt: (b, 0)),
        scratch_shapes=[
            pltpu.VMEM((rows, width), _BF16),
            pltpu.VMEM((npages, heads, hd), _F32),
            pltpu.VMEM((nb * heads, hd), _F32),
            pltpu.VMEM((nb, rows, 1), _F32),
            pltpu.VMEM((nb, rows, 1), _F32),
            pltpu.VMEM((nb, rows, width), _F32),
        ],
    )
    return pl.pallas_call(
        functools.partial(_moba_sample_kernel, npages=npages, heads=heads, lq=lq, group=group),
        grid_spec=grid_spec,
        out_shape=jax.ShapeDtypeStruct((bsz * lq, width), _F32),
        compiler_params=_params("parallel", "arbitrary"),
        name="moba_sample",
    )(page_table.astype(jnp.int32), proj, proj, proj, *([kc] * group), *([vc] * group))


def _top_rows(x, k):
    r, t = x.shape
    rowi = lax.broadcasted_iota(jnp.int32, (r, t), 0).astype(_F32)
    slot = lax.broadcasted_iota(jnp.int32, (k, t), 0)
    vals = jnp.zeros((k, t), _F32)
    idxs = jnp.zeros((k, t), _F32)
    for n in range(k):
        best = jnp.max(x, axis=0, keepdims=True)
        idx = jnp.min(jnp.where(x == best, rowi, float(r)), axis=0, keepdims=True)
        vals = jnp.where(slot == n, best, vals)
        idxs = jnp.where(slot == n, idx, idxs)
        x = jnp.where(rowi == idx, -jnp.inf, x)
    return vals, idxs


def _candidate_pieces(topk):
    pieces, start = [], 0
    a = 0
    while topk // (a + 1) > 1:
        nb = topk // (a + 1)
        rows = -(-nb // SUBLANES) * SUBLANES
        pieces.append((start, rows, a, nb))
        start += rows
        a += 1
    return pieces, (start, a)


def _peer_retrieve_kernel(q_ref, sk_ref, g_ref, i1_ref, i2_ref):
    topk = PEER_TOPK
    dh = sk_ref.shape[-1]
    tm = q_ref.shape[0]
    tops = []
    for a in range(2):
        qa = q_ref[:, a * dh:(a + 1) * dh].astype(_BF16)
        tops.append(_top_rows(_dot_nt(sk_ref[0, a], qa), topk))
    (s1, k1), (s2, k2) = tops

    pieces, (tail_start, tail_a) = _candidate_pieces(topk)
    parts = []
    for _, rows, a, nb in pieces:
        part = s1[a:a + 1] + s2[:rows]
        if nb < rows:
            part = jnp.where(lax.broadcasted_iota(jnp.int32, (rows, tm), 0) < nb, part, -jnp.inf)
        parts.append(part)
    tail = s1[tail_a:] + s2[0:1]
    tail_rows = -(-(topk - tail_a) // SUBLANES) * SUBLANES
    if tail_rows > topk - tail_a:
        tail = jnp.concatenate([tail, jnp.full((tail_rows - (topk - tail_a), tm), -jnp.inf, _F32)], axis=0)
    cand = jnp.concatenate(parts + [tail], axis=0)
    best_s, pos = _top_rows(cand, topk)

    a_sel = pos - float(tail_start - tail_a)
    b_sel = jnp.zeros_like(pos)
    for start, rows, a, _ in pieces:
        inside = (pos >= float(start)) & (pos < float(start + rows))
        a_sel = jnp.where(inside, float(a), a_sel)
        b_sel = jnp.where(inside, pos - float(start), b_sel)
    i1 = jnp.zeros_like(pos)
    i2 = jnp.zeros_like(pos)
    for a in range(topk):
        i1 = jnp.where(a_sel == float(a), k1[a:a + 1], i1)
        i2 = jnp.where(b_sel == float(a), k2[a:a + 1], i2)
    e = jnp.exp(best_s - best_s[0:1])
    g_ref[...] = e / jnp.sum(e, axis=0, keepdims=True)
    i1_ref[...] = i1
    i2_ref[...] = i2


def _peer_retrieve(q, sub_keys, tm=512):
    t = q.shape[0]
    heads, _, nkeys, dh = sub_keys.shape
    tm = min(tm, t)
    spec = pl.BlockSpec((PEER_TOPK, tm), lambda i, h: (h, i))
    shape = jax.ShapeDtypeStruct((heads * PEER_TOPK, t), _F32)
    return pl.pallas_call(
        _peer_retrieve_kernel,
        grid=(t // tm, heads),
        in_specs=[
            pl.BlockSpec((tm, 2 * dh), lambda i, h: (i, h)),
            pl.BlockSpec((1, 2, nkeys, dh), lambda i, h: (h, 0, 0, 0)),
        ],
        out_specs=[spec, spec, spec],
        out_shape=[shape, shape, shape],
        compiler_params=_params("parallel", "parallel"),
        name="peer_retrieve",
    )(q, sub_keys.astype(_BF16))


def _peer_weights_kernel(g_ref, i1_ref, i2_ref, w_ref, gt_ref, i1t_ref, i2t_ref, scr_ref, *, tm, stride):
    nk = LANES
    gt_ref[...] = g_ref[...].T
    i1t_ref[...] = i1_ref[...].T
    i2t_ref[...] = i2_ref[...].T
    sub = lax.broadcasted_iota(jnp.int32, (nk, nk), 0).astype(_F32)

    zeros = jnp.zeros((nk, nk), _F32)

    def one_hots(n):
        r1 = jnp.broadcast_to(i1t_ref[pl.ds(n, 1), :], (nk, nk))
        r2 = jnp.broadcast_to(i2t_ref[pl.ds(n, 1), :], (nk, nk))
        rg = jnp.broadcast_to(gt_ref[pl.ds(n, 1), :], (nk, nk))
        return jnp.where(sub == r1, 1.0, 0.0), jnp.where(sub == r2, rg, 0.0)

    def body(t, carry):
        n = 2 * t
        a1, a2 = one_hots(n)
        b1, b2 = one_hots(n + 1)
        m1 = jnp.concatenate([a1, b1], axis=1).astype(_BF16)
        m2 = jnp.concatenate([jnp.concatenate([a2, zeros], axis=1),
                              jnp.concatenate([zeros, b2], axis=1)], axis=0).astype(_BF16)
        planes = _dot_nt(m1, m2)
        scr_ref[pl.ds(n, nk, stride=stride), :] = planes[:, :nk]
        scr_ref[pl.ds(n + 1, nk, stride=stride), :] = planes[:, nk:]
        return carry

    lax.fori_loop(0, tm // 2, body, 0, unroll=4 * SUBLANES)
    for j in range(nk):
        w_ref[:, j * nk:(j + 1) * nk] = scr_ref[j * stride:j * stride + tm, :].astype(w_ref.dtype)


def _peer_weights(g, i1, i2, nkeys, tm=256):
    picks, t = g.shape
    assert picks == LANES and nkeys == LANES
    tm = min(tm, t)
    stride = tm + SUBLANES
    spec = pl.BlockSpec((picks, tm), lambda i: (0, i))
    return pl.pallas_call(
        functools.partial(_peer_weights_kernel, tm=tm, stride=stride),
        grid=(t // tm,),
        in_specs=[spec, spec, spec],
        out_specs=pl.BlockSpec((tm, nkeys * nkeys), lambda i: (i, 0)),
        out_shape=jax.ShapeDtypeStruct((t, nkeys * nkeys), _BF16),
        scratch_shapes=[
            pltpu.VMEM((tm, picks), _F32),
            pltpu.VMEM((tm, picks), _F32),
            pltpu.VMEM((tm, picks), _F32),
            pltpu.VMEM((nkeys * stride, nkeys), _F32),
        ],
        compiler_params=_params("parallel"),
        name="peer_weights",
    )(g, i1, i2)


def _peer_ffn_kernel(xn_ref, w_ref, u_ref, v_ref, o_ref, *, tn):
    j = pl.program_id(1)

    @pl.when(j == 0)
    def _():
        o_ref[...] = jnp.zeros_like(o_ref)

    xn = xn_ref[...]
    parts = []
    for e in range(0, u_ref.shape[0], tn):
        h = _dot_nt(xn, u_ref[e:e + tn, :])
        w = w_ref[:, e:e + tn].astype(_F32)
        parts.append(jnp.where(w != 0.0, w * jax.nn.gelu(h), 0.0).astype(_BF16))
    coef = jnp.concatenate(parts, axis=1)
    for n in range(0, o_ref.shape[1], tn):
        o_ref[:, n:n + tn] += _dot(coef, v_ref[:, n:n + tn])


def _peer_ffn(xn, w, u, v, tm=1024, te=1024):
    t, d = xn.shape
    n_exp = u.shape[0]
    tm = min(tm, t)
    tn = _tile_n(d)
    blocks = 2 * (2 * (tm * d + tm * te + 2 * te * d) + 4 * tm * d)
    temps = tm * te * 2 + 4 * tm * tn * 4
    return pl.pallas_call(
        functools.partial(_peer_ffn_kernel, tn=tn),
        grid=(t // tm, n_exp // te),
        in_specs=[
            pl.BlockSpec((tm, d), lambda i, j: (i, 0)),
            pl.BlockSpec((tm, te), lambda i, j: (i, j)),
            pl.BlockSpec((te, d), lambda i, j: (j, 0)),
            pl.BlockSpec((te, d), lambda i, j: (j, 0)),
        ],
        out_specs=pl.BlockSpec((tm, d), lambda i, j: (i, 0)),
        out_shape=jax.ShapeDtypeStruct((t, d), _F32),
        compiler_params=_params("parallel", "arbitrary", vmem=max(VMEM_LIMIT, blocks + temps)),
        name="peer_ffn",
    )(xn, w, u, v)


def _residual_norm_kernel(x_ref, y_ref, g_ref, o_ref, *, final):
    x = x_ref[...] + y_ref[...]
    if final:
        x = x * lax.rsqrt(jnp.mean(x * x, axis=-1, keepdims=True) + RMS_EPS) * g_ref[...]
    o_ref[...] = x


def _residual_norm(x, y, gain, final):
    t, d = x.shape
    tm = min(t, 512)
    spec = pl.BlockSpec((tm, d), lambda i: (i, 0))
    return pl.pallas_call(
        functools.partial(_residual_norm_kernel, final=final),
        grid=(t // tm,),
        in_specs=[spec, spec, pl.BlockSpec((1, d), lambda i: (0, 0))],
        out_specs=spec,
        out_shape=jax.ShapeDtypeStruct((t, d), _F32),
        compiler_params=_params("parallel"),
        name="residual_norm",
    )(x, y, gain.reshape(1, d).astype(_F32))


def _layer(x, bsz, length, s0, past, lw, layer, final):
    d_model = x.shape[1]
    a_width = lw["gnorm"].shape[-1]
    b_width = lw["w_b"].shape[0]
    hn = _rmsnorm(x, lw["norm_mix"], _BF16)
    proj = _matmul(hn, lw["w_in"])
    b_q = 4 * a_width
    b_k, b_v = b_q + b_width, b_q + 2 * b_width
    g_a = b_q + 3 * b_width
    g_b = g_a + d_model

    o_a, s_new = _hgrn(proj, lw["lb_logits"], lw["gnorm"], s0, layer, bsz, length, a_width)
    if past is None:
        o_b = _moba_prompt(proj, bsz, length, b_q, b_k, b_v, b_width)
    else:
        o_b = _moba_sample(proj, past[0], past[1], past[2], bsz, length, b_q, b_k, b_v, b_width)
    merged = _merge(o_a, o_b, lw["w_a"], lw["w_b"], proj, g_a, g_b, d_model)
    x = _matmul_residual(merged, lw["w_out"], x)

    xn = _rmsnorm(x, lw["norm_ffn"], _BF16)
    q = _matmul(xn, lw["w_query"])
    gates, i1, i2 = _peer_retrieve(q, lw["sub_keys"])
    w = _peer_weights(gates, i1, i2, lw["sub_keys"].shape[2])
    x = _residual_norm(x, _peer_ffn(xn, w, lw["peer_u"], lw["peer_v"]), lw["norm_final"], final)

    k_new = proj[:, b_k:b_k + b_width].reshape(bsz, length, B_HEADS, b_width // B_HEADS)
    v_new = proj[:, b_v:b_v + b_width].reshape(bsz, length, B_HEADS, b_width // B_HEADS)
    return x, s_new, k_new, v_new


def kernel(x_prompt, x_sample, cache_k, cache_v, state_hgrn, page_table, norm_mix, w_in, hgrn_lb_logits,
           hgrn_gnorm, w_a_proj, w_b_proj, w_out, norm_ffn, peer_w_query, peer_sub_keys, peer_u, peer_v,
           norm_final):
    depth = w_in.shape[0]
    bp, lp, d_model = x_prompt.shape
    bs, ls, _ = x_sample.shape
    x_p = x_prompt.reshape(bp * lp, d_model)
    x_s = x_sample.reshape(bs * ls, d_model)
    outs = [[] for _ in range(6)]
    for layer in range(depth):
        lw = {
            "norm_mix": norm_mix[layer], "w_in": w_in[layer].astype(_BF16), "lb_logits": hgrn_lb_logits,
            "gnorm": hgrn_gnorm[layer], "w_a": w_a_proj[layer].astype(_BF16), "w_b": w_b_proj[layer].astype(_BF16),
            "w_out": w_out[layer].astype(_BF16), "norm_ffn": norm_ffn[layer],
            "w_query": peer_w_query[layer].astype(_BF16), "sub_keys": peer_sub_keys[layer],
            "peer_u": peer_u[layer].astype(_BF16), "peer_v": peer_v[layer].astype(_BF16), "norm_final": norm_final,
        }
        final = layer == depth - 1
        s0_p = jnp.zeros((bp,) + state_hgrn.shape[2:], _F32)
        x_p, s_p, k_p, v_p = _layer(x_p, bp, lp, s0_p, None, lw, layer, final)
        past = (cache_k[layer], cache_v[layer], page_table)
        x_s, s_s, k_s, v_s = _layer(x_s, bs, ls, state_hgrn[layer], past, lw, layer, final)
        for lst, val in zip(outs, (s_p, s_s, k_p, v_p, k_s, v_s)):
            lst.append(val)
    sp, ss, kp, vp, ks, vs = (jnp.stack(lst, axis=0) for lst in outs)
    return (x_p.reshape(bp, lp, d_model), x_s.reshape(bs, ls, d_model),
            sp.astype(state_hgrn.dtype), ss.astype(state_hgrn.dtype),
            kp.astype(cache_k.dtype), vp.astype(cache_v.dtype), ks.astype(cache_k.dtype), vs.astype(cache_v.dtype))
```

```python
import functools

import jax
import jax.numpy as jnp
from jax import lax
from jax.experimental import pallas as pl
from jax.experimental.pallas import tpu as pltpu

A_HEADS = 8
B_HEADS = 8
MOBA_BLOCK = 256
MOBA_TOPK = 3
PEER_HEADS = 8
PEER_TOPK = 16
RMS_EPS = 1e-6
NEG_INF = -1e30

LANES = 128
SUBLANES = 8
HGRN_CHUNK = 128
HGRN_SUB = 16
HGRN_SAFE_SPAN = 40.0
VMEM_LIMIT = 48 * 1024 * 1024

_F32 = jnp.float32
_BF16 = jnp.bfloat16
_NT = (((1,), (1,)), ((), ()))


def _params(*sem, vmem=VMEM_LIMIT):
    return pltpu.CompilerParams(dimension_semantics=sem, vmem_limit_bytes=vmem)


def _dot_nt(a, b):
    return lax.dot_general(a, b, _NT, preferred_element_type=_F32)


def _dot(a, b):
    return jnp.dot(a, b, preferred_element_type=_F32)


def _log2(n):
    assert n > 0 and n & (n - 1) == 0, n
    return n.bit_length() - 1


def _div(x, n):
    return lax.shift_right_logical(x, jnp.int32(_log2(n)))


def _mod(x, n):
    return lax.bitwise_and(x, jnp.int32((1 << _log2(n)) - 1))


def _tile_n(n, *cols, cap=512):
    for tn in (1024, 512, 256, LANES):
        if tn <= cap and all(v % tn == 0 for v in (n,) + cols):
            return tn
    raise ValueError((n, cols))


def _norm_matmul_kernel(x_ref, g_ref, b_ref, o_ref, *refs):
    xn_ref = refs[-1]

    @pl.when(pl.program_id(1) == 0)
    def _():
        x = x_ref[...]
        y = (x * lax.rsqrt(jnp.mean(x * x, axis=-1, keepdims=True) + RMS_EPS) * g_ref[...]).astype(_BF16)
        xn_ref[...] = y
        for n_ref in refs[:-1]:
            n_ref[...] = y

    o_ref[...] = _dot(xn_ref[...], b_ref[...])


def _norm_matmul(x, gain, b, emit_norm=False):
    m, k = x.shape
    n = b.shape[1]
    tm = min(m, 1024)
    tn = _tile_n(n, cap=512 if emit_norm else 1024)
    out_specs = [pl.BlockSpec((tm, tn), lambda i, j: (i, j))]
    out_shape = [jax.ShapeDtypeStruct((m, n), _F32)]
    if emit_norm:
        out_specs.append(pl.BlockSpec((tm, k), lambda i, j: (i, 0)))
        out_shape.append(jax.ShapeDtypeStruct((m, k), _BF16))
    outs = pl.pallas_call(
        _norm_matmul_kernel,
        grid=(m // tm, n // tn),
        in_specs=[
            pl.BlockSpec((tm, k), lambda i, j: (i, 0)),
            pl.BlockSpec((1, k), lambda i, j: (0, 0)),
            pl.BlockSpec((k, tn), lambda i, j: (0, j)),
        ],
        out_specs=out_specs,
        out_shape=out_shape,
        scratch_shapes=[pltpu.VMEM((tm, k), _BF16)],
        compiler_params=_params("parallel", "arbitrary"),
        name="norm_matmul",
    )(x, gain.reshape(1, k).astype(_F32), b)
    return outs if emit_norm else outs[0]


def _merge_kernel(oa_ref, ob_ref, wa_ref, wb_ref, ga_ref, gb_ref, o_ref):
    ya = _dot(oa_ref[...].astype(_BF16), wa_ref[...])
    yb = _dot(ob_ref[...].astype(_BF16), wb_ref[...])
    o_ref[...] = (jax.nn.sigmoid(ga_ref[...]) * ya + jax.nn.sigmoid(gb_ref[...]) * yb).astype(o_ref.dtype)


def _merge(oa, ob, wa, wb, proj, ga_col, gb_col, d_model):
    t, ka = oa.shape
    kb = ob.shape[1]
    tm = min(t, 1024)
    tn = _tile_n(d_model, ga_col, gb_col)
    return pl.pallas_call(
        _merge_kernel,
        grid=(t // tm, d_model // tn),
        in_specs=[
            pl.BlockSpec((tm, ka), lambda i, j: (i, 0)),
            pl.BlockSpec((tm, kb), lambda i, j: (i, 0)),
            pl.BlockSpec((ka, tn), lambda i, j: (0, j)),
            pl.BlockSpec((kb, tn), lambda i, j: (0, j)),
            pl.BlockSpec((tm, tn), lambda i, j: (i, ga_col // tn + j)),
            pl.BlockSpec((tm, tn), lambda i, j: (i, gb_col // tn + j)),
        ],
        out_specs=pl.BlockSpec((tm, tn), lambda i, j: (i, j)),
        out_shape=jax.ShapeDtypeStruct((t, d_model), _BF16),
        compiler_params=_params("parallel", "parallel"),
        name="merge",
    )(oa, ob, wa, wb, proj, proj)


def _mm_residual_kernel(a_ref, b_ref, r_ref, o_ref):
    o_ref[...] = r_ref[...] + _dot(a_ref[...], b_ref[...])


def _matmul_residual(a, b, res):
    m, k = a.shape
    n = b.shape[1]
    tm = min(m, 1024)
    tn = _tile_n(n)
    return pl.pallas_call(
        _mm_residual_kernel,
        grid=(m // tm, n // tn),
        in_specs=[
            pl.BlockSpec((tm, k), lambda i, j: (i, 0)),
            pl.BlockSpec((k, tn), lambda i, j: (0, j)),
            pl.BlockSpec((tm, tn), lambda i, j: (i, j)),
        ],
        out_specs=pl.BlockSpec((tm, tn), lambda i, j: (i, j)),
        out_shape=jax.ShapeDtypeStruct((m, n), _F32),
        compiler_params=_params("parallel", "parallel"),
        name="matmul_residual",
    )(a, b, res)


def _hgrn_inputs(q, fpre, v, logits, *, layer, rows):
    c_len = HGRN_CHUNK

    def pad(x):
        if rows == c_len:
            return x
        return jnp.concatenate([x, jnp.zeros((c_len - rows, x.shape[1]), x.dtype)], axis=0)

    ex = jnp.exp(logits - jnp.max(logits, axis=0, keepdims=True))
    lb = jnp.sum(ex[: layer + 1], axis=0, keepdims=True) / jnp.sum(ex, axis=0, keepdims=True)

    forget = lb + (1.0 - lb) * jax.nn.sigmoid(fpre)
    b = pad(jnp.log(forget))
    row = lax.broadcasted_iota(jnp.int32, (c_len, LANES), 0)
    shift = 1
    while shift < c_len:
        b = b + jnp.where(row >= shift, pltpu.roll(b, shift, 0), 0.0)
        shift *= 2
    return pad(q), pad(1.0 - forget), pad(v), b


def _hgrn_start(b, i):
    return b[i * HGRN_SUB - 1:i * HGRN_SUB] if i > 0 else jnp.zeros((1, LANES), _F32)


def _hgrn_att_pairwise(q, kk, b, n_sub):
    c_len, sub = HGRN_CHUNK, HGRN_SUB
    rowc = lax.broadcasted_iota(jnp.int32, (sub, LANES), 0)
    lane = lax.broadcasted_iota(jnp.int32, (sub, LANES), 1)
    att_rows = []
    for i in range(n_sub):
        lo = i * sub
        qi, bi, ki = q[lo:lo + sub], b[lo:lo + sub], kk[lo:lo + sub]
        att = jnp.zeros((sub, LANES), _F32)
        for s in range(sub):
            dec = jnp.exp(jnp.minimum(bi - bi[s:s + 1], 0.0))
            x = jnp.where(rowc >= s, qi * dec * ki[s:s + 1], 0.0)
            att = jnp.where(lane == lo + s, jnp.sum(x, axis=-1, keepdims=True), att)
        if i > 0:
            ref_b = _hgrn_start(b, i)
            qt = (qi * jnp.exp(bi - ref_b)).astype(_BF16)
            kp = kk[:lo] * jnp.exp(ref_b - b[:lo])
            kp = jnp.concatenate([kp, jnp.zeros((c_len - lo, LANES), _F32)], axis=0).astype(_BF16)
            att = att + _dot_nt(qt, kp)
        att_rows.append(att)
    if n_sub * sub < c_len:
        att_rows.append(jnp.zeros((c_len - n_sub * sub, LANES), _F32))
    return jnp.concatenate(att_rows, axis=0)


def _hgrn_att_factored(q, kk, b, n_sub):
    c_len, sub = HGRN_CHUNK, HGRN_SUB
    rowc = lax.broadcasted_iota(jnp.int32, (sub, LANES), 0)
    lane = lax.broadcasted_iota(jnp.int32, (sub, LANES), 1)
    att_rows = []
    for i in range(n_sub):
        lo, hi = i * sub, (i + 1) * sub
        ref_b = _hgrn_start(b, i)
        qt = (q[lo:hi] * jnp.exp(b[lo:hi] - ref_b)).astype(_BF16)
        kp = kk[:hi] * jnp.exp(ref_b - b[:hi])
        if hi < c_len:
            kp = jnp.concatenate([kp, jnp.zeros((c_len - hi, LANES), _F32)], axis=0)
        att_rows.append(jnp.where(lane <= rowc + lo, _dot_nt(qt, kp.astype(_BF16)), 0.0))
    if n_sub * sub < c_len:
        att_rows.append(jnp.zeros((c_len - n_sub * sub, LANES), _F32))
    return jnp.concatenate(att_rows, axis=0)


def _hgrn_outputs(q, kk, v, b, att, st, ag, gn, rows):
    c_len = HGRN_CHUNK
    o = _dot_nt((q * jnp.exp(b)).astype(_BF16), st.astype(_BF16))
    o = o + _dot(att.astype(_BF16), v.astype(_BF16))
    b_last = b[c_len - 1:c_len]
    k_hat = (kk * jnp.exp(b_last - b)).astype(_BF16)
    st_new = st * jnp.exp(b_last) + _dot(v.T.astype(_BF16), k_hat)
    oo = o[:rows]
    on = oo * lax.rsqrt(jnp.mean(oo * oo, axis=-1, keepdims=True) + RMS_EPS) * gn
    return on * (ag * jax.nn.sigmoid(ag)), st_new


def _hgrn_kernel(q_ref, f_ref, i_ref, g_ref, lbl_ref, gn_ref, s0_ref, o_ref, sfin_ref, st_ref, *,
                 layer, rows, hb):
    ci = pl.program_id(2)
    sub = HGRN_SUB
    n_sub = -(-rows // sub)

    @pl.when(ci == 0)
    def _():
        for h in range(hb):
            st_ref[h] = s0_ref[0, h].T

    cols = [slice(h * LANES, (h + 1) * LANES) for h in range(hb)]
    heads = [_hgrn_inputs(q_ref[:, c], f_ref[:, c], i_ref[:, c], lbl_ref[:, c], layer=layer, rows=rows)
             for c in cols]
    spans = [_hgrn_start(b, i) - b[(i + 1) * sub - 1:(i + 1) * sub] for _, _, _, b in heads for i in range(n_sub)]
    worst = jnp.max(jnp.concatenate(spans, axis=0))
    atts = lax.cond(worst < HGRN_SAFE_SPAN,
                    lambda: tuple(_hgrn_att_factored(q, kk, b, n_sub) for q, kk, _, b in heads),
                    lambda: tuple(_hgrn_att_pairwise(q, kk, b, n_sub) for q, kk, _, b in heads))
    for h, c in enumerate(cols):
        q, kk, v, b = heads[h]
        out, st_new = _hgrn_outputs(q, kk, v, b, atts[h], st_ref[h], g_ref[:, c], gn_ref[:, c], rows)
        o_ref[:, c] = out.astype(o_ref.dtype)
        st_ref[h] = st_new

    @pl.when(ci == pl.num_programs(2) - 1)
    def _():
        for h in range(hb):
            sfin_ref[0, h] = st_ref[h].T


def _hgrn(proj, lb_logits, gnorm, s0, layer, bsz, length, width, hb=4):
    heads = A_HEADS
    dk = width // heads
    if length >= HGRN_CHUNK:
        assert length % HGRN_CHUNK == 0
        rows, nc = HGRN_CHUNK, length // HGRN_CHUNK
    else:
        assert length % SUBLANES == 0
        rows, nc = length, 1
        hb = heads
    hb = min(hb, heads)
    assert dk == LANES and heads % hb == 0
    sec = heads // hb
    wb = hb * dk

    def col(section):
        return lambda b, h, c: (b * nc + c, section * sec + h)

    return pl.pallas_call(
        functools.partial(_hgrn_kernel, layer=layer, rows=rows, hb=hb),
        grid=(bsz, heads // hb, nc),
        in_specs=[
            pl.BlockSpec((rows, wb), col(0)),
            pl.BlockSpec((rows, wb), col(1)),
            pl.BlockSpec((rows, wb), col(2)),
            pl.BlockSpec((rows, wb), col(3)),
            pl.BlockSpec((lb_logits.shape[0], wb), lambda b, h, c: (0, h)),
            pl.BlockSpec((1, wb), lambda b, h, c: (0, h)),
            pl.BlockSpec((1, hb, dk, dk), lambda b, h, c: (b, h, 0, 0)),
        ],
        out_specs=[
            pl.BlockSpec((rows, wb), lambda b, h, c: (b * nc + c, h)),
            pl.BlockSpec((1, hb, dk, dk), lambda b, h, c: (b, h, 0, 0)),
        ],
        out_shape=[
            jax.ShapeDtypeStruct((bsz * length, width), _BF16 if rows % (2 * SUBLANES) == 0 else _F32),
            jax.ShapeDtypeStruct((bsz, heads, dk, dk), _F32),
        ],
        scratch_shapes=[pltpu.VMEM((hb, dk, dk), _F32)],
        compiler_params=_params("parallel", "parallel", "arbitrary"),
        name="hgrn2",
    )(proj, proj, proj, proj, lb_logits.astype(_F32), gnorm.reshape(1, width).astype(_F32), s0.astype(_F32))


def _moba_prompt_block(q_ref, k_ref, v_ref, o_ref, *, own, nblk):
    blk = MOBA_BLOCK
    hd = q_ref.shape[-1]
    n_sel = min(MOBA_TOPK, nblk, own)
    q = q_ref[...].astype(_BF16)
    k = k_ref[:(own + 1) * blk, :]
    s = _dot_nt(q, k.astype(_BF16)) * (hd ** -0.5)

    sel = [None] * own
    if own > n_sel:
        k_mean = jnp.mean(k[:own * blk].reshape(own, blk, hd), axis=1)
        k_mean = jnp.concatenate([k_mean, jnp.zeros((LANES - own, hd), _F32)], axis=0)
        gate = _dot_nt(k_mean.astype(_BF16), q)
        nrow = -(-own // SUBLANES) * SUBLANES
        blk_id = lax.broadcasted_iota(jnp.int32, (nrow, blk), 0)
        g = jnp.where(blk_id < own, gate[:nrow], NEG_INF)
        rank = jnp.zeros((nrow, blk), jnp.int32)
        for jp in range(own):
            other = g[jp:jp + 1]
            ahead = (other > g) | ((other == g) & (blk_id > jp))
            rank = rank + ahead.astype(jnp.int32)
        picked = jnp.where((rank < n_sel) & (blk_id < own), 1.0, 0.0)
        picked = jnp.concatenate([picked, jnp.zeros((LANES - nrow, blk), _F32)], axis=0).T
        sel = [picked[:, j:j + 1] > 0.0 for j in range(own)]

    row = lax.broadcasted_iota(jnp.int32, (blk, blk), 0)
    col = lax.broadcasted_iota(jnp.int32, (blk, blk), 1)
    masked = []
    for j in range(own + 1):
        sj = s[:, j * blk:(j + 1) * blk]
        if j == own:
            sj = jnp.where(col <= row, sj, NEG_INF)
        elif sel[j] is not None:
            sj = jnp.where(sel[j], sj, NEG_INF)
        masked.append(sj)
    m = masked[0].max(axis=-1, keepdims=True)
    for sj in masked[1:]:
        m = jnp.maximum(m, sj.max(axis=-1, keepdims=True))
    l = jnp.zeros((blk, 1), _F32)
    acc = jnp.zeros((blk, hd), _F32)
    for j, sj in enumerate(masked):
        p = jnp.exp(sj - m)
        l = l + p.sum(axis=-1, keepdims=True)
        acc = acc + _dot(p.astype(_BF16), v_ref[j * blk:(j + 1) * blk, :].astype(_BF16))
    o_ref[...] = (acc / l).astype(o_ref.dtype)


def _moba_prompt_kernel(q_ref, k_ref, v_ref, o_ref, kout_ref, vout_ref, *, nblk):
    i = pl.program_id(2)

    @pl.when(i == 0)
    def _():
        kout_ref[...] = k_ref[...]
        vout_ref[...] = v_ref[...]

    for own in range(nblk):
        pl.when(i == own)(functools.partial(_moba_prompt_block, q_ref, k_ref, v_ref, o_ref, own=own, nblk=nblk))


def _moba_prompt(proj, bsz, length, q_col, k_col, v_col, width):
    heads = B_HEADS
    hd = width // heads
    assert hd == LANES and length % MOBA_BLOCK == 0
    nblk = length // MOBA_BLOCK
    assert nblk <= LANES
    kv_spec = pl.BlockSpec((length, hd), lambda b, h, i: (b, h))
    kv_shape = jax.ShapeDtypeStruct((bsz * length, width), _F32)
    return pl.pallas_call(
        functools.partial(_moba_prompt_kernel, nblk=nblk),
        grid=(bsz, heads, nblk),
        in_specs=[
            pl.BlockSpec((MOBA_BLOCK, hd), lambda b, h, i: (b * nblk + i, q_col // hd + h)),
            pl.BlockSpec((length, hd), lambda b, h, i: (b, k_col // hd + h)),
            pl.BlockSpec((length, hd), lambda b, h, i: (b, v_col // hd + h)),
        ],
        out_specs=[pl.BlockSpec((MOBA_BLOCK, hd), lambda b, h, i: (b * nblk + i, h)), kv_spec, kv_spec],
        out_shape=[jax.ShapeDtypeStruct((bsz * length, width), _BF16), kv_shape, kv_shape],
        compiler_params=_params("parallel", "parallel", "arbitrary"),
        name="moba_prompt",
    )(proj, proj, proj)


def _moba_sample_kernel(pt_ref, q_ref, kn_ref, vn_ref, *refs, npages, heads, lq, group):
    kc_refs, vc_refs = refs[:group], refs[group:2 * group]
    o_ref, qbd_ref, ks_ref, km_ref, m_ref, l_ref, acc_ref = refs[2 * group:]
    p = pl.program_id(1)
    nsteps = npages // group
    hd = kc_refs[0].shape[2]
    page = kc_refs[0].shape[1] // heads
    width = heads * hd
    rows = heads * lq
    blk_pages = MOBA_BLOCK // page
    nb = npages // blk_pages
    scale = hd ** -0.5

    def by_token(ref):
        return jnp.concatenate([ref[0, pl.ds(h, page, stride=heads), :] for h in range(heads)], axis=1)

    @pl.when(p == 0)
    def _():
        qt = jnp.concatenate([q_ref[...]] * heads, axis=0)
        rh = _div(lax.broadcasted_iota(jnp.int32, (rows, width), 0), lq)
        ch = _div(lax.broadcasted_iota(jnp.int32, (rows, width), 1), hd)
        qbd_ref[...] = jnp.where(rh == ch, qt, 0.0).astype(_BF16)

    qbd = qbd_ref[...]
    for g0 in range(0, group, blk_pages):
        blk = (p * group + g0) // blk_pages
        s = [_dot_nt(qbd, by_token(kc_refs[g0 + g]).astype(_BF16)) * scale for g in range(blk_pages)]
        m = s[0].max(axis=1, keepdims=True)
        for sg in s[1:]:
            m = jnp.maximum(m, sg.max(axis=1, keepdims=True))
        l = jnp.zeros((rows, 1), _F32)
        acc = jnp.zeros((rows, width), _F32)
        for g in range(blk_pages):
            w = jnp.exp(s[g] - m)
            l = l + w.sum(axis=1, keepdims=True)
            acc = acc + _dot(w.astype(_BF16), by_token(vc_refs[g0 + g]).astype(_BF16))
            ks_ref[p * group + g0 + g] = kc_refs[g0 + g][0].reshape(page, heads, hd).sum(axis=0)
        m_ref[blk] = m
        l_ref[blk] = l
        acc_ref[blk] = acc

    @pl.when(p == nsteps - 1)
    def _():
        k_sum = ks_ref[...].reshape(nb, blk_pages, heads, hd).sum(axis=1)
        km_ref[...] = k_sum.reshape(nb * heads, hd) * (1.0 / MOBA_BLOCK)
        k_mean = jnp.concatenate([km_ref[pl.ds(h, nb, stride=heads), :] for h in range(heads)], axis=1)
        gate = _dot_nt(qbd, k_mean.astype(_BF16))
        lane = lax.broadcasted_iota(jnp.int32, (rows, nb), 1)
        sel = jnp.zeros((rows, nb), _F32)
        for _ in range(MOBA_TOPK):
            best = jnp.max(gate, axis=1, keepdims=True)
            idx = jnp.min(jnp.where(gate == best, lane, nb), axis=1, keepdims=True)
            hit = lane == idx
            sel = jnp.where(hit, 1.0, sel)
            gate = jnp.where(hit, -jnp.inf, gate)

        pad = 2 * SUBLANES - lq
        kn = jnp.concatenate([kn_ref[...], jnp.zeros((pad, width), _F32)], axis=0).astype(_BF16)
        vn = jnp.concatenate([vn_ref[...], jnp.zeros((pad, width), _F32)], axis=0).astype(_BF16)
        s_own = _dot_nt(qbd, kn) * scale
        qi = _mod(lax.broadcasted_iota(jnp.int32, s_own.shape, 0), lq)
        kj = lax.broadcasted_iota(jnp.int32, s_own.shape, 1)
        s_own = jnp.where((kj <= qi) & (kj < lq), s_own, NEG_INF)

        picked = [sel[:, b:b + 1] > 0.0 for b in range(nb)]
        top = jnp.max(s_own, axis=1, keepdims=True)
        for b in range(nb):
            top = jnp.maximum(top, jnp.where(picked[b], m_ref[b], NEG_INF))
        p_own = jnp.exp(s_own - top)
        l = jnp.sum(p_own, axis=1, keepdims=True)
        out = _dot(p_own.astype(_BF16), vn)
        for b in range(nb):
            c = jnp.where(picked[b], jnp.exp(jnp.minimum(m_ref[b] - top, 0.0)), 0.0)
            l = l + c * l_ref[b]
            out = out + c * acc_ref[b]
        out = out / l
        for h in range(heads):
            o_ref[:, h * hd:(h + 1) * hd] = out[h * lq:(h + 1) * lq, h * hd:(h + 1) * hd]


def _moba_sample(proj, cache_k, cache_v, page_table, bsz, lq, q_col, k_col, v_col, width):
    heads = B_HEADS
    hd = width // heads
    n_phys, page = cache_k.shape[0], cache_k.shape[1]
    npages = page_table.shape[1]
    past = npages * page
    assert MOBA_BLOCK % page == 0 and past % MOBA_BLOCK == 0 and lq % SUBLANES == 0 and lq <= 2 * SUBLANES
    assert past // MOBA_BLOCK >= MOBA_TOPK and heads % SUBLANES == 0
    blk_pages = MOBA_BLOCK // page
    nb = past // MOBA_BLOCK
    group = blk_pages * max(g for g in (1, 2, 4) if nb % g == 0)
    nsteps = npages // group
    rows = heads * lq
    kc = cache_k.reshape(n_phys, page * heads, hd)
    vc = cache_v.reshape(n_phys, page * heads, hd)

    def nth_page(g):
        return lambda b, p, pt: (pt[b, p * group + g], 0, 0)

    page_specs = [pl.BlockSpec((1, page * heads, hd), nth_page(g)) for g in range(group)] * 2
    grid_spec = pltpu.PrefetchScalarGridSpec(
        num_scalar_prefetch=1,
        grid=(bsz, nsteps),
        in_specs=[
            pl.BlockSpec((lq, width), lambda b, p, pt: (b, q_col // width)),
            pl.BlockSpec((lq, width), lambda b, p, pt: (b, k_col // width)),
            pl.BlockSpec((lq, width), lambda b, p, pt: (b, v_col // width)),
        ] + page_specs,
        out_specs=pl.BlockSpec((lq, width), lambda b, p, pt: (b, 0)),
        scratch_shapes=[
            pltpu.VMEM((rows, width), _BF16),
            pltpu.VMEM((npages, heads, hd), _F32),
            pltpu.VMEM((nb * heads, hd), _F32),
            pltpu.VMEM((nb, rows, 1), _F32),
            pltpu.VMEM((nb, rows, 1), _F32),
            pltpu.VMEM((nb, rows, width), _F32),
        ],
    )
    return pl.pallas_call(
        functools.partial(_moba_sample_kernel, npages=npages, heads=heads, lq=lq, group=group),
        grid_spec=grid_spec,
        out_shape=jax.ShapeDtypeStruct((bsz * lq, width), _F32),
        compiler_params=_params("parallel", "arbitrary"),
        name="moba_sample",
    )(page_table.astype(jnp.int32), proj, proj, proj, *([kc] * group), *([vc] * group))


def _top_rows(x, k):
    r, t = x.shape
    rowi = lax.broadcasted_iota(jnp.int32, (r, t), 0).astype(_F32)
    slot = lax.broadcasted_iota(jnp.int32, (k, t), 0)
    vals = jnp.zeros((k, t), _F32)
    idxs = jnp.zeros((k, t), _F32)
    for n in range(k):
        best = jnp.max(x, axis=0, keepdims=True)
        idx = jnp.min(jnp.where(x == best, rowi, float(r)), axis=0, keepdims=True)
        vals = jnp.where(slot == n, best, vals)
        idxs = jnp.where(slot == n, idx, idxs)
        x = jnp.where(rowi == idx, -jnp.inf, x)
    return vals, idxs


def _candidate_pieces(topk):
    pieces, start = [], 0
    a = 0
    while topk // (a + 1) > 1:
        nb = topk // (a + 1)
        rows = -(-nb // SUBLANES) * SUBLANES
        pieces.append((start, rows, a, nb))
        start += rows
        a += 1
    return pieces, (start, a)


def _peer_retrieve_kernel(q_ref, sk_ref, *refs):
    if len(refs) > 3:
        u_ref, v_ref, g_ref, i1_ref, i2_ref, ub_ref, vb_ref = refs
        ub_ref[...] = u_ref[...].astype(_BF16)
        vb_ref[...] = v_ref[...].astype(_BF16)
    else:
        g_ref, i1_ref, i2_ref = refs
    topk = PEER_TOPK
    dh = sk_ref.shape[-1]
    tm = q_ref.shape[0]
    tops = []
    for a in range(2):
        qa = q_ref[:, a * dh:(a + 1) * dh].astype(_BF16)
        tops.append(_top_rows(_dot_nt(sk_ref[0, a], qa), topk))
    (s1, k1), (s2, k2) = tops

    pieces, (tail_start, tail_a) = _candidate_pieces(topk)
    parts = []
    for _, rows, a, nb in pieces:
        part = s1[a:a + 1] + s2[:rows]
        if nb < rows:
            part = jnp.where(lax.broadcasted_iota(jnp.int32, (rows, tm), 0) < nb, part, -jnp.inf)
        parts.append(part)
    tail = s1[tail_a:] + s2[0:1]
    tail_rows = -(-(topk - tail_a) // SUBLANES) * SUBLANES
    if tail_rows > topk - tail_a:
        tail = jnp.concatenate([tail, jnp.full((tail_rows - (topk - tail_a), tm), -jnp.inf, _F32)], axis=0)
    cand = jnp.concatenate(parts + [tail], axis=0)
    best_s, pos = _top_rows(cand, topk)

    a_sel = pos - float(tail_start - tail_a)
    b_sel = jnp.zeros_like(pos)
    for start, rows, a, _ in pieces:
        inside = (pos >= float(start)) & (pos < float(start + rows))
        a_sel = jnp.where(inside, float(a), a_sel)
        b_sel = jnp.where(inside, pos - float(start), b_sel)
    i1 = jnp.zeros_like(pos)
    i2 = jnp.zeros_like(pos)
    for a in range(topk):
        i1 = jnp.where(a_sel == float(a), k1[a:a + 1], i1)
        i2 = jnp.where(b_sel == float(a), k2[a:a + 1], i2)
    e = jnp.exp(best_s - best_s[0:1])
    g_ref[...] = e / jnp.sum(e, axis=0, keepdims=True)
    i1_ref[...] = i1
    i2_ref[...] = i2


def _peer_retrieve(q, sub_keys, tables=None, tm=512):
    t = q.shape[0]
    heads, _, nkeys, dh = sub_keys.shape
    tm = min(tm, t)
    steps = (t // tm) * heads
    spec = pl.BlockSpec((PEER_TOPK, tm), lambda i, h: (h, i))
    shape = jax.ShapeDtypeStruct((heads * PEER_TOPK, t), _F32)
    in_specs = [
        pl.BlockSpec((tm, 2 * dh), lambda i, h: (i, h)),
        pl.BlockSpec((1, 2, nkeys, dh), lambda i, h: (h, 0, 0, 0)),
    ]
    out_specs, out_shape, operands = [spec, spec, spec], [shape, shape, shape], [q, sub_keys.astype(_BF16)]
    if tables is not None:
        n_exp, d = tables[0].shape
        slab = n_exp // steps
        assert slab * steps == n_exp and slab % (2 * SUBLANES) == 0, (n_exp, steps)
        slab_spec = pl.BlockSpec((slab, d), lambda i, h: (i * heads + h, 0))
        in_specs += [slab_spec, slab_spec]
        out_specs += [slab_spec, slab_spec]
        out_shape += [jax.ShapeDtypeStruct((n_exp, d), _BF16)] * 2
        operands += list(tables)
    return pl.pallas_call(
        _peer_retrieve_kernel,
        grid=(t // tm, heads),
        in_specs=in_specs,
        out_specs=out_specs,
        out_shape=out_shape,
        compiler_params=_params("parallel", "parallel"),
        name="peer_retrieve",
    )(*operands)


def _peer_weights_kernel(g_ref, i1_ref, i2_ref, w_ref, gt_ref, i1t_ref, i2t_ref, scr_ref, *, tm, stride):
    nk = LANES
    gt_ref[...] = g_ref[...].T
    i1t_ref[...] = i1_ref[...].T
    i2t_ref[...] = i2_ref[...].T
    sub = lax.broadcasted_iota(jnp.int32, (nk, nk), 0).astype(_F32)

    zeros = jnp.zeros((nk, nk), _F32)

    def one_hots(n):
        r1 = jnp.broadcast_to(i1t_ref[pl.ds(n, 1), :], (nk, nk))
        r2 = jnp.broadcast_to(i2t_ref[pl.ds(n, 1), :], (nk, nk))
        rg = jnp.broadcast_to(gt_ref[pl.ds(n, 1), :], (nk, nk))
        return jnp.where(sub == r1, 1.0, 0.0), jnp.where(sub == r2, rg, 0.0)

    def body(t, carry):
        n = 2 * t
        a1, a2 = one_hots(n)
        b1, b2 = one_hots(n + 1)
        m1 = jnp.concatenate([a1, b1], axis=1).astype(_BF16)
        m2 = jnp.concatenate([jnp.concatenate([a2, zeros], axis=1),
                              jnp.concatenate([zeros, b2], axis=1)], axis=0).astype(_BF16)
        planes = _dot_nt(m1, m2)
        scr_ref[pl.ds(n, nk, stride=stride), :] = planes[:, :nk]
        scr_ref[pl.ds(n + 1, nk, stride=stride), :] = planes[:, nk:]
        return carry

    lax.fori_loop(0, tm // 2, body, 0, unroll=4 * SUBLANES)
    for j in range(nk):
        w_ref[:, j * nk:(j + 1) * nk] = scr_ref[j * stride:j * stride + tm, :].astype(w_ref.dtype)


def _peer_weights(g, i1, i2, nkeys, tm=256):
    picks, t = g.shape
    assert picks == LANES and nkeys == LANES
    tm = min(tm, t)
    stride = tm + SUBLANES
    spec = pl.BlockSpec((picks, tm), lambda i: (0, i))
    return pl.pallas_call(
        functools.partial(_peer_weights_kernel, tm=tm, stride=stride),
        grid=(t // tm,),
        in_specs=[spec, spec, spec],
        out_specs=pl.BlockSpec((tm, nkeys * nkeys), lambda i: (i, 0)),
        out_shape=jax.ShapeDtypeStruct((t, nkeys * nkeys), _BF16),
        scratch_shapes=[
            pltpu.VMEM((tm, picks), _F32),
            pltpu.VMEM((tm, picks), _F32),
            pltpu.VMEM((tm, picks), _F32),
            pltpu.VMEM((nkeys * stride, nkeys), _F32),
        ],
        compiler_params=_params("parallel"),
        name="peer_weights",
    )(g, i1, i2)


def _peer_ffn_kernel(xn_ref, w_ref, u_ref, v_ref, o_ref, *, tn):
    j = pl.program_id(1)

    @pl.when(j == 0)
    def _():
        o_ref[...] = jnp.zeros_like(o_ref)

    xn = xn_ref[...]
    parts = []
    for e in range(0, u_ref.shape[0], tn):
        h = _dot_nt(xn, u_ref[e:e + tn, :])
        w = w_ref[:, e:e + tn].astype(_F32)
        parts.append(jnp.where(w != 0.0, w * jax.nn.gelu(h), 0.0).astype(_BF16))
    coef = jnp.concatenate(parts, axis=1)
    for n in range(0, o_ref.shape[1], tn):
        o_ref[:, n:n + tn] += _dot(coef, v_ref[:, n:n + tn])


def _peer_ffn(xn, w, u, v, tm=1024, te=1024):
    t, d = xn.shape
    n_exp = u.shape[0]
    tm = min(tm, t)
    tn = _tile_n(d)
    blocks = 2 * (2 * (tm * d + tm * te + 2 * te * d) + 4 * tm * d)
    temps = tm * te * 2 + 4 * tm * tn * 4
    return pl.pallas_call(
        functools.partial(_peer_ffn_kernel, tn=tn),
        grid=(t // tm, n_exp // te),
        in_specs=[
            pl.BlockSpec((tm, d), lambda i, j: (i, 0)),
            pl.BlockSpec((tm, te), lambda i, j: (i, j)),
            pl.BlockSpec((te, d), lambda i, j: (j, 0)),
            pl.BlockSpec((te, d), lambda i, j: (j, 0)),
        ],
        out_specs=pl.BlockSpec((tm, d), lambda i, j: (i, 0)),
        out_shape=jax.ShapeDtypeStruct((t, d), _F32),
        compiler_params=_params("parallel", "arbitrary", vmem=max(VMEM_LIMIT, blocks + temps)),
        name="peer_ffn",
    )(xn, w, u, v)


def _residual_norm_kernel(x_ref, y_ref, g_ref, o_ref, *, final):
    x = x_ref[...] + y_ref[...]
    if final:
        x = x * lax.rsqrt(jnp.mean(x * x, axis=-1, keepdims=True) + RMS_EPS) * g_ref[...]
    o_ref[...] = x


def _residual_norm(x, y, gain, final):
    t, d = x.shape
    tm = min(t, 512)
    spec = pl.BlockSpec((tm, d), lambda i: (i, 0))
    return pl.pallas_call(
        functools.partial(_residual_norm_kernel, final=final),
        grid=(t // tm,),
        in_specs=[spec, spec, pl.BlockSpec((1, d), lambda i: (0, 0))],
        out_specs=spec,
        out_shape=jax.ShapeDtypeStruct((t, d), _F32),
        compiler_params=_params("parallel"),
        name="residual_norm",
    )(x, y, gain.reshape(1, d).astype(_F32))


def _layer(x, bsz, length, s0, past, lw, layer, final):
    d_model = x.shape[1]
    a_width = lw["gnorm"].shape[-1]
    b_width = lw["w_b"].shape[0]
    proj = _norm_matmul(x, lw["norm_mix"], lw["w_in"])
    b_q = 4 * a_width
    b_k, b_v = b_q + b_width, b_q + 2 * b_width
    g_a = b_q + 3 * b_width
    g_b = g_a + d_model

    o_a, s_new = _hgrn(proj, lw["lb_logits"], lw["gnorm"], s0, layer, bsz, length, a_width)
    if past is None:
        o_b, k_new, v_new = _moba_prompt(proj, bsz, length, b_q, b_k, b_v, b_width)
    else:
        o_b = _moba_sample(proj, past[0], past[1], past[2], bsz, length, b_q, b_k, b_v, b_width)
        k_new, v_new = proj[:, b_k:b_k + b_width], proj[:, b_v:b_v + b_width]
    merged = _merge(o_a, o_b, lw["w_a"], lw["w_b"], proj, g_a, g_b, d_model)
    x = _matmul_residual(merged, lw["w_out"], x)

    q, xn = _norm_matmul(x, lw["norm_ffn"], lw["w_query"], emit_norm=True)
    if "peer_u_bf16" in lw:
        gates, i1, i2 = _peer_retrieve(q, lw["sub_keys"])
    else:
        gates, i1, i2, lw["peer_u_bf16"], lw["peer_v_bf16"] = _peer_retrieve(
            q, lw["sub_keys"], tables=(lw["peer_u"], lw["peer_v"]))
    w = _peer_weights(gates, i1, i2, lw["sub_keys"].shape[2])
    x = _residual_norm(x, _peer_ffn(xn, w, lw["peer_u_bf16"], lw["peer_v_bf16"]), lw["norm_final"], final)

    kv_shape = (bsz, length, B_HEADS, b_width // B_HEADS)
    return x, s_new, k_new.reshape(kv_shape), v_new.reshape(kv_shape)


def kernel(x_prompt, x_sample, cache_k, cache_v, state_hgrn, page_table, norm_mix, w_in, hgrn_lb_logits,
           hgrn_gnorm, w_a_proj, w_b_proj, w_out, norm_ffn, peer_w_query, peer_sub_keys, peer_u, peer_v,
           norm_final):
    depth = w_in.shape[0]
    bp, lp, d_model = x_prompt.shape
    bs, ls, _ = x_sample.shape
    x_p = x_prompt.reshape(bp * lp, d_model)
    x_s = x_sample.reshape(bs * ls, d_model)
    outs = [[] for _ in range(6)]
    for layer in range(depth):
        lw = {
            "norm_mix": norm_mix[layer], "w_in": w_in[layer].astype(_BF16), "lb_logits": hgrn_lb_logits,
            "gnorm": hgrn_gnorm[layer], "w_a": w_a_proj[layer].astype(_BF16), "w_b": w_b_proj[layer].astype(_BF16),
            "w_out": w_out[layer].astype(_BF16), "norm_ffn": norm_ffn[layer],
            "w_query": peer_w_query[layer].astype(_BF16), "sub_keys": peer_sub_keys[layer],
            "peer_u": peer_u[layer], "peer_v": peer_v[layer], "norm_final": norm_final,
        }
        final = layer == depth - 1
        s0_p = jnp.zeros((bp,) + state_hgrn.shape[2:], _F32)
        x_p, s_p, k_p, v_p = _layer(x_p, bp, lp, s0_p, None, lw, layer, final)
        past = (cache_k[layer], cache_v[layer], page_table)
        x_s, s_s, k_s, v_s = _layer(x_s, bs, ls, state_hgrn[layer], past, lw, layer, final)
        for lst, val in zip(outs, (s_p, s_s, k_p, v_p, k_s, v_s)):
            lst.append(val)
    sp, ss, kp, vp, ks, vs = (jnp.stack(lst, axis=0) for lst in outs)
    return (x_p.reshape(bp, lp, d_model), x_s.reshape(bs, ls, d_model),
            sp.astype(state_hgrn.dtype), ss.astype(state_hgrn.dtype),
            kp.astype(cache_k.dtype), vp.astype(cache_v.dtype), ks.astype(cache_k.dtype), vs.astype(cache_v.dtype))
```

```python
import functools

import jax
import jax.numpy as jnp
from jax import lax
from jax.experimental import pallas as pl
from jax.experimental.pallas import tpu as pltpu

A_HEADS = 8
B_HEADS = 8
MOBA_BLOCK = 256
MOBA_TOPK = 3
PEER_HEADS = 8
PEER_TOPK = 16
RMS_EPS = 1e-6
NEG_INF = -1e30

LANES = 128
SUBLANES = 8
HGRN_CHUNK = 128
HGRN_SUB = 16
HGRN_SAFE_SPAN = 40.0
VMEM_LIMIT = 48 * 1024 * 1024

_F32 = jnp.float32
_BF16 = jnp.bfloat16
_NT = (((1,), (1,)), ((), ()))


def _params(*sem, vmem=VMEM_LIMIT):
    return pltpu.CompilerParams(dimension_semantics=sem, vmem_limit_bytes=vmem)


def _dot_nt(a, b):
    return lax.dot_general(a, b, _NT, preferred_element_type=_F32)


def _dot(a, b):
    return jnp.dot(a, b, preferred_element_type=_F32)


def _log2(n):
    assert n > 0 and n & (n - 1) == 0, n
    return n.bit_length() - 1


def _div(x, n):
    return lax.shift_right_logical(x, jnp.int32(_log2(n)))


def _mod(x, n):
    return lax.bitwise_and(x, jnp.int32((1 << _log2(n)) - 1))


def _tile_n(n, *cols, cap=512):
    for tn in (1024, 512, 256, LANES):
        if tn <= cap and all(v % tn == 0 for v in (n,) + cols):
            return tn
    raise ValueError((n, cols))


def _norm_matmul_kernel(x_ref, g_ref, b_ref, o_ref, *refs):
    xn_ref = refs[-1]

    @pl.when(pl.program_id(1) == 0)
    def _():
        x = x_ref[...]
        y = (x * lax.rsqrt(jnp.mean(x * x, axis=-1, keepdims=True) + RMS_EPS) * g_ref[...]).astype(_BF16)
        xn_ref[...] = y
        for n_ref in refs[:-1]:
            n_ref[...] = y

    o_ref[...] = _dot(xn_ref[...], b_ref[...]).astype(o_ref.dtype)


def _norm_matmul(x, gain, b, emit_norm=False, out_dtype=_F32):
    m, k = x.shape
    n = b.shape[1]
    if 2 * k * n * 2 <= VMEM_LIMIT // 3:
        tm, tn = min(m, 512), n
    else:
        tm, tn = min(m, 1024), _tile_n(n, cap=512 if emit_norm else 1024)
    out_specs = [pl.BlockSpec((tm, tn), lambda i, j: (i, j))]
    out_shape = [jax.ShapeDtypeStruct((m, n), out_dtype)]
    if emit_norm:
        out_specs.append(pl.BlockSpec((tm, k), lambda i, j: (i, 0)))
        out_shape.append(jax.ShapeDtypeStruct((m, k), _BF16))
    outs = pl.pallas_call(
        _norm_matmul_kernel,
        grid=(m // tm, n // tn),
        in_specs=[
            pl.BlockSpec((tm, k), lambda i, j: (i, 0)),
            pl.BlockSpec((1, k), lambda i, j: (0, 0)),
            pl.BlockSpec((k, tn), lambda i, j: (0, j)),
        ],
        out_specs=out_specs,
        out_shape=out_shape,
        scratch_shapes=[pltpu.VMEM((tm, k), _BF16)],
        compiler_params=_params("parallel", "arbitrary"),
        name="norm_matmul",
    )(x, gain.reshape(1, k).astype(_F32), b)
    return outs if emit_norm else outs[0]


def _merge_kernel(oa_ref, ob_ref, wa_ref, wb_ref, ga_ref, gb_ref, o_ref):
    ya = _dot(oa_ref[...].astype(_BF16), wa_ref[...])
    yb = _dot(ob_ref[...].astype(_BF16), wb_ref[...])
    o_ref[...] = (jax.nn.sigmoid(ga_ref[...]) * ya + jax.nn.sigmoid(gb_ref[...]) * yb).astype(o_ref.dtype)


def _merge(oa, ob, wa, wb, proj, ga_col, gb_col, d_model):
    t, ka = oa.shape
    kb = ob.shape[1]
    tm = min(t, 1024)
    tn = _tile_n(d_model, ga_col, gb_col)
    return pl.pallas_call(
        _merge_kernel,
        grid=(t // tm, d_model // tn),
        in_specs=[
            pl.BlockSpec((tm, ka), lambda i, j: (i, 0)),
            pl.BlockSpec((tm, kb), lambda i, j: (i, 0)),
            pl.BlockSpec((ka, tn), lambda i, j: (0, j)),
            pl.BlockSpec((kb, tn), lambda i, j: (0, j)),
            pl.BlockSpec((tm, tn), lambda i, j: (i, ga_col // tn + j)),
            pl.BlockSpec((tm, tn), lambda i, j: (i, gb_col // tn + j)),
        ],
        out_specs=pl.BlockSpec((tm, tn), lambda i, j: (i, j)),
        out_shape=jax.ShapeDtypeStruct((t, d_model), _BF16),
        compiler_params=_params("parallel", "parallel"),
        name="merge",
    )(oa, ob, wa, wb, proj, proj)


def _mm_residual_kernel(a_ref, b_ref, r_ref, o_ref):
    o_ref[...] = r_ref[...] + _dot(a_ref[...], b_ref[...])


def _matmul_residual(a, b, res):
    m, k = a.shape
    n = b.shape[1]
    tm = min(m, 1024)
    tn = _tile_n(n)
    return pl.pallas_call(
        _mm_residual_kernel,
        grid=(m // tm, n // tn),
        in_specs=[
            pl.BlockSpec((tm, k), lambda i, j: (i, 0)),
            pl.BlockSpec((k, tn), lambda i, j: (0, j)),
            pl.BlockSpec((tm, tn), lambda i, j: (i, j)),
        ],
        out_specs=pl.BlockSpec((tm, tn), lambda i, j: (i, j)),
        out_shape=jax.ShapeDtypeStruct((m, n), _F32),
        compiler_params=_params("parallel", "parallel"),
        name="matmul_residual",
    )(a, b, res)


def _hgrn_inputs(q, fpre, v, logits, *, layer, rows):
    c_len = HGRN_CHUNK

    def pad(x):
        if rows == c_len:
            return x
        return jnp.concatenate([x, jnp.zeros((c_len - rows, x.shape[1]), x.dtype)], axis=0)

    ex = jnp.exp(logits - jnp.max(logits, axis=0, keepdims=True))
    lb = jnp.sum(ex[: layer + 1], axis=0, keepdims=True) / jnp.sum(ex, axis=0, keepdims=True)

    forget = lb + (1.0 - lb) * jax.nn.sigmoid(fpre)
    b = pad(jnp.log(forget))
    row = lax.broadcasted_iota(jnp.int32, (c_len, LANES), 0)
    shift = 1
    while shift < c_len:
        b = b + jnp.where(row >= shift, pltpu.roll(b, shift, 0), 0.0)
        shift *= 2
    return pad(q), pad(1.0 - forget), pad(v), b


def _hgrn_start(b, i):
    return b[i * HGRN_SUB - 1:i * HGRN_SUB] if i > 0 else jnp.zeros((1, LANES), _F32)


def _hgrn_att_pairwise(q, kk, b, n_sub):
    c_len, sub = HGRN_CHUNK, HGRN_SUB
    rowc = lax.broadcasted_iota(jnp.int32, (sub, LANES), 0)
    lane = lax.broadcasted_iota(jnp.int32, (sub, LANES), 1)
    att_rows = []
    for i in range(n_sub):
        lo = i * sub
        qi, bi, ki = q[lo:lo + sub], b[lo:lo + sub], kk[lo:lo + sub]
        att = jnp.zeros((sub, LANES), _F32)
        for s in range(sub):
            dec = jnp.exp(jnp.minimum(bi - bi[s:s + 1], 0.0))
            x = jnp.where(rowc >= s, qi * dec * ki[s:s + 1], 0.0)
            att = jnp.where(lane == lo + s, jnp.sum(x, axis=-1, keepdims=True), att)
        if i > 0:
            ref_b = _hgrn_start(b, i)
            qt = (qi * jnp.exp(bi - ref_b)).astype(_BF16)
            kp = kk[:lo] * jnp.exp(ref_b - b[:lo])
            kp = jnp.concatenate([kp, jnp.zeros((c_len - lo, LANES), _F32)], axis=0).astype(_BF16)
            att = att + _dot_nt(qt, kp)
        att_rows.append(att)
    if n_sub * sub < c_len:
        att_rows.append(jnp.zeros((c_len - n_sub * sub, LANES), _F32))
    return jnp.concatenate(att_rows, axis=0)


def _hgrn_att_factored(q, kk, b, n_sub):
    c_len, sub = HGRN_CHUNK, HGRN_SUB
    rowc = lax.broadcasted_iota(jnp.int32, (sub, LANES), 0)
    lane = lax.broadcasted_iota(jnp.int32, (sub, LANES), 1)
    att_rows = []
    for i in range(n_sub):
        lo, hi = i * sub, (i + 1) * sub
        ref_b = _hgrn_start(b, i)
        qt = (q[lo:hi] * jnp.exp(b[lo:hi] - ref_b)).astype(_BF16)
        kp = kk[:hi] * jnp.exp(ref_b - b[:hi])
        if hi < c_len:
            kp = jnp.concatenate([kp, jnp.zeros((c_len - hi, LANES), _F32)], axis=0)
        att_rows.append(jnp.where(lane <= rowc + lo, _dot_nt(qt, kp.astype(_BF16)), 0.0))
    if n_sub * sub < c_len:
        att_rows.append(jnp.zeros((c_len - n_sub * sub, LANES), _F32))
    return jnp.concatenate(att_rows, axis=0)


def _hgrn_outputs(q, kk, v, b, att, st, ag, gn, rows):
    c_len = HGRN_CHUNK
    o = _dot_nt((q * jnp.exp(b)).astype(_BF16), st.astype(_BF16))
    o = o + _dot(att.astype(_BF16), v.astype(_BF16))
    b_last = b[c_len - 1:c_len]
    k_hat = (kk * jnp.exp(b_last - b)).astype(_BF16)
    st_new = st * jnp.exp(b_last) + _dot(v.T.astype(_BF16), k_hat)
    oo = o[:rows]
    on = oo * lax.rsqrt(jnp.mean(oo * oo, axis=-1, keepdims=True) + RMS_EPS) * gn
    return on * (ag * jax.nn.sigmoid(ag)), st_new


def _hgrn_kernel(q_ref, f_ref, i_ref, g_ref, lbl_ref, gn_ref, s0_ref, o_ref, sfin_ref, st_ref, *,
                 layer, rows, hb):
    ci = pl.program_id(2)
    sub = HGRN_SUB
    n_sub = -(-rows // sub)

    @pl.when(ci == 0)
    def _():
        for h in range(hb):
            st_ref[h] = s0_ref[0, h].T

    cols = [slice(h * LANES, (h + 1) * LANES) for h in range(hb)]
    heads = [_hgrn_inputs(q_ref[:, c], f_ref[:, c], i_ref[:, c], lbl_ref[:, c], layer=layer, rows=rows)
             for c in cols]
    spans = [_hgrn_start(b, i) - b[(i + 1) * sub - 1:(i + 1) * sub] for _, _, _, b in heads for i in range(n_sub)]
    worst = jnp.max(jnp.concatenate(spans, axis=0))
    atts = lax.cond(worst < HGRN_SAFE_SPAN,
                    lambda: tuple(_hgrn_att_factored(q, kk, b, n_sub) for q, kk, _, b in heads),
                    lambda: tuple(_hgrn_att_pairwise(q, kk, b, n_sub) for q, kk, _, b in heads))
    for h, c in enumerate(cols):
        q, kk, v, b = heads[h]
        out, st_new = _hgrn_outputs(q, kk, v, b, atts[h], st_ref[h], g_ref[:, c], gn_ref[:, c], rows)
        o_ref[:, c] = out.astype(o_ref.dtype)
        st_ref[h] = st_new

    @pl.when(ci == pl.num_programs(2) - 1)
    def _():
        for h in range(hb):
            sfin_ref[0, h] = st_ref[h].T


def _hgrn(proj, lb_logits, gnorm, s0, layer, bsz, length, width, hb=4):
    heads = A_HEADS
    dk = width // heads
    if length >= HGRN_CHUNK:
        assert length % HGRN_CHUNK == 0
        rows, nc = HGRN_CHUNK, length // HGRN_CHUNK
    else:
        assert length % SUBLANES == 0
        rows, nc = length, 1
        hb = heads
    hb = min(hb, heads)
    assert dk == LANES and heads % hb == 0
    sec = heads // hb
    wb = hb * dk

    def col(section):
        return lambda b, h, c: (b * nc + c, section * sec + h)

    return pl.pallas_call(
        functools.partial(_hgrn_kernel, layer=layer, rows=rows, hb=hb),
        grid=(bsz, heads // hb, nc),
        in_specs=[
            pl.BlockSpec((rows, wb), col(0)),
            pl.BlockSpec((rows, wb), col(1)),
            pl.BlockSpec((rows, wb), col(2)),
            pl.BlockSpec((rows, wb), col(3)),
            pl.BlockSpec((lb_logits.shape[0], wb), lambda b, h, c: (0, h)),
            pl.BlockSpec((1, wb), lambda b, h, c: (0, h)),
            pl.BlockSpec((1, hb, dk, dk), lambda b, h, c: (b, h, 0, 0)),
        ],
        out_specs=[
            pl.BlockSpec((rows, wb), lambda b, h, c: (b * nc + c, h)),
            pl.BlockSpec((1, hb, dk, dk), lambda b, h, c: (b, h, 0, 0)),
        ],
        out_shape=[
            jax.ShapeDtypeStruct((bsz * length, width), _BF16 if rows % (2 * SUBLANES) == 0 else _F32),
            jax.ShapeDtypeStruct((bsz, heads, dk, dk), _F32),
        ],
        scratch_shapes=[pltpu.VMEM((hb, dk, dk), _F32)],
        compiler_params=_params("parallel", "parallel", "arbitrary"),
        name="hgrn2",
    )(proj, proj, proj, proj, lb_logits.astype(_F32), gnorm.reshape(1, width).astype(_F32), s0.astype(_F32))


def _moba_prompt_block(q_ref, k_ref, v_ref, o_ref, *, cols, own, nblk):
    blk = MOBA_BLOCK
    hd = cols.stop - cols.start
    n_sel = min(MOBA_TOPK, nblk, own)
    q = q_ref[:, cols].astype(_BF16)
    k = k_ref[:(own + 1) * blk, cols]
    s = _dot_nt(q, k.astype(_BF16)) * (hd ** -0.5)

    sel = [None] * own
    if own > n_sel:
        k_mean = jnp.mean(k[:own * blk].reshape(own, blk, hd), axis=1)
        k_mean = jnp.concatenate([k_mean, jnp.zeros((LANES - own, hd), _F32)], axis=0)
        gate = _dot_nt(k_mean.astype(_BF16), q)
        nrow = -(-own // SUBLANES) * SUBLANES
        blk_id = lax.broadcasted_iota(jnp.int32, (nrow, blk), 0)
        g = jnp.where(blk_id < own, gate[:nrow], NEG_INF)
        rank = jnp.zeros((nrow, blk), jnp.int32)
        for jp in range(own):
            other = g[jp:jp + 1]
            ahead = (other > g) | ((other == g) & (blk_id > jp))
            rank = rank + ahead.astype(jnp.int32)
        picked = jnp.where((rank < n_sel) & (blk_id < own), 1.0, 0.0)
        picked = jnp.concatenate([picked, jnp.zeros((LANES - nrow, blk), _F32)], axis=0).T
        sel = [picked[:, j:j + 1] > 0.0 for j in range(own)]

    row = lax.broadcasted_iota(jnp.int32, (blk, blk), 0)
    col = lax.broadcasted_iota(jnp.int32, (blk, blk), 1)
    masked = []
    for j in range(own + 1):
        sj = s[:, j * blk:(j + 1) * blk]
        if j == own:
            sj = jnp.where(col <= row, sj, NEG_INF)
        elif sel[j] is not None:
            sj = jnp.where(sel[j], sj, NEG_INF)
        masked.append(sj)
    m = masked[0].max(axis=-1, keepdims=True)
    for sj in masked[1:]:
        m = jnp.maximum(m, sj.max(axis=-1, keepdims=True))
    l = jnp.zeros((blk, 1), _F32)
    acc = jnp.zeros((blk, hd), _F32)
    for j, sj in enumerate(masked):
        p = jnp.exp(sj - m)
        l = l + p.sum(axis=-1, keepdims=True)
        acc = acc + _dot(p.astype(_BF16), v_ref[j * blk:(j + 1) * blk, cols].astype(_BF16))
    o_ref[:, cols] = (acc / l).astype(o_ref.dtype)


def _moba_prompt_kernel(q_ref, k_ref, v_ref, o_ref, kout_ref, vout_ref, *, nblk, hp):
    i = pl.program_id(2)

    @pl.when(i == 0)
    def _():
        kout_ref[...] = k_ref[...]
        vout_ref[...] = v_ref[...]

    def blocks(own):
        for h in range(hp):
            _moba_prompt_block(q_ref, k_ref, v_ref, o_ref, cols=slice(h * LANES, (h + 1) * LANES), own=own, nblk=nblk)

    for own in range(nblk):
        pl.when(i == own)(functools.partial(blocks, own))


def _moba_prompt(proj, bsz, length, q_col, k_col, v_col, width, hp=2):
    heads = B_HEADS
    hd = width // heads
    hp = min(hp, heads)
    assert hd == LANES and length % MOBA_BLOCK == 0 and heads % hp == 0
    nblk = length // MOBA_BLOCK
    assert nblk <= LANES
    wb = hp * hd
    kv_spec = pl.BlockSpec((length, wb), lambda b, h, i: (b, h))
    kv_shape = jax.ShapeDtypeStruct((bsz * length, width), _F32)
    return pl.pallas_call(
        functools.partial(_moba_prompt_kernel, nblk=nblk, hp=hp),
        grid=(bsz, heads // hp, nblk),
        in_specs=[
            pl.BlockSpec((MOBA_BLOCK, wb), lambda b, h, i: (b * nblk + i, q_col // wb + h)),
            pl.BlockSpec((length, wb), lambda b, h, i: (b, k_col // wb + h)),
            pl.BlockSpec((length, wb), lambda b, h, i: (b, v_col // wb + h)),
        ],
        out_specs=[pl.BlockSpec((MOBA_BLOCK, wb), lambda b, h, i: (b * nblk + i, h)), kv_spec, kv_spec],
        out_shape=[jax.ShapeDtypeStruct((bsz * length, width), _BF16), kv_shape, kv_shape],
        compiler_params=_params("parallel", "parallel", "arbitrary"),
        name="moba_prompt",
    )(proj, proj, proj)


def _moba_sample_kernel(pt_ref, q_ref, kn_ref, vn_ref, *refs, npages, heads, lq, group):
    kc_refs, vc_refs = refs[:group], refs[group:2 * group]
    o_ref, qbd_ref, ks_ref, km_ref, m_ref, l_ref, acc_ref = refs[2 * group:]
    p = pl.program_id(1)
    nsteps = npages // group
    hd = kc_refs[0].shape[2]
    page = kc_refs[0].shape[1] // heads
    width = heads * hd
    rows = heads * lq
    blk_pages = MOBA_BLOCK // page
    nb = npages // blk_pages
    scale = hd ** -0.5

    def by_token(ref):
        return jnp.concatenate([ref[0, pl.ds(h, page, stride=heads), :] for h in range(heads)], axis=1)

    def own_head(x):
        return jnp.concatenate([x[h * lq:(h + 1) * lq, h * hd:(h + 1) * hd] for h in range(heads)], axis=0)

    @pl.when(p == 0)
    def _():
        qt = jnp.concatenate([q_ref[...]] * heads, axis=0)
        rh = _div(lax.broadcasted_iota(jnp.int32, (rows, width), 0), lq)
        ch = _div(lax.broadcasted_iota(jnp.int32, (rows, width), 1), hd)
        qbd_ref[...] = jnp.where(rh == ch, qt, 0.0).astype(_BF16)

    qbd = qbd_ref[...]
    for g0 in range(0, group, blk_pages):
        blk = (p * group + g0) // blk_pages
        s = [_dot_nt(qbd, by_token(kc_refs[g0 + g]).astype(_BF16)) * scale for g in range(blk_pages)]
        m = s[0].max(axis=1, keepdims=True)
        for sg in s[1:]:
            m = jnp.maximum(m, sg.max(axis=1, keepdims=True))
        l = jnp.zeros((rows, 1), _F32)
        acc = jnp.zeros((rows, width), _F32)
        for g in range(blk_pages):
            w = jnp.exp(s[g] - m)
            l = l + w.sum(axis=1, keepdims=True)
            acc = acc + _dot(w.astype(_BF16), by_token(vc_refs[g0 + g]).astype(_BF16))
            ks_ref[p * group + g0 + g] = kc_refs[g0 + g][0].reshape(page, heads, hd).sum(axis=0)
        m_ref[blk] = m
        l_ref[blk] = l
        acc_ref[blk] = own_head(acc)

    @pl.when(p == nsteps - 1)
    def _():
        k_sum = ks_ref[...].reshape(nb, blk_pages, heads, hd).sum(axis=1)
        km_ref[...] = k_sum.reshape(nb * heads, hd) * (1.0 / MOBA_BLOCK)
        k_mean = jnp.concatenate([km_ref[pl.ds(h, nb, stride=heads), :] for h in range(heads)], axis=1)
        gate = _dot_nt(qbd, k_mean.astype(_BF16))
        lane = lax.broadcasted_iota(jnp.int32, (rows, nb), 1)
        sel = jnp.zeros((rows, nb), _F32)
        for _ in range(MOBA_TOPK):
            best = jnp.max(gate, axis=1, keepdims=True)
            idx = jnp.min(jnp.where(gate == best, lane, nb), axis=1, keepdims=True)
            hit = lane == idx
            sel = jnp.where(hit, 1.0, sel)
            gate = jnp.where(hit, -jnp.inf, gate)

        pad = 2 * SUBLANES - lq
        kn = jnp.concatenate([kn_ref[...], jnp.zeros((pad, width), _F32)], axis=0).astype(_BF16)
        vn = jnp.concatenate([vn_ref[...], jnp.zeros((pad, width), _F32)], axis=0).astype(_BF16)
        s_own = _dot_nt(qbd, kn) * scale
        qi = _mod(lax.broadcasted_iota(jnp.int32, s_own.shape, 0), lq)
        kj = lax.broadcasted_iota(jnp.int32, s_own.shape, 1)
        s_own = jnp.where((kj <= qi) & (kj < lq), s_own, NEG_INF)

        picked = [sel[:, b:b + 1] > 0.0 for b in range(nb)]
        top = jnp.max(s_own, axis=1, keepdims=True)
        for b in range(nb):
            top = jnp.maximum(top, jnp.where(picked[b], m_ref[b], NEG_INF))
        p_own = jnp.exp(s_own - top)
        l = jnp.sum(p_own, axis=1, keepdims=True)
        out = own_head(_dot(p_own.astype(_BF16), vn))
        for b in range(nb):
            c = jnp.where(picked[b], jnp.exp(jnp.minimum(m_ref[b] - top, 0.0)), 0.0)
            l = l + c * l_ref[b]
            out = out + c * acc_ref[b]
        out = out / l
        for h in range(heads):
            o_ref[:, h * hd:(h + 1) * hd] = out[h * lq:(h + 1) * lq, :]


def _moba_sample(proj, cache_k, cache_v, page_table, bsz, lq, q_col, k_col, v_col, width):
    heads = B_HEADS
    hd = width // heads
    n_phys, page = cache_k.shape[0], cache_k.shape[1]
    npages = page_table.shape[1]
    past = npages * page
    assert MOBA_BLOCK % page == 0 and past % MOBA_BLOCK == 0 and lq % SUBLANES == 0 and lq <= 2 * SUBLANES
    assert past // MOBA_BLOCK >= MOBA_TOPK and heads % SUBLANES == 0
    blk_pages = MOBA_BLOCK // page
    nb = past // MOBA_BLOCK
    group = blk_pages * max(g for g in (1, 2, 4, 8) if nb % g == 0)
    nsteps = npages // group
    rows = heads * lq
    kc = cache_k.reshape(n_phys, page * heads, hd)
    vc = cache_v.reshape(n_phys, page * heads, hd)

    def nth_page(g):
        return lambda b, p, pt: (pt[b, p * group + g], 0, 0)

    page_specs = [pl.BlockSpec((1, page * heads, hd), nth_page(g)) for g in range(group)] * 2
    grid_spec = pltpu.PrefetchScalarGridSpec(
        num_scalar_prefetch=1,
        grid=(bsz, nsteps),
        in_specs=[
            pl.BlockSpec((lq, width), lambda b, p, pt: (b, q_col // width)),
            pl.BlockSpec((lq, width), lambda b, p, pt: (b, k_col // width)),
            pl.BlockSpec((lq, width), lambda b, p, pt: (b, v_col // width)),
        ] + page_specs,
        out_specs=pl.BlockSpec((lq, width), lambda b, p, pt: (b, 0)),
        scratch_shapes=[
            pltpu.VMEM((rows, width), _BF16),
            pltpu.VMEM((npages, heads, hd), _F32),
            pltpu.VMEM((nb * heads, hd), _F32),
            pltpu.VMEM((nb, rows, 1), _F32),
            pltpu.VMEM((nb, rows, 1), _F32),
            pltpu.VMEM((nb, rows, hd), _F32),
        ],
    )
    return pl.pallas_call(
        functools.partial(_moba_sample_kernel, npages=npages, heads=heads, lq=lq, group=group),
        grid_spec=grid_spec,
        out_shape=jax.ShapeDtypeStruct((bsz * lq, width), _F32),
        compiler_params=_params("parallel", "arbitrary"),
        name="moba_sample",
    )(page_table.astype(jnp.int32), proj, proj, proj, *([kc] * group), *([vc] * group))


def _top_rows(x, k):
    r, t = x.shape
    rowi = lax.broadcasted_iota(jnp.int32, (r, t), 0).astype(_F32)
    slot = lax.broadcasted_iota(jnp.int32, (k, t), 0)
    vals = jnp.zeros((k, t), _F32)
    idxs = jnp.zeros((k, t), _F32)
    for n in range(k):
        best = jnp.max(x, axis=0, keepdims=True)
        idx = jnp.min(jnp.where(x == best, rowi, float(r)), axis=0, keepdims=True)
        vals = jnp.where(slot == n, best, vals)
        idxs = jnp.where(slot == n, idx, idxs)
        x = jnp.where(rowi == idx, -jnp.inf, x)
    return vals, idxs


def _candidate_pieces(topk):
    pieces, start = [], 0
    a = 0
    while topk // (a + 1) > 1:
        nb = topk // (a + 1)
        rows = -(-nb // SUBLANES) * SUBLANES
        pieces.append((start, rows, a, nb))
        start += rows
        a += 1
    return pieces, (start, a)


def _peer_retrieve_kernel(q_ref, sk_ref, *refs):
    if len(refs) > 3:
        u_ref, v_ref, g_ref, i1_ref, i2_ref, ub_ref, vb_ref = refs
        ub_ref[...] = u_ref[...].astype(_BF16)
        vb_ref[...] = v_ref[...].astype(_BF16)
    else:
        g_ref, i1_ref, i2_ref = refs
    topk = PEER_TOPK
    dh = sk_ref.shape[-1]
    tm = q_ref.shape[0]
    tops = []
    for a in range(2):
        qa = q_ref[:, a * dh:(a + 1) * dh].astype(_BF16)
        tops.append(_top_rows(_dot_nt(sk_ref[0, a], qa), topk))
    (s1, k1), (s2, k2) = tops

    pieces, (tail_start, tail_a) = _candidate_pieces(topk)
    parts = []
    for _, rows, a, nb in pieces:
        part = s1[a:a + 1] + s2[:rows]
        if nb < rows:
            part = jnp.where(lax.broadcasted_iota(jnp.int32, (rows, tm), 0) < nb, part, -jnp.inf)
        parts.append(part)
    tail = s1[tail_a:] + s2[0:1]
    tail_rows = -(-(topk - tail_a) // SUBLANES) * SUBLANES
    if tail_rows > topk - tail_a:
        tail = jnp.concatenate([tail, jnp.full((tail_rows - (topk - tail_a), tm), -jnp.inf, _F32)], axis=0)
    cand = jnp.concatenate(parts + [tail], axis=0)
    best_s, pos = _top_rows(cand, topk)

    a_sel = pos - float(tail_start - tail_a)
    b_sel = jnp.zeros_like(pos)
    for start, rows, a, _ in pieces:
        inside = (pos >= float(start)) & (pos < float(start + rows))
        a_sel = jnp.where(inside, float(a), a_sel)
        b_sel = jnp.where(inside, pos - float(start), b_sel)
    i1 = jnp.zeros_like(pos)
    i2 = jnp.zeros_like(pos)
    for a in range(topk):
        i1 = jnp.where(a_sel == float(a), k1[a:a + 1], i1)
        i2 = jnp.where(b_sel == float(a), k2[a:a + 1], i2)
    e = jnp.exp(best_s - best_s[0:1])
    g_ref[...] = e / jnp.sum(e, axis=0, keepdims=True)
    i1_ref[...] = i1
    i2_ref[...] = i2


def _peer_retrieve(q, sub_keys, tables=None, tm=512):
    t = q.shape[0]
    heads, _, nkeys, dh = sub_keys.shape
    tm = min(tm, t)
    steps = (t // tm) * heads
    spec = pl.BlockSpec((PEER_TOPK, tm), lambda i, h: (h, i))
    shape = jax.ShapeDtypeStruct((heads * PEER_TOPK, t), _F32)
    in_specs = [
        pl.BlockSpec((tm, 2 * dh), lambda i, h: (i, h)),
        pl.BlockSpec((1, 2, nkeys, dh), lambda i, h: (h, 0, 0, 0)),
    ]
    out_specs, out_shape, operands = [spec, spec, spec], [shape, shape, shape], [q, sub_keys.astype(_BF16)]
    if tables is not None:
        n_exp, d = tables[0].shape
        slab = n_exp // steps
        assert slab * steps == n_exp and slab % (2 * SUBLANES) == 0, (n_exp, steps)
        slab_spec = pl.BlockSpec((slab, d), lambda i, h: (i * heads + h, 0))
        in_specs += [slab_spec, slab_spec]
        out_specs += [slab_spec, slab_spec]
        out_shape += [jax.ShapeDtypeStruct((n_exp, d), _BF16)] * 2
        operands += list(tables)
    return pl.pallas_call(
        _peer_retrieve_kernel,
        grid=(t // tm, heads),
        in_specs=in_specs,
        out_specs=out_specs,
        out_shape=out_shape,
        compiler_params=_params("parallel", "parallel"),
        name="peer_retrieve",
    )(*operands)


def _peer_weights_kernel(g_ref, i1_ref, i2_ref, w_ref, gt_ref, i1t_ref, i2t_ref, scr_ref, *, tm, stride):
    nk = LANES
    gt_ref[...] = g_ref[...].T
    i1t_ref[...] = i1_ref[...].T
    i2t_ref[...] = i2_ref[...].T
    sub = lax.broadcasted_iota(jnp.int32, (nk, nk), 0).astype(_F32)

    zeros = jnp.zeros((nk, nk), _F32)

    def one_hots(n):
        r1 = jnp.broadcast_to(i1t_ref[pl.ds(n, 1), :], (nk, nk))
        r2 = jnp.broadcast_to(i2t_ref[pl.ds(n, 1), :], (nk, nk))
        rg = jnp.broadcast_to(gt_ref[pl.ds(n, 1), :], (nk, nk))
        return jnp.where(sub == r1, 1.0, 0.0), jnp.where(sub == r2, rg, 0.0)

    def body(t, carry):
        n = 2 * t
        a1, a2 = one_hots(n)
        b1, b2 = one_hots(n + 1)
        m1 = jnp.concatenate([a1, b1], axis=1).astype(_BF16)
        m2 = jnp.concatenate([jnp.concatenate([a2, zeros], axis=1),
                              jnp.concatenate([zeros, b2], axis=1)], axis=0).astype(_BF16)
        planes = _dot_nt(m1, m2)
        scr_ref[pl.ds(n, nk, stride=stride), :] = planes[:, :nk]
        scr_ref[pl.ds(n + 1, nk, stride=stride), :] = planes[:, nk:]
        return carry

    lax.fori_loop(0, tm // 2, body, 0, unroll=4 * SUBLANES)
    for j in range(nk):
        w_ref[:, j * nk:(j + 1) * nk] = scr_ref[j * stride:j * stride + tm, :].astype(w_ref.dtype)


def _peer_weights(g, i1, i2, nkeys, tm=256):
    picks, t = g.shape
    assert picks == LANES and nkeys == LANES
    tm = min(tm, t)
    stride = tm + SUBLANES
    spec = pl.BlockSpec((picks, tm), lambda i: (0, i))
    return pl.pallas_call(
        functools.partial(_peer_weights_kernel, tm=tm, stride=stride),
        grid=(t // tm,),
        in_specs=[spec, spec, spec],
        out_specs=pl.BlockSpec((tm, nkeys * nkeys), lambda i: (i, 0)),
        out_shape=jax.ShapeDtypeStruct((t, nkeys * nkeys), _BF16),
        scratch_shapes=[
            pltpu.VMEM((tm, picks), _F32),
            pltpu.VMEM((tm, picks), _F32),
            pltpu.VMEM((tm, picks), _F32),
            pltpu.VMEM((nkeys * stride, nkeys), _F32),
        ],
        compiler_params=_params("parallel"),
        name="peer_weights",
    )(g, i1, i2)


def _peer_ffn_kernel(xn_ref, w_ref, u_ref, v_ref, o_ref, *, tn):
    j = pl.program_id(1)

    @pl.when(j == 0)
    def _():
        o_ref[...] = jnp.zeros_like(o_ref)

    xn = xn_ref[...]
    parts = []
    for e in range(0, u_ref.shape[0], tn):
        h = _dot_nt(xn, u_ref[e:e + tn, :])
        w = w_ref[:, e:e + tn].astype(_F32)
        parts.append(jnp.where(w != 0.0, w * jax.nn.gelu(h), 0.0).astype(_BF16))
    coef = jnp.concatenate(parts, axis=1)
    for n in range(0, o_ref.shape[1], tn):
        o_ref[:, n:n + tn] += _dot(coef, v_ref[:, n:n + tn])


def _peer_ffn(xn, w, u, v, tm=1024, te=1024):
    t, d = xn.shape
    n_exp = u.shape[0]
    tm = min(tm, t)
    tn = _tile_n(d)
    blocks = 2 * (2 * (tm * d + tm * te + 2 * te * d) + 4 * tm * d)
    temps = tm * te * 2 + 4 * tm * tn * 4
    return pl.pallas_call(
        functools.partial(_peer_ffn_kernel, tn=tn),
        grid=(t // tm, n_exp // te),
        in_specs=[
            pl.BlockSpec((tm, d), lambda i, j: (i, 0)),
            pl.BlockSpec((tm, te), lambda i, j: (i, j)),
            pl.BlockSpec((te, d), lambda i, j: (j, 0)),
            pl.BlockSpec((te, d), lambda i, j: (j, 0)),
        ],
        out_specs=pl.BlockSpec((tm, d), lambda i, j: (i, 0)),
        out_shape=jax.ShapeDtypeStruct((t, d), _F32),
        compiler_params=_params("parallel", "arbitrary", vmem=max(VMEM_LIMIT, blocks + temps)),
        name="peer_ffn",
    )(xn, w, u, v)


def _residual_norm_kernel(x_ref, y_ref, g_ref, o_ref, *, final):
    x = x_ref[...] + y_ref[...]
    if final:
        x = x * lax.rsqrt(jnp.mean(x * x, axis=-1, keepdims=True) + RMS_EPS) * g_ref[...]
    o_ref[...] = x


def _residual_norm(x, y, gain, final):
    t, d = x.shape
    tm = min(t, 512)
    spec = pl.BlockSpec((tm, d), lambda i: (i, 0))
    return pl.pallas_call(
        functools.partial(_residual_norm_kernel, final=final),
        grid=(t // tm,),
        in_specs=[spec, spec, pl.BlockSpec((1, d), lambda i: (0, 0))],
        out_specs=spec,
        out_shape=jax.ShapeDtypeStruct((t, d), _F32),
        compiler_params=_params("parallel"),
        name="residual_norm",
    )(x, y, gain.reshape(1, d).astype(_F32))


def _layer(x, bsz, length, s0, past, lw, layer, final):
    d_model = x.shape[1]
    a_width = lw["gnorm"].shape[-1]
    b_width = lw["w_b"].shape[0]
    proj = _norm_matmul(x, lw["norm_mix"], lw["w_in"])
    b_q = 4 * a_width
    b_k, b_v = b_q + b_width, b_q + 2 * b_width
    g_a = b_q + 3 * b_width
    g_b = g_a + d_model

    o_a, s_new = _hgrn(proj, lw["lb_logits"], lw["gnorm"], s0, layer, bsz, length, a_width)
    if past is None:
        o_b, k_new, v_new = _moba_prompt(proj, bsz, length, b_q, b_k, b_v, b_width)
    else:
        o_b = _moba_sample(proj, past[0], past[1], past[2], bsz, length, b_q, b_k, b_v, b_width)
        k_new, v_new = proj[:, b_k:b_k + b_width], proj[:, b_v:b_v + b_width]
    merged = _merge(o_a, o_b, lw["w_a"], lw["w_b"], proj, g_a, g_b, d_model)
    x = _matmul_residual(merged, lw["w_out"], x)

    q, xn = _norm_matmul(x, lw["norm_ffn"], lw["w_query"], emit_norm=True, out_dtype=_BF16)
    if "peer_u_bf16" in lw:
        gates, i1, i2 = _peer_retrieve(q, lw["sub_keys"])
    else:
        gates, i1, i2, lw["peer_u_bf16"], lw["peer_v_bf16"] = _peer_retrieve(
            q, lw["sub_keys"], tables=(lw["peer_u"], lw["peer_v"]))
    w = _peer_weights(gates, i1, i2, lw["sub_keys"].shape[2])
    x = _residual_norm(x, _peer_ffn(xn, w, lw["peer_u_bf16"], lw["peer_v_bf16"]), lw["norm_final"], final)

    kv_shape = (bsz, length, B_HEADS, b_width // B_HEADS)
    return x, s_new, k_new.reshape(kv_shape), v_new.reshape(kv_shape)


def kernel(x_prompt, x_sample, cache_k, cache_v, state_hgrn, page_table, norm_mix, w_in, hgrn_lb_logits,
           hgrn_gnorm, w_a_proj, w_b_proj, w_out, norm_ffn, peer_w_query, peer_sub_keys, peer_u, peer_v,
           norm_final):
    depth = w_in.shape[0]
    bp, lp, d_model = x_prompt.shape
    bs, ls, _ = x_sample.shape
    x_p = x_prompt.reshape(bp * lp, d_model)
    x_s = x_sample.reshape(bs * ls, d_model)
    outs = [[] for _ in range(6)]
    for layer in range(depth):
        lw = {
            "norm_mix": norm_mix[layer], "w_in": w_in[layer].astype(_BF16), "lb_logits": hgrn_lb_logits,
            "gnorm": hgrn_gnorm[layer], "w_a": w_a_proj[layer].astype(_BF16), "w_b": w_b_proj[layer].astype(_BF16),
            "w_out": w_out[layer].astype(_BF16), "norm_ffn": norm_ffn[layer],
            "w_query": peer_w_query[layer].astype(_BF16), "sub_keys": peer_sub_keys[layer],
            "peer_u": peer_u[layer], "peer_v": peer_v[layer], "norm_final": norm_final,
        }
        final = layer == depth - 1
        s0_p = jnp.zeros((bp,) + state_hgrn.shape[2:], _F32)
        x_p, s_p, k_p, v_p = _layer(x_p, bp, lp, s0_p, None, lw, layer, final)
        past = (cache_k[layer], cache_v[layer], page_table)
        x_s, s_s, k_s, v_s = _layer(x_s, bs, ls, state_hgrn[layer], past, lw, layer, final)
        for lst, val in zip(outs, (s_p, s_s, k_p, v_p, k_s, v_s)):
            lst.append(val)
    sp, ss, kp, vp, ks, vs = (jnp.stack(lst, axis=0) for lst in outs)
    return (x_p.reshape(bp, lp, d_model), x_s.reshape(bs, ls, d_model),
            sp.astype(state_hgrn.dtype), ss.astype(state_hgrn.dtype),
            kp.astype(cache_k.dtype), vp.astype(cache_v.dtype), ks.astype(cache_k.dtype), vs.astype(cache_v.dtype))
```

```python
import functools

import jax
import jax.numpy as jnp
from jax import lax
from jax.experimental import pallas as pl
from jax.experimental.pallas import tpu as pltpu

A_HEADS = 8
B_HEADS = 8
MOBA_BLOCK = 256
MOBA_TOPK = 3
PEER_HEADS = 8
PEER_TOPK = 16
RMS_EPS = 1e-6
NEG_INF = -1e30

LANES = 128
SUBLANES = 8
HGRN_CHUNK = 128
HGRN_SUB = 16
HGRN_SAFE_SPAN = 40.0
VMEM_LIMIT = 48 * 1024 * 1024

_F32 = jnp.float32
_BF16 = jnp.bfloat16
_NT = (((1,), (1,)), ((), ()))


def _params(*sem, vmem=VMEM_LIMIT):
    return pltpu.CompilerParams(dimension_semantics=sem, vmem_limit_bytes=vmem)


def _dot_nt(a, b):
    return lax.dot_general(a, b, _NT, preferred_element_type=_F32)


def _dot(a, b):
    return jnp.dot(a, b, preferred_element_type=_F32)


def _log2(n):
    assert n > 0 and n & (n - 1) == 0, n
    return n.bit_length() - 1


def _div(x, n):
    return lax.shift_right_logical(x, jnp.int32(_log2(n)))


def _mod(x, n):
    return lax.bitwise_and(x, jnp.int32((1 << _log2(n)) - 1))


def _tile_n(n, *cols, cap=512):
    for tn in (1024, 512, 256, LANES):
        if tn <= cap and all(v % tn == 0 for v in (n,) + cols):
            return tn
    raise ValueError((n, cols))


def _norm_matmul_kernel(x_ref, g_ref, b_ref, o_ref, xn_ref):
    @pl.when(pl.program_id(1) == 0)
    def _():
        x = x_ref[...]
        xn_ref[...] = (x * lax.rsqrt(jnp.mean(x * x, axis=-1, keepdims=True) + RMS_EPS) * g_ref[...]).astype(_BF16)

    o_ref[...] = _dot(xn_ref[...], b_ref[...])


def _norm_matmul(x, gain, b):
    m, k = x.shape
    n = b.shape[1]
    tm, tn = min(m, 1024), _tile_n(n, cap=1024)
    return pl.pallas_call(
        _norm_matmul_kernel,
        grid=(m // tm, n // tn),
        in_specs=[
            pl.BlockSpec((tm, k), lambda i, j: (i, 0)),
            pl.BlockSpec((1, k), lambda i, j: (0, 0)),
            pl.BlockSpec((k, tn), lambda i, j: (0, j)),
        ],
        out_specs=pl.BlockSpec((tm, tn), lambda i, j: (i, j)),
        out_shape=jax.ShapeDtypeStruct((m, n), _F32),
        scratch_shapes=[pltpu.VMEM((tm, k), _BF16)],
        compiler_params=_params("parallel", "arbitrary"),
        name="norm_matmul",
    )(x, gain.reshape(1, k).astype(_F32), b)


def _merge_kernel(oa_ref, ob_ref, wa_ref, wb_ref, ga_ref, gb_ref, o_ref):
    ya = _dot(oa_ref[...].astype(_BF16), wa_ref[...])
    yb = _dot(ob_ref[...].astype(_BF16), wb_ref[...])
    o_ref[...] = (jax.nn.sigmoid(ga_ref[...]) * ya + jax.nn.sigmoid(gb_ref[...]) * yb).astype(o_ref.dtype)


def _merge(oa, ob, wa, wb, proj, ga_col, gb_col, d_model):
    t, ka = oa.shape
    kb = ob.shape[1]
    tm = min(t, 1024)
    tn = _tile_n(d_model, ga_col, gb_col)
    return pl.pallas_call(
        _merge_kernel,
        grid=(t // tm, d_model // tn),
        in_specs=[
            pl.BlockSpec((tm, ka), lambda i, j: (i, 0)),
            pl.BlockSpec((tm, kb), lambda i, j: (i, 0)),
            pl.BlockSpec((ka, tn), lambda i, j: (0, j)),
            pl.BlockSpec((kb, tn), lambda i, j: (0, j)),
            pl.BlockSpec((tm, tn), lambda i, j: (i, ga_col // tn + j)),
            pl.BlockSpec((tm, tn), lambda i, j: (i, gb_col // tn + j)),
        ],
        out_specs=pl.BlockSpec((tm, tn), lambda i, j: (i, j)),
        out_shape=jax.ShapeDtypeStruct((t, d_model), _BF16),
        compiler_params=_params("parallel", "parallel"),
        name="merge",
    )(oa, ob, wa, wb, proj, proj)


def _mm_residual_kernel(a_ref, b_ref, r_ref, o_ref):
    o_ref[...] = r_ref[...] + _dot(a_ref[...], b_ref[...])


def _matmul_residual(a, b, res):
    m, k = a.shape
    n = b.shape[1]
    tm = min(m, 1024)
    tn = _tile_n(n)
    return pl.pallas_call(
        _mm_residual_kernel,
        grid=(m // tm, n // tn),
        in_specs=[
            pl.BlockSpec((tm, k), lambda i, j: (i, 0)),
            pl.BlockSpec((k, tn), lambda i, j: (0, j)),
            pl.BlockSpec((tm, tn), lambda i, j: (i, j)),
        ],
        out_specs=pl.BlockSpec((tm, tn), lambda i, j: (i, j)),
        out_shape=jax.ShapeDtypeStruct((m, n), _F32),
        compiler_params=_params("parallel", "parallel"),
        name="matmul_residual",
    )(a, b, res)


def _hgrn_inputs(q, fpre, v, logits, *, layer, rows):
    c_len = HGRN_CHUNK

    def pad(x):
        if rows == c_len:
            return x
        return jnp.concatenate([x, jnp.zeros((c_len - rows, x.shape[1]), x.dtype)], axis=0)

    ex = jnp.exp(logits - jnp.max(logits, axis=0, keepdims=True))
    lb = jnp.sum(ex[: layer + 1], axis=0, keepdims=True) / jnp.sum(ex, axis=0, keepdims=True)

    forget = lb + (1.0 - lb) * jax.nn.sigmoid(fpre)
    b = pad(jnp.log(forget))
    row = lax.broadcasted_iota(jnp.int32, (c_len, LANES), 0)
    shift = 1
    while shift < c_len:
        b = b + jnp.where(row >= shift, pltpu.roll(b, shift, 0), 0.0)
        shift *= 2
    return pad(q), pad(1.0 - forget), pad(v), b


def _hgrn_start(b, i):
    return b[i * HGRN_SUB - 1:i * HGRN_SUB] if i > 0 else jnp.zeros((1, LANES), _F32)


def _hgrn_att_pairwise(q, kk, b, n_sub):
    c_len, sub = HGRN_CHUNK, HGRN_SUB
    rowc = lax.broadcasted_iota(jnp.int32, (sub, LANES), 0)
    lane = lax.broadcasted_iota(jnp.int32, (sub, LANES), 1)
    att_rows = []
    for i in range(n_sub):
        lo = i * sub
        qi, bi, ki = q[lo:lo + sub], b[lo:lo + sub], kk[lo:lo + sub]
        att = jnp.zeros((sub, LANES), _F32)
        for s in range(sub):
            dec = jnp.exp(jnp.minimum(bi - bi[s:s + 1], 0.0))
            x = jnp.where(rowc >= s, qi * dec * ki[s:s + 1], 0.0)
            att = jnp.where(lane == lo + s, jnp.sum(x, axis=-1, keepdims=True), att)
        if i > 0:
            ref_b = _hgrn_start(b, i)
            qt = (qi * jnp.exp(bi - ref_b)).astype(_BF16)
            kp = kk[:lo] * jnp.exp(ref_b - b[:lo])
            kp = jnp.concatenate([kp, jnp.zeros((c_len - lo, LANES), _F32)], axis=0).astype(_BF16)
            att = att + _dot_nt(qt, kp)
        att_rows.append(att)
    if n_sub * sub < c_len:
        att_rows.append(jnp.zeros((c_len - n_sub * sub, LANES), _F32))
    return jnp.concatenate(att_rows, axis=0)


def _hgrn_att_factored(q, kk, b, n_sub):
    c_len, sub = HGRN_CHUNK, HGRN_SUB
    rowc = lax.broadcasted_iota(jnp.int32, (sub, LANES), 0)
    lane = lax.broadcasted_iota(jnp.int32, (sub, LANES), 1)
    att_rows = []
    for i in range(n_sub):
        lo, hi = i * sub, (i + 1) * sub
        ref_b = _hgrn_start(b, i)
        qt = (q[lo:hi] * jnp.exp(b[lo:hi] - ref_b)).astype(_BF16)
        kp = kk[:hi] * jnp.exp(ref_b - b[:hi])
        if hi < c_len:
            kp = jnp.concatenate([kp, jnp.zeros((c_len - hi, LANES), _F32)], axis=0)
        att_rows.append(jnp.where(lane <= rowc + lo, _dot_nt(qt, kp.astype(_BF16)), 0.0))
    if n_sub * sub < c_len:
        att_rows.append(jnp.zeros((c_len - n_sub * sub, LANES), _F32))
    return jnp.concatenate(att_rows, axis=0)


def _hgrn_outputs(q, kk, v, b, att, st, ag, gn, rows):
    c_len = HGRN_CHUNK
    o = _dot_nt((q * jnp.exp(b)).astype(_BF16), st.astype(_BF16))
    o = o + _dot(att.astype(_BF16), v.astype(_BF16))
    b_last = b[c_len - 1:c_len]
    k_hat = (kk * jnp.exp(b_last - b)).astype(_BF16)
    st_new = st * jnp.exp(b_last) + _dot(v.T.astype(_BF16), k_hat)
    oo = o[:rows]
    on = oo * lax.rsqrt(jnp.mean(oo * oo, axis=-1, keepdims=True) + RMS_EPS) * gn
    return on * (ag * jax.nn.sigmoid(ag)), st_new


def _hgrn_kernel(q_ref, f_ref, i_ref, g_ref, lbl_ref, gn_ref, s0_ref, o_ref, sfin_ref, st_ref, *,
                 layer, rows, hb):
    ci = pl.program_id(2)
    sub = HGRN_SUB
    n_sub = -(-rows // sub)

    @pl.when(ci == 0)
    def _():
        for h in range(hb):
            st_ref[h] = s0_ref[0, h].T

    cols = [slice(h * LANES, (h + 1) * LANES) for h in range(hb)]
    heads = [_hgrn_inputs(q_ref[:, c], f_ref[:, c], i_ref[:, c], lbl_ref[:, c], layer=layer, rows=rows)
             for c in cols]
    spans = [_hgrn_start(b, i) - b[(i + 1) * sub - 1:(i + 1) * sub] for _, _, _, b in heads for i in range(n_sub)]
    worst = jnp.max(jnp.concatenate(spans, axis=0))
    atts = lax.cond(worst < HGRN_SAFE_SPAN,
                    lambda: tuple(_hgrn_att_factored(q, kk, b, n_sub) for q, kk, _, b in heads),
                    lambda: tuple(_hgrn_att_pairwise(q, kk, b, n_sub) for q, kk, _, b in heads))
    for h, c in enumerate(cols):
        q, kk, v, b = heads[h]
        out, st_new = _hgrn_outputs(q, kk, v, b, atts[h], st_ref[h], g_ref[:, c], gn_ref[:, c], rows)
        o_ref[:, c] = out.astype(o_ref.dtype)
        st_ref[h] = st_new

    @pl.when(ci == pl.num_programs(2) - 1)
    def _():
        for h in range(hb):
            sfin_ref[0, h] = st_ref[h].T


def _hgrn(proj, lb_logits, gnorm, s0, layer, bsz, length, width, hb=8):
    heads = A_HEADS
    dk = width // heads
    if length >= HGRN_CHUNK:
        assert length % HGRN_CHUNK == 0
        rows, nc = HGRN_CHUNK, length // HGRN_CHUNK
    else:
        assert length % SUBLANES == 0
        rows, nc = length, 1
        hb = heads
    hb = min(hb, heads)
    assert dk == LANES and heads % hb == 0
    sec = heads // hb
    wb = hb * dk

    def col(section):
        return lambda b, h, c: (b * nc + c, section * sec + h)

    return pl.pallas_call(
        functools.partial(_hgrn_kernel, layer=layer, rows=rows, hb=hb),
        grid=(bsz, heads // hb, nc),
        in_specs=[
            pl.BlockSpec((rows, wb), col(0)),
            pl.BlockSpec((rows, wb), col(1)),
            pl.BlockSpec((rows, wb), col(2)),
            pl.BlockSpec((rows, wb), col(3)),
            pl.BlockSpec((lb_logits.shape[0], wb), lambda b, h, c: (0, h)),
            pl.BlockSpec((1, wb), lambda b, h, c: (0, h)),
            pl.BlockSpec((1, hb, dk, dk), lambda b, h, c: (b, h, 0, 0)),
        ],
        out_specs=[
            pl.BlockSpec((rows, wb), lambda b, h, c: (b * nc + c, h)),
            pl.BlockSpec((1, hb, dk, dk), lambda b, h, c: (b, h, 0, 0)),
        ],
        out_shape=[
            jax.ShapeDtypeStruct((bsz * length, width), _BF16 if rows % (2 * SUBLANES) == 0 else _F32),
            jax.ShapeDtypeStruct((bsz, heads, dk, dk), _F32),
        ],
        scratch_shapes=[pltpu.VMEM((hb, dk, dk), _F32)],
        compiler_params=_params("parallel", "parallel", "arbitrary"),
        name="hgrn2",
    )(proj, proj, proj, proj, lb_logits.astype(_F32), gnorm.reshape(1, width).astype(_F32), s0.astype(_F32))


def _moba_prompt_block(q_ref, k_ref, v_ref, o_ref, *, cols, own, nblk):
    blk = MOBA_BLOCK
    hd = cols.stop - cols.start
    n_sel = min(MOBA_TOPK, nblk, own)
    q = q_ref[:, cols].astype(_BF16)
    k = k_ref[:(own + 1) * blk, cols]
    s = _dot_nt(q, k.astype(_BF16)) * (hd ** -0.5)

    sel = [None] * own
    if own > n_sel:
        k_mean = jnp.mean(k[:own * blk].reshape(own, blk, hd), axis=1)
        k_mean = jnp.concatenate([k_mean, jnp.zeros((LANES - own, hd), _F32)], axis=0)
        gate = _dot_nt(k_mean.astype(_BF16), q)
        nrow = -(-own // SUBLANES) * SUBLANES
        blk_id = lax.broadcasted_iota(jnp.int32, (nrow, blk), 0)
        g = jnp.where(blk_id < own, gate[:nrow], NEG_INF)
        rank = jnp.zeros((nrow, blk), jnp.int32)
        for jp in range(own):
            other = g[jp:jp + 1]
            ahead = (other > g) | ((other == g) & (blk_id > jp))
            rank = rank + ahead.astype(jnp.int32)
        picked = jnp.where((rank < n_sel) & (blk_id < own), 1.0, 0.0)
        picked = jnp.concatenate([picked, jnp.zeros((LANES - nrow, blk), _F32)], axis=0).T
        sel = [picked[:, j:j + 1] > 0.0 for j in range(own)]

    row = lax.broadcasted_iota(jnp.int32, (blk, blk), 0)
    col = lax.broadcasted_iota(jnp.int32, (blk, blk), 1)
    masked = []
    for j in range(own + 1):
        sj = s[:, j * blk:(j + 1) * blk]
        if j == own:
            sj = jnp.where(col <= row, sj, NEG_INF)
        elif sel[j] is not None:
            sj = jnp.where(sel[j], sj, NEG_INF)
        masked.append(sj)
    m = masked[0].max(axis=-1, keepdims=True)
    for sj in masked[1:]:
        m = jnp.maximum(m, sj.max(axis=-1, keepdims=True))
    l = jnp.zeros((blk, 1), _F32)
    acc = jnp.zeros((blk, hd), _F32)
    for j, sj in enumerate(masked):
        p = jnp.exp(sj - m)
        l = l + p.sum(axis=-1, keepdims=True)
        acc = acc + _dot(p.astype(_BF16), v_ref[j * blk:(j + 1) * blk, cols].astype(_BF16))
    o_ref[:, cols] = (acc / l).astype(o_ref.dtype)


def _moba_prompt_kernel(q_ref, k_ref, v_ref, o_ref, kout_ref, vout_ref, *, nblk, hp):
    i = pl.program_id(2)

    @pl.when(i == 0)
    def _():
        kout_ref[...] = k_ref[...]
        vout_ref[...] = v_ref[...]

    def blocks(own):
        for h in range(hp):
            _moba_prompt_block(q_ref, k_ref, v_ref, o_ref, cols=slice(h * LANES, (h + 1) * LANES), own=own, nblk=nblk)

    for own in range(nblk):
        pl.when(i == own)(functools.partial(blocks, own))


def _moba_prompt(proj, bsz, length, q_col, k_col, v_col, width, hp=4):
    heads = B_HEADS
    hd = width // heads
    hp = min(hp, heads)
    assert hd == LANES and length % MOBA_BLOCK == 0 and heads % hp == 0
    nblk = length // MOBA_BLOCK
    assert nblk <= LANES
    wb = hp * hd
    kv_spec = pl.BlockSpec((length, wb), lambda b, h, i: (b, h))
    kv_shape = jax.ShapeDtypeStruct((bsz * length, width), _F32)
    return pl.pallas_call(
        functools.partial(_moba_prompt_kernel, nblk=nblk, hp=hp),
        grid=(bsz, heads // hp, nblk),
        in_specs=[
            pl.BlockSpec((MOBA_BLOCK, wb), lambda b, h, i: (b * nblk + i, q_col // wb + h)),
            pl.BlockSpec((length, wb), lambda b, h, i: (b, k_col // wb + h)),
            pl.BlockSpec((length, wb), lambda b, h, i: (b, v_col // wb + h)),
        ],
        out_specs=[pl.BlockSpec((MOBA_BLOCK, wb), lambda b, h, i: (b * nblk + i, h)), kv_spec, kv_spec],
        out_shape=[jax.ShapeDtypeStruct((bsz * length, width), _BF16), kv_shape, kv_shape],
        compiler_params=_params("parallel", "parallel", "arbitrary"),
        name="moba_prompt",
    )(proj, proj, proj)


def _moba_sample_kernel(pt_ref, q_ref, kn_ref, vn_ref, *refs, npages, heads, lq, group):
    kc_refs, vc_refs = refs[:group], refs[group:2 * group]
    o_ref, qbd_ref, ks_ref, km_ref, m_ref, l_ref, acc_ref = refs[2 * group:]
    p = pl.program_id(1)
    nsteps = npages // group
    hd = kc_refs[0].shape[2]
    page = kc_refs[0].shape[1] // heads
    width = heads * hd
    rows = heads * lq
    blk_pages = MOBA_BLOCK // page
    nb = npages // blk_pages
    scale = hd ** -0.5

    def by_token(ref):
        return jnp.concatenate([ref[0, pl.ds(h, page, stride=heads), :] for h in range(heads)], axis=1)

    def own_head(x):
        return jnp.concatenate([x[h * lq:(h + 1) * lq, h * hd:(h + 1) * hd] for h in range(heads)], axis=0)

    @pl.when(p == 0)
    def _():
        qt = jnp.concatenate([q_ref[...]] * heads, axis=0)
        rh = _div(lax.broadcasted_iota(jnp.int32, (rows, width), 0), lq)
        ch = _div(lax.broadcasted_iota(jnp.int32, (rows, width), 1), hd)
        qbd_ref[...] = jnp.where(rh == ch, qt, 0.0).astype(_BF16)

    qbd = qbd_ref[...]
    for g0 in range(0, group, blk_pages):
        blk = (p * group + g0) // blk_pages
        s = [_dot_nt(qbd, by_token(kc_refs[g0 + g]).astype(_BF16)) * scale for g in range(blk_pages)]
        m = s[0].max(axis=1, keepdims=True)
        for sg in s[1:]:
            m = jnp.maximum(m, sg.max(axis=1, keepdims=True))
        l = jnp.zeros((rows, 1), _F32)
        acc = jnp.zeros((rows, width), _F32)
        for g in range(blk_pages):
            w = jnp.exp(s[g] - m)
            l = l + w.sum(axis=1, keepdims=True)
            acc = acc + _dot(w.astype(_BF16), by_token(vc_refs[g0 + g]).astype(_BF16))
            ks_ref[p * group + g0 + g] = kc_refs[g0 + g][0].reshape(page, heads, hd).sum(axis=0)
        m_ref[blk] = m
        l_ref[blk] = l
        acc_ref[blk] = own_head(acc)

    @pl.when(p == nsteps - 1)
    def _():
        k_sum = ks_ref[...].reshape(nb, blk_pages, heads, hd).sum(axis=1)
        km_ref[...] = k_sum.reshape(nb * heads, hd) * (1.0 / MOBA_BLOCK)
        k_mean = jnp.concatenate([km_ref[pl.ds(h, nb, stride=heads), :] for h in range(heads)], axis=1)
        gate = _dot_nt(qbd, k_mean.astype(_BF16))
        lane = lax.broadcasted_iota(jnp.int32, (rows, nb), 1)
        sel = jnp.zeros((rows, nb), _F32)
        for _ in range(MOBA_TOPK):
            best = jnp.max(gate, axis=1, keepdims=True)
            idx = jnp.min(jnp.where(gate == best, lane, nb), axis=1, keepdims=True)
            hit = lane == idx
            sel = jnp.where(hit, 1.0, sel)
            gate = jnp.where(hit, -jnp.inf, gate)

        pad = 2 * SUBLANES - lq
        kn = jnp.concatenate([kn_ref[...], jnp.zeros((pad, width), _F32)], axis=0).astype(_BF16)
        vn = jnp.concatenate([vn_ref[...], jnp.zeros((pad, width), _F32)], axis=0).astype(_BF16)
        s_own = _dot_nt(qbd, kn) * scale
        qi = _mod(lax.broadcasted_iota(jnp.int32, s_own.shape, 0), lq)
        kj = lax.broadcasted_iota(jnp.int32, s_own.shape, 1)
        s_own = jnp.where((kj <= qi) & (kj < lq), s_own, NEG_INF)

        picked = [sel[:, b:b + 1] > 0.0 for b in range(nb)]
        top = jnp.max(s_own, axis=1, keepdims=True)
        for b in range(nb):
            top = jnp.maximum(top, jnp.where(picked[b], m_ref[b], NEG_INF))
        p_own = jnp.exp(s_own - top)
        l = jnp.sum(p_own, axis=1, keepdims=True)
        out = own_head(_dot(p_own.astype(_BF16), vn))
        for b in range(nb):
            c = jnp.where(picked[b], jnp.exp(jnp.minimum(m_ref[b] - top, 0.0)), 0.0)
            l = l + c * l_ref[b]
            out = out + c * acc_ref[b]
        out = out / l
        for h in range(heads):
            o_ref[:, h * hd:(h + 1) * hd] = out[h * lq:(h + 1) * lq, :]


def _moba_sample(proj, cache_k, cache_v, page_table, bsz, lq, q_col, k_col, v_col, width):
    heads = B_HEADS
    hd = width // heads
    n_phys, page = cache_k.shape[0], cache_k.shape[1]
    npages = page_table.shape[1]
    past = npages * page
    assert MOBA_BLOCK % page == 0 and past % MOBA_BLOCK == 0 and lq % SUBLANES == 0 and lq <= 2 * SUBLANES
    assert past // MOBA_BLOCK >= MOBA_TOPK and heads % SUBLANES == 0
    blk_pages = MOBA_BLOCK // page
    nb = past // MOBA_BLOCK
    group = blk_pages * max(g for g in (1, 2, 4, 8) if nb % g == 0)
    nsteps = npages // group
    rows = heads * lq
    kc = cache_k.reshape(n_phys, page * heads, hd)
    vc = cache_v.reshape(n_phys, page * heads, hd)

    def nth_page(g):
        return lambda b, p, pt: (pt[b, p * group + g], 0, 0)

    page_specs = [pl.BlockSpec((1, page * heads, hd), nth_page(g)) for g in range(group)] * 2
    grid_spec = pltpu.PrefetchScalarGridSpec(
        num_scalar_prefetch=1,
        grid=(bsz, nsteps),
        in_specs=[
            pl.BlockSpec((lq, width), lambda b, p, pt: (b, q_col // width)),
            pl.BlockSpec((lq, width), lambda b, p, pt: (b, k_col // width)),
            pl.BlockSpec((lq, width), lambda b, p, pt: (b, v_col // width)),
        ] + page_specs,
        out_specs=pl.BlockSpec((lq, width), lambda b, p, pt: (b, 0)),
        scratch_shapes=[
            pltpu.VMEM((rows, width), _BF16),
            pltpu.VMEM((npages, heads, hd), _F32),
            pltpu.VMEM((nb * heads, hd), _F32),
            pltpu.VMEM((nb, rows, 1), _F32),
            pltpu.VMEM((nb, rows, 1), _F32),
            pltpu.VMEM((nb, rows, hd), _F32),
        ],
    )
    return pl.pallas_call(
        functools.partial(_moba_sample_kernel, npages=npages, heads=heads, lq=lq, group=group),
        grid_spec=grid_spec,
        out_shape=jax.ShapeDtypeStruct((bsz * lq, width), _F32),
        compiler_params=_params("parallel", "arbitrary"),
        name="moba_sample",
    )(page_table.astype(jnp.int32), proj, proj, proj, *([kc] * group), *([vc] * group))


def _top_rows(x, k):
    r, t = x.shape
    rowi = lax.broadcasted_iota(jnp.int32, (r, t), 0).astype(_F32)
    slot = lax.broadcasted_iota(jnp.int32, (k, t), 0)
    vals = jnp.zeros((k, t), _F32)
    idxs = jnp.zeros((k, t), _F32)
    for n in range(k):
        best = jnp.max(x, axis=0, keepdims=True)
        idx = jnp.min(jnp.where(x == best, rowi, float(r)), axis=0, keepdims=True)
        vals = jnp.where(slot == n, best, vals)
        idxs = jnp.where(slot == n, idx, idxs)
        x = jnp.where(rowi == idx, -jnp.inf, x)
    return vals, idxs


def _candidate_pieces(topk):
    pieces, start = [], 0
    a = 0
    while topk // (a + 1) > 1:
        nb = topk // (a + 1)
        rows = -(-nb // SUBLANES) * SUBLANES
        pieces.append((start, rows, a, nb))
        start += rows
        a += 1
    return pieces, (start, a)


def _peer_retrieve_kernel(x_ref, gain_ref, wq_ref, sk_ref, *refs):
    if len(refs) > 5:
        u_ref, v_ref, g_ref, i1_ref, i2_ref, xn_ref, ub_ref, vb_ref, q_ref = refs
        ub_ref[...] = u_ref[...].astype(_BF16)
        vb_ref[...] = v_ref[...].astype(_BF16)
    else:
        g_ref, i1_ref, i2_ref, xn_ref, q_ref = refs
    h = pl.program_id(1)
    heads = wq_ref.shape[0]

    @pl.when(h == 0)
    def _():
        x = x_ref[...]
        xn = (x * lax.rsqrt(jnp.mean(x * x, axis=-1, keepdims=True) + RMS_EPS) * gain_ref[...]).astype(_BF16)
        xn_ref[...] = xn
        for hh in range(heads):
            q_ref[hh] = _dot(xn, wq_ref[hh]).astype(_BF16)

    q = q_ref[h]

    topk = PEER_TOPK
    dh = sk_ref.shape[-1]
    tm = x_ref.shape[0]
    tops = []
    for a in range(2):
        qa = q[:, a * dh:(a + 1) * dh]
        tops.append(_top_rows(_dot_nt(sk_ref[0, a], qa), topk))
    (s1, k1), (s2, k2) = tops

    pieces, (tail_start, tail_a) = _candidate_pieces(topk)
    parts = []
    for _, rows, a, nb in pieces:
        part = s1[a:a + 1] + s2[:rows]
        if nb < rows:
            part = jnp.where(lax.broadcasted_iota(jnp.int32, (rows, tm), 0) < nb, part, -jnp.inf)
        parts.append(part)
    tail = s1[tail_a:] + s2[0:1]
    tail_rows = -(-(topk - tail_a) // SUBLANES) * SUBLANES
    if tail_rows > topk - tail_a:
        tail = jnp.concatenate([tail, jnp.full((tail_rows - (topk - tail_a), tm), -jnp.inf, _F32)], axis=0)
    cand = jnp.concatenate(parts + [tail], axis=0)
    best_s, pos = _top_rows(cand, topk)

    a_sel = pos - float(tail_start - tail_a)
    b_sel = jnp.zeros_like(pos)
    for start, rows, a, _ in pieces:
        inside = (pos >= float(start)) & (pos < float(start + rows))
        a_sel = jnp.where(inside, float(a), a_sel)
        b_sel = jnp.where(inside, pos - float(start), b_sel)
    i1 = jnp.zeros_like(pos)
    i2 = jnp.zeros_like(pos)
    for a in range(topk):
        i1 = jnp.where(a_sel == float(a), k1[a:a + 1], i1)
        i2 = jnp.where(b_sel == float(a), k2[a:a + 1], i2)
    e = jnp.exp(best_s - best_s[0:1])
    g_ref[...] = e / jnp.sum(e, axis=0, keepdims=True)
    i1_ref[...] = i1
    i2_ref[...] = i2


def _peer_retrieve(x, gain, w_query, sub_keys, tables=None, tm=512):
    t, d_model = x.shape
    heads, _, nkeys, dh = sub_keys.shape
    tm = min(tm, t)
    steps = (t // tm) * heads
    spec = pl.BlockSpec((PEER_TOPK, tm), lambda i, h: (h, i))
    shape = jax.ShapeDtypeStruct((heads * PEER_TOPK, t), _F32)
    wq = w_query.reshape(d_model, heads, 2 * dh).transpose(1, 0, 2)
    in_specs = [
        pl.BlockSpec((tm, d_model), lambda i, h: (i, 0)),
        pl.BlockSpec((1, d_model), lambda i, h: (0, 0)),
        pl.BlockSpec((heads, d_model, 2 * dh), lambda i, h: (0, 0, 0)),
        pl.BlockSpec((1, 2, nkeys, dh), lambda i, h: (h, 0, 0, 0)),
    ]
    out_specs = [spec, spec, spec, pl.BlockSpec((tm, d_model), lambda i, h: (i, 0))]
    out_shape = [shape, shape, shape, jax.ShapeDtypeStruct((t, d_model), _BF16)]
    operands = [x, gain.reshape(1, d_model).astype(_F32), wq, sub_keys.astype(_BF16)]
    if tables is not None:
        n_exp, d = tables[0].shape
        slab = n_exp // steps
        assert slab * steps == n_exp and slab % (2 * SUBLANES) == 0, (n_exp, steps)
        slab_spec = pl.BlockSpec((slab, d), lambda i, h: (i * heads + h, 0))
        in_specs += [slab_spec, slab_spec]
        out_specs += [slab_spec, slab_spec]
        out_shape += [jax.ShapeDtypeStruct((n_exp, d), _BF16)] * 2
        operands += list(tables)
    return pl.pallas_call(
        _peer_retrieve_kernel,
        grid=(t // tm, heads),
        in_specs=in_specs,
        out_specs=out_specs,
        out_shape=out_shape,
        scratch_shapes=[pltpu.VMEM((heads, tm, 2 * dh), _BF16)],
        compiler_params=_params("parallel", "arbitrary"),
        name="peer_retrieve",
    )(*operands)


def _peer_weights_kernel(g_ref, i1_ref, i2_ref, w_ref, gt_ref, i1t_ref, i2t_ref, scr_ref, *, tm, stride):
    nk = LANES
    gt_ref[...] = g_ref[...].T
    i1t_ref[...] = i1_ref[...].T
    i2t_ref[...] = i2_ref[...].T
    sub = lax.broadcasted_iota(jnp.int32, (nk, nk), 0).astype(_F32)

    zeros = jnp.zeros((nk, nk), _F32)

    def one_hots(n):
        r1 = jnp.broadcast_to(i1t_ref[pl.ds(n, 1), :], (nk, nk))
        r2 = jnp.broadcast_to(i2t_ref[pl.ds(n, 1), :], (nk, nk))
        rg = jnp.broadcast_to(gt_ref[pl.ds(n, 1), :], (nk, nk))
        return jnp.where(sub == r1, 1.0, 0.0), jnp.where(sub == r2, rg, 0.0)

    def body(t, carry):
        n = 2 * t
        a1, a2 = one_hots(n)
        b1, b2 = one_hots(n + 1)
        m1 = jnp.concatenate([a1, b1], axis=1).astype(_BF16)
        m2 = jnp.concatenate([jnp.concatenate([a2, zeros], axis=1),
                              jnp.concatenate([zeros, b2], axis=1)], axis=0).astype(_BF16)
        planes = _dot_nt(m1, m2)
        scr_ref[pl.ds(n, nk, stride=stride), :] = planes[:, :nk]
        scr_ref[pl.ds(n + 1, nk, stride=stride), :] = planes[:, nk:]
        return carry

    lax.fori_loop(0, tm // 2, body, 0, unroll=4 * SUBLANES)
    for j in range(nk):
        w_ref[:, j * nk:(j + 1) * nk] = scr_ref[j * stride:j * stride + tm, :].astype(w_ref.dtype)


def _peer_weights(g, i1, i2, nkeys, tm=256):
    picks, t = g.shape
    assert picks == LANES and nkeys == LANES
    tm = min(tm, t)
    stride = tm + SUBLANES
    spec = pl.BlockSpec((picks, tm), lambda i: (0, i))
    return pl.pallas_call(
        functools.partial(_peer_weights_kernel, tm=tm, stride=stride),
        grid=(t // tm,),
        in_specs=[spec, spec, spec],
        out_specs=pl.BlockSpec((tm, nkeys * nkeys), lambda i: (i, 0)),
        out_shape=jax.ShapeDtypeStruct((t, nkeys * nkeys), _BF16),
        scratch_shapes=[
            pltpu.VMEM((tm, picks), _F32),
            pltpu.VMEM((tm, picks), _F32),
            pltpu.VMEM((tm, picks), _F32),
            pltpu.VMEM((nkeys * stride, nkeys), _F32),
        ],
        compiler_params=_params("parallel"),
        name="peer_weights",
    )(g, i1, i2)


def _peer_ffn_kernel(xn_ref, w_ref, u_ref, v_ref, o_ref, *, tn):
    j = pl.program_id(1)

    @pl.when(j == 0)
    def _():
        o_ref[...] = jnp.zeros_like(o_ref)

    xn = xn_ref[...]
    parts = []
    for e in range(0, u_ref.shape[0], tn):
        h = _dot_nt(xn, u_ref[e:e + tn, :])
        w = w_ref[:, e:e + tn].astype(_F32)
        parts.append(jnp.where(w != 0.0, w * jax.nn.gelu(h), 0.0).astype(_BF16))
    coef = jnp.concatenate(parts, axis=1)
    for n in range(0, o_ref.shape[1], tn):
        o_ref[:, n:n + tn] += _dot(coef, v_ref[:, n:n + tn])


def _peer_ffn(xn, w, u, v, tm=1024, te=1024):
    t, d = xn.shape
    n_exp = u.shape[0]
    tm = min(tm, t)
    tn = _tile_n(d)
    blocks = 2 * (2 * (tm * d + tm * te + 2 * te * d) + 4 * tm * d)
    temps = tm * te * 2 + 4 * tm * tn * 4
    return pl.pallas_call(
        functools.partial(_peer_ffn_kernel, tn=tn),
        grid=(t // tm, n_exp // te),
        in_specs=[
            pl.BlockSpec((tm, d), lambda i, j: (i, 0)),
            pl.BlockSpec((tm, te), lambda i, j: (i, j)),
            pl.BlockSpec((te, d), lambda i, j: (j, 0)),
            pl.BlockSpec((te, d), lambda i, j: (j, 0)),
        ],
        out_specs=pl.BlockSpec((tm, d), lambda i, j: (i, 0)),
        out_shape=jax.ShapeDtypeStruct((t, d), _F32),
        compiler_params=_params("parallel", "arbitrary", vmem=max(VMEM_LIMIT, blocks + temps)),
        name="peer_ffn",
    )(xn, w, u, v)


def _residual_norm_kernel(x_ref, y_ref, g_ref, o_ref, *, final):
    x = x_ref[...] + y_ref[...]
    if final:
        x = x * lax.rsqrt(jnp.mean(x * x, axis=-1, keepdims=True) + RMS_EPS) * g_ref[...]
    o_ref[...] = x


def _residual_norm(x, y, gain, final):
    t, d = x.shape
    tm = min(t, 512)
    spec = pl.BlockSpec((tm, d), lambda i: (i, 0))
    return pl.pallas_call(
        functools.partial(_residual_norm_kernel, final=final),
        grid=(t // tm,),
        in_specs=[spec, spec, pl.BlockSpec((1, d), lambda i: (0, 0))],
        out_specs=spec,
        out_shape=jax.ShapeDtypeStruct((t, d), _F32),
        compiler_params=_params("parallel"),
        name="residual_norm",
    )(x, y, gain.reshape(1, d).astype(_F32))


def _layer(x, bsz, length, s0, past, lw, layer, final):
    d_model = x.shape[1]
    a_width = lw["gnorm"].shape[-1]
    b_width = lw["w_b"].shape[0]
    proj = _norm_matmul(x, lw["norm_mix"], lw["w_in"])
    b_q = 4 * a_width
    b_k, b_v = b_q + b_width, b_q + 2 * b_width
    g_a = b_q + 3 * b_width
    g_b = g_a + d_model

    o_a, s_new = _hgrn(proj, lw["lb_logits"], lw["gnorm"], s0, layer, bsz, length, a_width)
    if past is None:
        o_b, k_new, v_new = _moba_prompt(proj, bsz, length, b_q, b_k, b_v, b_width)
    else:
        o_b = _moba_sample(proj, past[0], past[1], past[2], bsz, length, b_q, b_k, b_v, b_width)
        k_new, v_new = proj[:, b_k:b_k + b_width], proj[:, b_v:b_v + b_width]
    merged = _merge(o_a, o_b, lw["w_a"], lw["w_b"], proj, g_a, g_b, d_model)
    x = _matmul_residual(merged, lw["w_out"], x)

    if "peer_u_bf16" in lw:
        gates, i1, i2, xn = _peer_retrieve(x, lw["norm_ffn"], lw["w_query"], lw["sub_keys"])
    else:
        gates, i1, i2, xn, lw["peer_u_bf16"], lw["peer_v_bf16"] = _peer_retrieve(
            x, lw["norm_ffn"], lw["w_query"], lw["sub_keys"], tables=(lw["peer_u"], lw["peer_v"]))
    w = _peer_weights(gates, i1, i2, lw["sub_keys"].shape[2])
    x = _residual_norm(x, _peer_ffn(xn, w, lw["peer_u_bf16"], lw["peer_v_bf16"]), lw["norm_final"], final)

    kv_shape = (bsz, length, B_HEADS, b_width // B_HEADS)
    return x, s_new, k_new.reshape(kv_shape), v_new.reshape(kv_shape)


def kernel(x_prompt, x_sample, cache_k, cache_v, state_hgrn, page_table, norm_mix, w_in, hgrn_lb_logits,
           hgrn_gnorm, w_a_proj, w_b_proj, w_out, norm_ffn, peer_w_query, peer_sub_keys, peer_u, peer_v,
           norm_final):
    depth = w_in.shape[0]
    bp, lp, d_model = x_prompt.shape
    bs, ls, _ = x_sample.shape
    x_p = x_prompt.reshape(bp * lp, d_model)
    x_s = x_sample.reshape(bs * ls, d_model)
    outs = [[] for _ in range(6)]
    for layer in range(depth):
        lw = {
            "norm_mix": norm_mix[layer], "w_in": w_in[layer].astype(_BF16), "lb_logits": hgrn_lb_logits,
            "gnorm": hgrn_gnorm[layer], "w_a": w_a_proj[layer].astype(_BF16), "w_b": w_b_proj[layer].astype(_BF16),
            "w_out": w_out[layer].astype(_BF16), "norm_ffn": norm_ffn[layer],
            "w_query": peer_w_query[layer].astype(_BF16), "sub_keys": peer_sub_keys[layer],
            "peer_u": peer_u[layer], "peer_v": peer_v[layer], "norm_final": norm_final,
        }
        final = layer == depth - 1
        s0_p = jnp.zeros((bp,) + state_hgrn.shape[2:], _F32)
        x_p, s_p, k_p, v_p = _layer(x_p, bp, lp, s0_p, None, lw, layer, final)
        past = (cache_k[layer], cache_v[layer], page_table)
        x_s, s_s, k_s, v_s = _layer(x_s, bs, ls, state_hgrn[layer], past, lw, layer, final)
        for lst, val in zip(outs, (s_p, s_s, k_p, v_p, k_s, v_s)):
            lst.append(val)
    sp, ss, kp, vp, ks, vs = (jnp.stack(lst, axis=0) for lst in outs)
    return (x_p.reshape(bp, lp, d_model), x_s.reshape(bs, ls, d_model),
            sp.astype(state_hgrn.dtype), ss.astype(state_hgrn.dtype),
            kp.astype(cache_k.dtype), vp.astype(cache_v.dtype), ks.astype(cache_k.dtype), vs.astype(cache_v.dtype))
```

```python
import functools

import jax
import jax.numpy as jnp
from jax import lax
from jax.experimental import pallas as pl
from jax.experimental.pallas import tpu as pltpu

A_HEADS = 8
B_HEADS = 8
MOBA_BLOCK = 256
MOBA_TOPK = 3
PEER_HEADS = 8
PEER_TOPK = 16
RMS_EPS = 1e-6
NEG_INF = -1e30

LANES = 128
SUBLANES = 8
HGRN_CHUNK = 128
HGRN_SUB = 16
HGRN_SAFE_SPAN = 40.0
VMEM_LIMIT = 48 * 1024 * 1024

_F32 = jnp.float32
_BF16 = jnp.bfloat16
_NT = (((1,), (1,)), ((), ()))


def _params(*sem, vmem=VMEM_LIMIT):
    return pltpu.CompilerParams(dimension_semantics=sem, vmem_limit_bytes=vmem)


def _dot_nt(a, b):
    return lax.dot_general(a, b, _NT, preferred_element_type=_F32)


def _dot(a, b):
    return jnp.dot(a, b, preferred_element_type=_F32)


def _log2(n):
    assert n > 0 and n & (n - 1) == 0, n
    return n.bit_length() - 1


def _div(x, n):
    return lax.shift_right_logical(x, jnp.int32(_log2(n)))


def _mod(x, n):
    return lax.bitwise_and(x, jnp.int32((1 << _log2(n)) - 1))


def _tile_n(n, *cols, cap=512):
    for tn in (1024, 512, 256, LANES):
        if tn <= cap and all(v % tn == 0 for v in (n,) + cols):
            return tn
    raise ValueError((n, cols))


def _norm_matmul_kernel(x_ref, g_ref, b_ref, o_ref, xn_ref):
    @pl.when(pl.program_id(1) == 0)
    def _():
        x = x_ref[...]
        xn_ref[...] = (x * lax.rsqrt(jnp.mean(x * x, axis=-1, keepdims=True) + RMS_EPS) * g_ref[...]).astype(_BF16)

    o_ref[...] = _dot(xn_ref[...], b_ref[...])


def _norm_matmul(x, gain, b):
    m, k = x.shape
    n = b.shape[1]
    tm, tn = min(m, 1024), _tile_n(n, cap=1024)
    return pl.pallas_call(
        _norm_matmul_kernel,
        grid=(m // tm, n // tn),
        in_specs=[
            pl.BlockSpec((tm, k), lambda i, j: (i, 0)),
            pl.BlockSpec((1, k), lambda i, j: (0, 0)),
            pl.BlockSpec((k, tn), lambda i, j: (0, j)),
        ],
        out_specs=pl.BlockSpec((tm, tn), lambda i, j: (i, j)),
        out_shape=jax.ShapeDtypeStruct((m, n), _F32),
        scratch_shapes=[pltpu.VMEM((tm, k), _BF16)],
        compiler_params=_params("parallel", "arbitrary"),
        name="norm_matmul",
    )(x, gain.reshape(1, k).astype(_F32), b)


def _merge_kernel(oa_ref, ob_ref, wa_ref, wb_ref, ga_ref, gb_ref, o_ref):
    ya = _dot(oa_ref[...].astype(_BF16), wa_ref[...])
    yb = _dot(ob_ref[...].astype(_BF16), wb_ref[...])
    o_ref[...] = (jax.nn.sigmoid(ga_ref[...]) * ya + jax.nn.sigmoid(gb_ref[...]) * yb).astype(o_ref.dtype)


def _merge(oa, ob, wa, wb, proj, ga_col, gb_col, d_model):
    t, ka = oa.shape
    kb = ob.shape[1]
    tm = min(t, 1024)
    tn = _tile_n(d_model, ga_col, gb_col)
    return pl.pallas_call(
        _merge_kernel,
        grid=(t // tm, d_model // tn),
        in_specs=[
            pl.BlockSpec((tm, ka), lambda i, j: (i, 0)),
            pl.BlockSpec((tm, kb), lambda i, j: (i, 0)),
            pl.BlockSpec((ka, tn), lambda i, j: (0, j)),
            pl.BlockSpec((kb, tn), lambda i, j: (0, j)),
            pl.BlockSpec((tm, tn), lambda i, j: (i, ga_col // tn + j)),
            pl.BlockSpec((tm, tn), lambda i, j: (i, gb_col // tn + j)),
        ],
        out_specs=pl.BlockSpec((tm, tn), lambda i, j: (i, j)),
        out_shape=jax.ShapeDtypeStruct((t, d_model), _BF16),
        compiler_params=_params("parallel", "parallel"),
        name="merge",
    )(oa, ob, wa, wb, proj, proj)


def _mm_residual_kernel(a_ref, b_ref, r_ref, o_ref):
    o_ref[...] = r_ref[...] + _dot(a_ref[...], b_ref[...])


def _matmul_residual(a, b, res):
    m, k = a.shape
    n = b.shape[1]
    tm = min(m, 1024)
    tn = _tile_n(n)
    return pl.pallas_call(
        _mm_residual_kernel,
        grid=(m // tm, n // tn),
        in_specs=[
            pl.BlockSpec((tm, k), lambda i, j: (i, 0)),
            pl.BlockSpec((k, tn), lambda i, j: (0, j)),
            pl.BlockSpec((tm, tn), lambda i, j: (i, j)),
        ],
        out_specs=pl.BlockSpec((tm, tn), lambda i, j: (i, j)),
        out_shape=jax.ShapeDtypeStruct((m, n), _F32),
        compiler_params=_params("parallel", "parallel"),
        name="matmul_residual",
    )(a, b, res)


def _hgrn_inputs(q, fpre, v, logits, *, layer, rows):
    c_len = HGRN_CHUNK

    def pad(x):
        if rows == c_len:
            return x
        return jnp.concatenate([x, jnp.zeros((c_len - rows, x.shape[1]), x.dtype)], axis=0)

    ex = jnp.exp(logits - jnp.max(logits, axis=0, keepdims=True))
    lb = jnp.sum(ex[: layer + 1], axis=0, keepdims=True) / jnp.sum(ex, axis=0, keepdims=True)

    forget = lb + (1.0 - lb) * jax.nn.sigmoid(fpre)
    b = pad(jnp.log(forget))
    row = lax.broadcasted_iota(jnp.int32, (c_len, LANES), 0)
    shift = 1
    while shift < c_len:
        b = b + jnp.where(row >= shift, pltpu.roll(b, shift, 0), 0.0)
        shift *= 2
    return pad(q), pad(1.0 - forget), pad(v), b


def _hgrn_start(b, i):
    return b[i * HGRN_SUB - 1:i * HGRN_SUB] if i > 0 else jnp.zeros((1, LANES), _F32)


def _hgrn_att_pairwise(q, kk, b, n_sub):
    c_len, sub = HGRN_CHUNK, HGRN_SUB
    rowc = lax.broadcasted_iota(jnp.int32, (sub, LANES), 0)
    lane = lax.broadcasted_iota(jnp.int32, (sub, LANES), 1)
    att_rows = []
    for i in range(n_sub):
        lo = i * sub
        qi, bi, ki = q[lo:lo + sub], b[lo:lo + sub], kk[lo:lo + sub]
        att = jnp.zeros((sub, LANES), _F32)
        for s in range(sub):
            dec = jnp.exp(jnp.minimum(bi - bi[s:s + 1], 0.0))
            x = jnp.where(rowc >= s, qi * dec * ki[s:s + 1], 0.0)
            att = jnp.where(lane == lo + s, jnp.sum(x, axis=-1, keepdims=True), att)
        if i > 0:
            ref_b = _hgrn_start(b, i)
            qt = (qi * jnp.exp(bi - ref_b)).astype(_BF16)
            kp = kk[:lo] * jnp.exp(ref_b - b[:lo])
            kp = jnp.concatenate([kp, jnp.zeros((c_len - lo, LANES), _F32)], axis=0).astype(_BF16)
            att = att + _dot_nt(qt, kp)
        att_rows.append(att)
    if n_sub * sub < c_len:
        att_rows.append(jnp.zeros((c_len - n_sub * sub, LANES), _F32))
    return jnp.concatenate(att_rows, axis=0)


def _hgrn_att_factored(q, kk, b, n_sub):
    c_len, sub = HGRN_CHUNK, HGRN_SUB
    rowc = lax.broadcasted_iota(jnp.int32, (sub, LANES), 0)
    lane = lax.broadcasted_iota(jnp.int32, (sub, LANES), 1)
    att_rows = []
    for i in range(n_sub):
        lo, hi = i * sub, (i + 1) * sub
        ref_b = _hgrn_start(b, i)
        qt = (q[lo:hi] * jnp.exp(b[lo:hi] - ref_b)).astype(_BF16)
        kp = kk[:hi] * jnp.exp(ref_b - b[:hi])
        if hi < c_len:
            kp = jnp.concatenate([kp, jnp.zeros((c_len - hi, LANES), _F32)], axis=0)
        att_rows.append(jnp.where(lane <= rowc + lo, _dot_nt(qt, kp.astype(_BF16)), 0.0))
    if n_sub * sub < c_len:
        att_rows.append(jnp.zeros((c_len - n_sub * sub, LANES), _F32))
    return jnp.concatenate(att_rows, axis=0)


def _hgrn_outputs(q, kk, v, b, att, st, ag, gn, rows):
    c_len = HGRN_CHUNK
    o = _dot_nt((q * jnp.exp(b)).astype(_BF16), st.astype(_BF16))
    o = o + _dot(att.astype(_BF16), v.astype(_BF16))
    b_last = b[c_len - 1:c_len]
    k_hat = (kk * jnp.exp(b_last - b)).astype(_BF16)
    st_new = st * jnp.exp(b_last) + _dot(v.T.astype(_BF16), k_hat)
    oo = o[:rows]
    on = oo * lax.rsqrt(jnp.mean(oo * oo, axis=-1, keepdims=True) + RMS_EPS) * gn
    return on * (ag * jax.nn.sigmoid(ag)), st_new


def _hgrn_kernel(q_ref, f_ref, i_ref, g_ref, lbl_ref, gn_ref, s0_ref, o_ref, sfin_ref, st_ref, *,
                 layer, rows, hb):
    ci = pl.program_id(2)
    sub = HGRN_SUB
    n_sub = -(-rows // sub)

    @pl.when(ci == 0)
    def _():
        for h in range(hb):
            st_ref[h] = s0_ref[0, h].T

    cols = [slice(h * LANES, (h + 1) * LANES) for h in range(hb)]
    heads = [_hgrn_inputs(q_ref[:, c], f_ref[:, c], i_ref[:, c], lbl_ref[:, c], layer=layer, rows=rows)
             for c in cols]
    spans = [_hgrn_start(b, i) - b[(i + 1) * sub - 1:(i + 1) * sub] for _, _, _, b in heads for i in range(n_sub)]
    worst = jnp.max(jnp.concatenate(spans, axis=0))
    atts = lax.cond(worst < HGRN_SAFE_SPAN,
                    lambda: tuple(_hgrn_att_factored(q, kk, b, n_sub) for q, kk, _, b in heads),
                    lambda: tuple(_hgrn_att_pairwise(q, kk, b, n_sub) for q, kk, _, b in heads))
    for h, c in enumerate(cols):
        q, kk, v, b = heads[h]
        out, st_new = _hgrn_outputs(q, kk, v, b, atts[h], st_ref[h], g_ref[:, c], gn_ref[:, c], rows)
        o_ref[:, c] = out.astype(o_ref.dtype)
        st_ref[h] = st_new

    @pl.when(ci == pl.num_programs(2) - 1)
    def _():
        for h in range(hb):
            sfin_ref[0, h] = st_ref[h].T


def _hgrn(proj, lb_logits, gnorm, s0, layer, bsz, length, width, hb=8):
    heads = A_HEADS
    dk = width // heads
    if length >= HGRN_CHUNK:
        assert length % HGRN_CHUNK == 0
        rows, nc = HGRN_CHUNK, length // HGRN_CHUNK
    else:
        assert length % SUBLANES == 0
        rows, nc = length, 1
        hb = heads
    hb = min(hb, heads)
    assert dk == LANES and heads % hb == 0
    sec = heads // hb
    wb = hb * dk

    def col(section):
        return lambda b, h, c: (b * nc + c, section * sec + h)

    return pl.pallas_call(
        functools.partial(_hgrn_kernel, layer=layer, rows=rows, hb=hb),
        grid=(bsz, heads // hb, nc),
        in_specs=[
            pl.BlockSpec((rows, wb), col(0)),
            pl.BlockSpec((rows, wb), col(1)),
            pl.BlockSpec((rows, wb), col(2)),
            pl.BlockSpec((rows, wb), col(3)),
            pl.BlockSpec((lb_logits.shape[0], wb), lambda b, h, c: (0, h)),
            pl.BlockSpec((1, wb), lambda b, h, c: (0, h)),
            pl.BlockSpec((1, hb, dk, dk), lambda b, h, c: (b, h, 0, 0)),
        ],
        out_specs=[
            pl.BlockSpec((rows, wb), lambda b, h, c: (b * nc + c, h)),
            pl.BlockSpec((1, hb, dk, dk), lambda b, h, c: (b, h, 0, 0)),
        ],
        out_shape=[
            jax.ShapeDtypeStruct((bsz * length, width), _BF16 if rows % (2 * SUBLANES) == 0 else _F32),
            jax.ShapeDtypeStruct((bsz, heads, dk, dk), _F32),
        ],
        scratch_shapes=[pltpu.VMEM((hb, dk, dk), _F32)],
        compiler_params=_params("parallel", "parallel", "arbitrary"),
        name="hgrn2",
    )(proj, proj, proj, proj, lb_logits.astype(_F32), gnorm.reshape(1, width).astype(_F32), s0.astype(_F32))


def _moba_prompt_block(q_ref, k_ref, v_ref, o_ref, *, cols, own, nblk):
    blk = MOBA_BLOCK
    hd = cols.stop - cols.start
    n_sel = min(MOBA_TOPK, nblk, own)
    q = q_ref[:, cols].astype(_BF16)
    k = k_ref[:(own + 1) * blk, cols]
    s = _dot_nt(q, k.astype(_BF16)) * (hd ** -0.5)

    sel = [None] * own
    if own > n_sel:
        k_mean = jnp.mean(k[:own * blk].reshape(own, blk, hd), axis=1)
        k_mean = jnp.concatenate([k_mean, jnp.zeros((LANES - own, hd), _F32)], axis=0)
        gate = _dot_nt(k_mean.astype(_BF16), q)
        nrow = -(-own // SUBLANES) * SUBLANES
        blk_id = lax.broadcasted_iota(jnp.int32, (nrow, blk), 0)
        g = jnp.where(blk_id < own, gate[:nrow], NEG_INF)
        rank = jnp.zeros((nrow, blk), jnp.int32)
        for jp in range(own):
            other = g[jp:jp + 1]
            ahead = (other > g) | ((other == g) & (blk_id > jp))
            rank = rank + ahead.astype(jnp.int32)
        picked = jnp.where((rank < n_sel) & (blk_id < own), 1.0, 0.0)
        picked = jnp.concatenate([picked, jnp.zeros((LANES - nrow, blk), _F32)], axis=0).T
        sel = [picked[:, j:j + 1] > 0.0 for j in range(own)]

    row = lax.broadcasted_iota(jnp.int32, (blk, blk), 0)
    col = lax.broadcasted_iota(jnp.int32, (blk, blk), 1)
    masked = []
    for j in range(own + 1):
        sj = s[:, j * blk:(j + 1) * blk]
        if j == own:
            sj = jnp.where(col <= row, sj, NEG_INF)
        elif sel[j] is not None:
            sj = jnp.where(sel[j], sj, NEG_INF)
        masked.append(sj)
    m = masked[0].max(axis=-1, keepdims=True)
    for sj in masked[1:]:
        m = jnp.maximum(m, sj.max(axis=-1, keepdims=True))
    l = jnp.zeros((blk, 1), _F32)
    acc = jnp.zeros((blk, hd), _F32)
    for j, sj in enumerate(masked):
        p = jnp.exp(sj - m)
        l = l + p.sum(axis=-1, keepdims=True)
        acc = acc + _dot(p.astype(_BF16), v_ref[j * blk:(j + 1) * blk, cols].astype(_BF16))
    o_ref[:, cols] = (acc / l).astype(o_ref.dtype)


def _moba_prompt_kernel(q_ref, k_ref, v_ref, o_ref, kout_ref, vout_ref, *, nblk, hp):
    i = pl.program_id(2)

    @pl.when(i == 0)
    def _():
        kout_ref[...] = k_ref[...]
        vout_ref[...] = v_ref[...]

    def blocks(own):
        for h in range(hp):
            _moba_prompt_block(q_ref, k_ref, v_ref, o_ref, cols=slice(h * LANES, (h + 1) * LANES), own=own, nblk=nblk)

    for own in range(nblk):
        pl.when(i == own)(functools.partial(blocks, own))


def _moba_prompt(proj, bsz, length, q_col, k_col, v_col, width, hp=4):
    heads = B_HEADS
    hd = width // heads
    hp = min(hp, heads)
    assert hd == LANES and length % MOBA_BLOCK == 0 and heads % hp == 0
    nblk = length // MOBA_BLOCK
    assert nblk <= LANES
    wb = hp * hd
    kv_spec = pl.BlockSpec((length, wb), lambda b, h, i: (b, h))
    kv_shape = jax.ShapeDtypeStruct((bsz * length, width), _F32)
    return pl.pallas_call(
        functools.partial(_moba_prompt_kernel, nblk=nblk, hp=hp),
        grid=(bsz, heads // hp, nblk),
        in_specs=[
            pl.BlockSpec((MOBA_BLOCK, wb), lambda b, h, i: (b * nblk + i, q_col // wb + h)),
            pl.BlockSpec((length, wb), lambda b, h, i: (b, k_col // wb + h)),
            pl.BlockSpec((length, wb), lambda b, h, i: (b, v_col // wb + h)),
        ],
        out_specs=[pl.BlockSpec((MOBA_BLOCK, wb), lambda b, h, i: (b * nblk + i, h)), kv_spec, kv_spec],
        out_shape=[jax.ShapeDtypeStruct((bsz * length, width), _BF16), kv_shape, kv_shape],
        compiler_params=_params("parallel", "parallel", "arbitrary"),
        name="moba_prompt",
    )(proj, proj, proj)


def _moba_sample_kernel(pt_ref, q_ref, kn_ref, vn_ref, *refs, npages, heads, lq, group):
    kc_hbm, vc_hbm, o_ref, qbd_ref, ks_ref, km_ref, m_ref, l_ref, acc_ref, kbuf, vbuf, sem = refs
    b, p = pl.program_id(0), pl.program_id(1)
    nsteps = npages // group
    nbuf = kbuf.shape[0]
    hd = kbuf.shape[3]
    page = kbuf.shape[2] // heads
    width = heads * hd
    rows = heads * lq
    blk_pages = MOBA_BLOCK // page
    nb = npages // blk_pages
    scale = hd ** -0.5

    total = pl.num_programs(0) * nsteps
    this = b * nsteps + p

    def page_copies(step, slot):
        seq, first = step // nsteps, (step % nsteps) * group
        copies = []
        for j in range(group):
            page_id = pt_ref[seq, first + j]
            copies.append(pltpu.make_async_copy(kc_hbm.at[page_id], kbuf.at[slot, j], sem.at[slot]))
            copies.append(pltpu.make_async_copy(vc_hbm.at[page_id], vbuf.at[slot, j], sem.at[slot]))
        return copies

    @pl.when(this == 0)
    def _():
        for step in range(nbuf - 1):
            for copy in page_copies(step, step):
                copy.start()

    ahead = this + nbuf - 1

    @pl.when(ahead < total)
    def _():
        for copy in page_copies(ahead, ahead % nbuf):
            copy.start()

    slot = this % nbuf
    for copy in page_copies(this, slot):
        copy.wait()
    kc_refs = [kbuf.at[slot, j] for j in range(group)]
    vc_refs = [vbuf.at[slot, j] for j in range(group)]

    def by_token(ref):
        return jnp.concatenate([ref[pl.ds(h, page, stride=heads), :] for h in range(heads)], axis=1)

    def own_head(x):
        return jnp.concatenate([x[h * lq:(h + 1) * lq, h * hd:(h + 1) * hd] for h in range(heads)], axis=0)

    @pl.when(p == 0)
    def _():
        qt = jnp.concatenate([q_ref[...]] * heads, axis=0)
        rh = _div(lax.broadcasted_iota(jnp.int32, (rows, width), 0), lq)
        ch = _div(lax.broadcasted_iota(jnp.int32, (rows, width), 1), hd)
        qbd_ref[...] = jnp.where(rh == ch, qt, 0.0).astype(_BF16)

    qbd = qbd_ref[...]
    for g0 in range(0, group, blk_pages):
        blk = (p * group + g0) // blk_pages
        s = [_dot_nt(qbd, by_token(kc_refs[g0 + g]).astype(_BF16)) * scale for g in range(blk_pages)]
        m = s[0].max(axis=1, keepdims=True)
        for sg in s[1:]:
            m = jnp.maximum(m, sg.max(axis=1, keepdims=True))
        l = jnp.zeros((rows, 1), _F32)
        acc = jnp.zeros((rows, width), _F32)
        for g in range(blk_pages):
            w = jnp.exp(s[g] - m)
            l = l + w.sum(axis=1, keepdims=True)
            acc = acc + _dot(w.astype(_BF16), by_token(vc_refs[g0 + g]).astype(_BF16))
            ks_ref[p * group + g0 + g] = kc_refs[g0 + g][...].reshape(page, heads, hd).sum(axis=0)
        m_ref[blk] = m
        l_ref[blk] = l
        acc_ref[blk] = own_head(acc)

    @pl.when(p == nsteps - 1)
    def _():
        k_sum = ks_ref[...].reshape(nb, blk_pages, heads, hd).sum(axis=1)
        km_ref[...] = k_sum.reshape(nb * heads, hd) * (1.0 / MOBA_BLOCK)
        k_mean = jnp.concatenate([km_ref[pl.ds(h, nb, stride=heads), :] for h in range(heads)], axis=1)
        gate = _dot_nt(qbd, k_mean.astype(_BF16))
        lane = lax.broadcasted_iota(jnp.int32, (rows, nb), 1)
        sel = jnp.zeros((rows, nb), _F32)
        for _ in range(MOBA_TOPK):
            best = jnp.max(gate, axis=1, keepdims=True)
            idx = jnp.min(jnp.where(gate == best, lane, nb), axis=1, keepdims=True)
            hit = lane == idx
            sel = jnp.where(hit, 1.0, sel)
            gate = jnp.where(hit, -jnp.inf, gate)

        pad = 2 * SUBLANES - lq
        kn = jnp.concatenate([kn_ref[...], jnp.zeros((pad, width), _F32)], axis=0).astype(_BF16)
        vn = jnp.concatenate([vn_ref[...], jnp.zeros((pad, width), _F32)], axis=0).astype(_BF16)
        s_own = _dot_nt(qbd, kn) * scale
        qi = _mod(lax.broadcasted_iota(jnp.int32, s_own.shape, 0), lq)
        kj = lax.broadcasted_iota(jnp.int32, s_own.shape, 1)
        s_own = jnp.where((kj <= qi) & (kj < lq), s_own, NEG_INF)

        picked = [sel[:, b:b + 1] > 0.0 for b in range(nb)]
        top = jnp.max(s_own, axis=1, keepdims=True)
        for b in range(nb):
            top = jnp.maximum(top, jnp.where(picked[b], m_ref[b], NEG_INF))
        p_own = jnp.exp(s_own - top)
        l = jnp.sum(p_own, axis=1, keepdims=True)
        out = own_head(_dot(p_own.astype(_BF16), vn))
        for b in range(nb):
            c = jnp.where(picked[b], jnp.exp(jnp.minimum(m_ref[b] - top, 0.0)), 0.0)
            l = l + c * l_ref[b]
            out = out + c * acc_ref[b]
        out = out / l
        for h in range(heads):
            o_ref[:, h * hd:(h + 1) * hd] = out[h * lq:(h + 1) * lq, :]


def _moba_sample(proj, cache_k, cache_v, page_table, bsz, lq, q_col, k_col, v_col, width):
    heads = B_HEADS
    hd = width // heads
    n_phys, page = cache_k.shape[0], cache_k.shape[1]
    npages = page_table.shape[1]
    past = npages * page
    assert MOBA_BLOCK % page == 0 and past % MOBA_BLOCK == 0 and lq % SUBLANES == 0 and lq <= 2 * SUBLANES
    assert past // MOBA_BLOCK >= MOBA_TOPK and heads % SUBLANES == 0
    blk_pages = MOBA_BLOCK // page
    nb = past // MOBA_BLOCK
    group = blk_pages * max(g for g in (1, 2, 4) if nb % g == 0)
    nsteps = npages // group
    nbuf = 3
    assert bsz * nsteps >= nbuf - 1
    rows = heads * lq
    kc = cache_k.reshape(n_phys, page * heads, hd)
    vc = cache_v.reshape(n_phys, page * heads, hd)
    grid_spec = pltpu.PrefetchScalarGridSpec(
        num_scalar_prefetch=1,
        grid=(bsz, nsteps),
        in_specs=[
            pl.BlockSpec((lq, width), lambda b, p, pt: (b, q_col // width)),
            pl.BlockSpec((lq, width), lambda b, p, pt: (b, k_col // width)),
            pl.BlockSpec((lq, width), lambda b, p, pt: (b, v_col // width)),
            pl.BlockSpec(memory_space=pl.ANY),
            pl.BlockSpec(memory_space=pl.ANY),
        ],
        out_specs=pl.BlockSpec((lq, width), lambda b, p, pt: (b, 0)),
        scratch_shapes=[
            pltpu.VMEM((rows, width), _BF16),
            pltpu.VMEM((npages, heads, hd), _F32),
            pltpu.VMEM((nb * heads, hd), _F32),
            pltpu.VMEM((nb, rows, 1), _F32),
            pltpu.VMEM((nb, rows, 1), _F32),
            pltpu.VMEM((nb, rows, hd), _F32),
            pltpu.VMEM((nbuf, group, page * heads, hd), _F32),
            pltpu.VMEM((nbuf, group, page * heads, hd), _F32),
            pltpu.SemaphoreType.DMA((nbuf,)),
        ],
    )
    return pl.pallas_call(
        functools.partial(_moba_sample_kernel, npages=npages, heads=heads, lq=lq, group=group),
        grid_spec=grid_spec,
        out_shape=jax.ShapeDtypeStruct((bsz * lq, width), _F32),
        compiler_params=_params("arbitrary", "arbitrary"),
        name="moba_sample",
    )(page_table.astype(jnp.int32), proj, proj, proj, kc, vc)


def _top_rows(x, k):
    r, t = x.shape
    rowi = lax.broadcasted_iota(jnp.int32, (r, t), 0).astype(_F32)
    slot = lax.broadcasted_iota(jnp.int32, (k, t), 0)
    vals = jnp.zeros((k, t), _F32)
    idxs = jnp.zeros((k, t), _F32)
    for n in range(k):
        best = jnp.max(x, axis=0, keepdims=True)
        idx = jnp.min(jnp.where(x == best, rowi, float(r)), axis=0, keepdims=True)
        vals = jnp.where(slot == n, best, vals)
        idxs = jnp.where(slot == n, idx, idxs)
        x = jnp.where(rowi == idx, -jnp.inf, x)
    return vals, idxs


def _candidate_pieces(topk):
    pieces, start = [], 0
    a = 0
    while topk // (a + 1) > 1:
        nb = topk // (a + 1)
        rows = -(-nb // SUBLANES) * SUBLANES
        pieces.append((start, rows, a, nb))
        start += rows
        a += 1
    return pieces, (start, a)


def _peer_retrieve_kernel(x_ref, gain_ref, wq_ref, sk_ref, *refs):
    if len(refs) > 5:
        u_ref, v_ref, g_ref, i1_ref, i2_ref, xn_ref, ub_ref, vb_ref, q_ref = refs
        ub_ref[...] = u_ref[...].astype(_BF16)
        vb_ref[...] = v_ref[...].astype(_BF16)
    else:
        g_ref, i1_ref, i2_ref, xn_ref, q_ref = refs
    h = pl.program_id(1)
    heads = wq_ref.shape[0]

    @pl.when(h == 0)
    def _():
        x = x_ref[...]
        xn = (x * lax.rsqrt(jnp.mean(x * x, axis=-1, keepdims=True) + RMS_EPS) * gain_ref[...]).astype(_BF16)
        xn_ref[...] = xn
        for hh in range(heads):
            q_ref[hh] = _dot(xn, wq_ref[hh]).astype(_BF16)

    q = q_ref[h]

    topk = PEER_TOPK
    dh = sk_ref.shape[-1]
    tm = x_ref.shape[0]
    tops = []
    for a in range(2):
        qa = q[:, a * dh:(a + 1) * dh]
        tops.append(_top_rows(_dot_nt(sk_ref[0, a], qa), topk))
    (s1, k1), (s2, k2) = tops

    pieces, (tail_start, tail_a) = _candidate_pieces(topk)
    parts = []
    for _, rows, a, nb in pieces:
        part = s1[a:a + 1] + s2[:rows]
        if nb < rows:
            part = jnp.where(lax.broadcasted_iota(jnp.int32, (rows, tm), 0) < nb, part, -jnp.inf)
        parts.append(part)
    tail = s1[tail_a:] + s2[0:1]
    tail_rows = -(-(topk - tail_a) // SUBLANES) * SUBLANES
    if tail_rows > topk - tail_a:
        tail = jnp.concatenate([tail, jnp.full((tail_rows - (topk - tail_a), tm), -jnp.inf, _F32)], axis=0)
    cand = jnp.concatenate(parts + [tail], axis=0)
    best_s, pos = _top_rows(cand, topk)

    a_sel = pos - float(tail_start - tail_a)
    b_sel = jnp.zeros_like(pos)
    for start, rows, a, _ in pieces:
        inside = (pos >= float(start)) & (pos < float(start + rows))
        a_sel = jnp.where(inside, float(a), a_sel)
        b_sel = jnp.where(inside, pos - float(start), b_sel)
    i1 = jnp.zeros_like(pos)
    i2 = jnp.zeros_like(pos)
    for a in range(topk):
        i1 = jnp.where(a_sel == float(a), k1[a:a + 1], i1)
        i2 = jnp.where(b_sel == float(a), k2[a:a + 1], i2)
    e = jnp.exp(best_s - best_s[0:1])
    g_ref[...] = e / jnp.sum(e, axis=0, keepdims=True)
    i1_ref[...] = i1
    i2_ref[...] = i2


def _peer_retrieve(x, gain, w_query, sub_keys, tables=None, tm=512):
    t, d_model = x.shape
    heads, _, nkeys, dh = sub_keys.shape
    tm = min(tm, t)
    steps = (t // tm) * heads
    spec = pl.BlockSpec((PEER_TOPK, tm), lambda i, h: (h, i))
    shape = jax.ShapeDtypeStruct((heads * PEER_TOPK, t), _F32)
    wq = w_query.reshape(d_model, heads, 2 * dh).transpose(1, 0, 2)
    in_specs = [
        pl.BlockSpec((tm, d_model), lambda i, h: (i, 0)),
        pl.BlockSpec((1, d_model), lambda i, h: (0, 0)),
        pl.BlockSpec((heads, d_model, 2 * dh), lambda i, h: (0, 0, 0)),
        pl.BlockSpec((1, 2, nkeys, dh), lambda i, h: (h, 0, 0, 0)),
    ]
    out_specs = [spec, spec, spec, pl.BlockSpec((tm, d_model), lambda i, h: (i, 0))]
    out_shape = [shape, shape, shape, jax.ShapeDtypeStruct((t, d_model), _BF16)]
    operands = [x, gain.reshape(1, d_model).astype(_F32), wq, sub_keys.astype(_BF16)]
    if tables is not None:
        n_exp, d = tables[0].shape
        slab = n_exp // steps
        assert slab * steps == n_exp and slab % (2 * SUBLANES) == 0, (n_exp, steps)
        slab_spec = pl.BlockSpec((slab, d), lambda i, h: (i * heads + h, 0))
        in_specs += [slab_spec, slab_spec]
        out_specs += [slab_spec, slab_spec]
        out_shape += [jax.ShapeDtypeStruct((n_exp, d), _BF16)] * 2
        operands += list(tables)
    return pl.pallas_call(
        _peer_retrieve_kernel,
        grid=(t // tm, heads),
        in_specs=in_specs,
        out_specs=out_specs,
        out_shape=out_shape,
        scratch_shapes=[pltpu.VMEM((heads, tm, 2 * dh), _BF16)],
        compiler_params=_params("parallel", "arbitrary"),
        name="peer_retrieve",
    )(*operands)


def _peer_weights_kernel(g_ref, i1_ref, i2_ref, w_ref, gt_ref, i1t_ref, i2t_ref, scr_ref, *, tm, stride):
    nk = LANES
    gt_ref[...] = g_ref[...].T
    i1t_ref[...] = i1_ref[...].T
    i2t_ref[...] = i2_ref[...].T
    sub = lax.broadcasted_iota(jnp.int32, (nk, nk), 0).astype(_F32)

    zeros = jnp.zeros((nk, nk), _F32)

    def one_hots(n):
        r1 = jnp.broadcast_to(i1t_ref[pl.ds(n, 1), :], (nk, nk))
        r2 = jnp.broadcast_to(i2t_ref[pl.ds(n, 1), :], (nk, nk))
        rg = jnp.broadcast_to(gt_ref[pl.ds(n, 1), :], (nk, nk))
        return jnp.where(sub == r1, 1.0, 0.0), jnp.where(sub == r2, rg, 0.0)

    def body(t, carry):
        n = 2 * t
        a1, a2 = one_hots(n)
        b1, b2 = one_hots(n + 1)
        m1 = jnp.concatenate([a1, b1], axis=1).astype(_BF16)
        m2 = jnp.concatenate([jnp.concatenate([a2, zeros], axis=1),
                              jnp.concatenate([zeros, b2], axis=1)], axis=0).astype(_BF16)
        planes = _dot_nt(m1, m2)
        scr_ref[pl.ds(n, nk, stride=stride), :] = planes[:, :nk]
        scr_ref[pl.ds(n + 1, nk, stride=stride), :] = planes[:, nk:]
        return carry

    lax.fori_loop(0, tm // 2, body, 0, unroll=4 * SUBLANES)
    for j in range(nk):
        w_ref[:, j * nk:(j + 1) * nk] = scr_ref[j * stride:j * stride + tm, :].astype(w_ref.dtype)


def _peer_weights(g, i1, i2, nkeys, tm=256):
    picks, t = g.shape
    assert picks == LANES and nkeys == LANES
    tm = min(tm, t)
    stride = tm + SUBLANES
    spec = pl.BlockSpec((picks, tm), lambda i: (0, i))
    return pl.pallas_call(
        functools.partial(_peer_weights_kernel, tm=tm, stride=stride),
        grid=(t // tm,),
        in_specs=[spec, spec, spec],
        out_specs=pl.BlockSpec((tm, nkeys * nkeys), lambda i: (i, 0)),
        out_shape=jax.ShapeDtypeStruct((t, nkeys * nkeys), _BF16),
        scratch_shapes=[
            pltpu.VMEM((tm, picks), _F32),
            pltpu.VMEM((tm, picks), _F32),
            pltpu.VMEM((tm, picks), _F32),
            pltpu.VMEM((nkeys * stride, nkeys), _F32),
        ],
        compiler_params=_params("parallel"),
        name="peer_weights",
    )(g, i1, i2)


def _peer_ffn_kernel(xn_ref, w_ref, u_ref, v_ref, o_ref, *, tn):
    j = pl.program_id(1)

    @pl.when(j == 0)
    def _():
        o_ref[...] = jnp.zeros_like(o_ref)

    xn = xn_ref[...]
    parts = []
    for e in range(0, u_ref.shape[0], tn):
        h = _dot_nt(xn, u_ref[e:e + tn, :])
        w = w_ref[:, e:e + tn].astype(_F32)
        parts.append(jnp.where(w != 0.0, w * jax.nn.gelu(h), 0.0).astype(_BF16))
    coef = jnp.concatenate(parts, axis=1)
    for n in range(0, o_ref.shape[1], tn):
        o_ref[:, n:n + tn] += _dot(coef, v_ref[:, n:n + tn])


def _peer_ffn(xn, w, u, v, tm=1024, te=1024):
    t, d = xn.shape
    n_exp = u.shape[0]
    tm = min(tm, t)
    tn = _tile_n(d)
    blocks = 2 * (2 * (tm * d + tm * te + 2 * te * d) + 4 * tm * d)
    temps = tm * te * 2 + 4 * tm * tn * 4
    return pl.pallas_call(
        functools.partial(_peer_ffn_kernel, tn=tn),
        grid=(t // tm, n_exp // te),
        in_specs=[
            pl.BlockSpec((tm, d), lambda i, j: (i, 0)),
            pl.BlockSpec((tm, te), lambda i, j: (i, j)),
            pl.BlockSpec((te, d), lambda i, j: (j, 0)),
            pl.BlockSpec((te, d), lambda i, j: (j, 0)),
        ],
        out_specs=pl.BlockSpec((tm, d), lambda i, j: (i, 0)),
        out_shape=jax.ShapeDtypeStruct((t, d), _F32),
        compiler_params=_params("parallel", "arbitrary", vmem=max(VMEM_LIMIT, blocks + temps)),
        name="peer_ffn",
    )(xn, w, u, v)


def _residual_norm_kernel(x_ref, y_ref, g_ref, o_ref, *, final):
    x = x_ref[...] + y_ref[...]
    if final:
        x = x * lax.rsqrt(jnp.mean(x * x, axis=-1, keepdims=True) + RMS_EPS) * g_ref[...]
    o_ref[...] = x


def _residual_norm(x, y, gain, final):
    t, d = x.shape
    tm = min(t, 512)
    spec = pl.BlockSpec((tm, d), lambda i: (i, 0))
    return pl.pallas_call(
        functools.partial(_residual_norm_kernel, final=final),
        grid=(t // tm,),
        in_specs=[spec, spec, pl.BlockSpec((1, d), lambda i: (0, 0))],
        out_specs=spec,
        out_shape=jax.ShapeDtypeStruct((t, d), _F32),
        compiler_params=_params("parallel"),
        name="residual_norm",
    )(x, y, gain.reshape(1, d).astype(_F32))


def _layer(x, bsz, length, s0, past, lw, layer, final):
    d_model = x.shape[1]
    a_width = lw["gnorm"].shape[-1]
    b_width = lw["w_b"].shape[0]
    proj = _norm_matmul(x, lw["norm_mix"], lw["w_in"])
    b_q = 4 * a_width
    b_k, b_v = b_q + b_width, b_q + 2 * b_width
    g_a = b_q + 3 * b_width
    g_b = g_a + d_model

    o_a, s_new = _hgrn(proj, lw["lb_logits"], lw["gnorm"], s0, layer, bsz, length, a_width)
    if past is None:
        o_b, k_new, v_new = _moba_prompt(proj, bsz, length, b_q, b_k, b_v, b_width)
    else:
        o_b = _moba_sample(proj, past[0], past[1], past[2], bsz, length, b_q, b_k, b_v, b_width)
        k_new, v_new = proj[:, b_k:b_k + b_width], proj[:, b_v:b_v + b_width]
    merged = _merge(o_a, o_b, lw["w_a"], lw["w_b"], proj, g_a, g_b, d_model)
    x = _matmul_residual(merged, lw["w_out"], x)

    if "peer_u_bf16" in lw:
        gates, i1, i2, xn = _peer_retrieve(x, lw["norm_ffn"], lw["w_query"], lw["sub_keys"])
    else:
        gates, i1, i2, xn, lw["peer_u_bf16"], lw["peer_v_bf16"] = _peer_retrieve(
            x, lw["norm_ffn"], lw["w_query"], lw["sub_keys"], tables=(lw["peer_u"], lw["peer_v"]))
    w = _peer_weights(gates, i1, i2, lw["sub_keys"].shape[2])
    x = _residual_norm(x, _peer_ffn(xn, w, lw["peer_u_bf16"], lw["peer_v_bf16"]), lw["norm_final"], final)

    kv_shape = (bsz, length, B_HEADS, b_width // B_HEADS)
    return x, s_new, k_new.reshape(kv_shape), v_new.reshape(kv_shape)


def kernel(x_prompt, x_sample, cache_k, cache_v, state_hgrn, page_table, norm_mix, w_in, hgrn_lb_logits,
           hgrn_gnorm, w_a_proj, w_b_proj, w_out, norm_ffn, peer_w_query, peer_sub_keys, peer_u, peer_v,
           norm_final):
    depth = w_in.shape[0]
    bp, lp, d_model = x_prompt.shape
    bs, ls, _ = x_sample.shape
    x_p = x_prompt.reshape(bp * lp, d_model)
    x_s = x_sample.reshape(bs * ls, d_model)
    outs = [[] for _ in range(6)]
    for layer in range(depth):
        lw = {
            "norm_mix": norm_mix[layer], "w_in": w_in[layer].astype(_BF16), "lb_logits": hgrn_lb_logits,
            "gnorm": hgrn_gnorm[layer], "w_a": w_a_proj[layer].astype(_BF16), "w_b": w_b_proj[layer].astype(_BF16),
            "w_out": w_out[layer].astype(_BF16), "norm_ffn": norm_ffn[layer],
            "w_query": peer_w_query[layer].astype(_BF16), "sub_keys": peer_sub_keys[layer],
            "peer_u": peer_u[layer], "peer_v": peer_v[layer], "norm_final": norm_final,
        }
        final = layer == depth - 1
        s0_p = jnp.zeros((bp,) + state_hgrn.shape[2:], _F32)
        x_p, s_p, k_p, v_p = _layer(x_p, bp, lp, s0_p, None, lw, layer, final)
        past = (cache_k[layer], cache_v[layer], page_table)
        x_s, s_s, k_s, v_s = _layer(x_s, bs, ls, state_hgrn[layer], past, lw, layer, final)
        for lst, val in zip(outs, (s_p, s_s, k_p, v_p, k_s, v_s)):
            lst.append(val)
    sp, ss, kp, vp, ks, vs = (jnp.stack(lst, axis=0) for lst in outs)
    return (x_p.reshape(bp, lp, d_model), x_s.reshape(bs, ls, d_model),
            sp.astype(state_hgrn.dtype), ss.astype(state_hgrn.dtype),
            kp.astype(cache_k.dtype), vp.astype(cache_v.dtype), ks.astype(cache_k.dtype), vs.astype(cache_v.dtype))
```

```python
import functools

import jax
import jax.numpy as jnp
from jax import lax
from jax.experimental import pallas as pl
from jax.experimental.pallas import tpu as pltpu

A_HEADS = 8
B_HEADS = 8
MOBA_BLOCK = 256
MOBA_TOPK = 3
PEER_HEADS = 8
PEER_TOPK = 16
RMS_EPS = 1e-6
NEG_INF = -1e30

LANES = 128
SUBLANES = 8
HGRN_CHUNK = 128
HGRN_SUB = 16
HGRN_SAFE_SPAN = 40.0
VMEM_LIMIT = 48 * 1024 * 1024

_F32 = jnp.float32
_BF16 = jnp.bfloat16
_NT = (((1,), (1,)), ((), ()))


def _params(*sem, vmem=VMEM_LIMIT):
    return pltpu.CompilerParams(dimension_semantics=sem, vmem_limit_bytes=vmem)


def _dot_nt(a, b):
    return lax.dot_general(a, b, _NT, preferred_element_type=_F32)


def _dot(a, b):
    return jnp.dot(a, b, preferred_element_type=_F32)


def _log2(n):
    assert n > 0 and n & (n - 1) == 0, n
    return n.bit_length() - 1


def _div(x, n):
    return lax.shift_right_logical(x, jnp.int32(_log2(n)))


def _mod(x, n):
    return lax.bitwise_and(x, jnp.int32((1 << _log2(n)) - 1))


def _tile_n(n, *cols, cap=512):
    for tn in (1024, 512, 256, LANES):
        if tn <= cap and all(v % tn == 0 for v in (n,) + cols):
            return tn
    raise ValueError((n, cols))


def _norm_matmul_kernel(x_ref, g_ref, b_ref, o_ref, xn_ref):
    @pl.when(pl.program_id(1) == 0)
    def _():
        x = x_ref[...]
        xn_ref[...] = (x * lax.rsqrt(jnp.mean(x * x, axis=-1, keepdims=True) + RMS_EPS) * g_ref[...]).astype(_BF16)

    o_ref[...] = _dot(xn_ref[...], b_ref[...])


def _norm_matmul(x, gain, b):
    m, k = x.shape
    n = b.shape[1]
    tm, tn = min(m, 1024), _tile_n(n, cap=1024)
    return pl.pallas_call(
        _norm_matmul_kernel,
        grid=(m // tm, n // tn),
        in_specs=[
            pl.BlockSpec((tm, k), lambda i, j: (i, 0)),
            pl.BlockSpec((1, k), lambda i, j: (0, 0)),
            pl.BlockSpec((k, tn), lambda i, j: (0, j)),
        ],
        out_specs=pl.BlockSpec((tm, tn), lambda i, j: (i, j)),
        out_shape=jax.ShapeDtypeStruct((m, n), _F32),
        scratch_shapes=[pltpu.VMEM((tm, k), _BF16)],
        compiler_params=_params("parallel", "arbitrary"),
        name="norm_matmul",
    )(x, gain.reshape(1, k).astype(_F32), b)


def _merge_kernel(oa_ref, ob_ref, wa_ref, wb_ref, ga_ref, gb_ref, o_ref):
    ya = _dot(oa_ref[...].astype(_BF16), wa_ref[...])
    yb = _dot(ob_ref[...].astype(_BF16), wb_ref[...])
    o_ref[...] = (jax.nn.sigmoid(ga_ref[...]) * ya + jax.nn.sigmoid(gb_ref[...]) * yb).astype(o_ref.dtype)


def _merge(oa, ob, wa, wb, proj, ga_col, gb_col, d_model):
    t, ka = oa.shape
    kb = ob.shape[1]
    tm = min(t, 1024)
    tn = _tile_n(d_model, ga_col, gb_col)
    return pl.pallas_call(
        _merge_kernel,
        grid=(t // tm, d_model // tn),
        in_specs=[
            pl.BlockSpec((tm, ka), lambda i, j: (i, 0)),
            pl.BlockSpec((tm, kb), lambda i, j: (i, 0)),
            pl.BlockSpec((ka, tn), lambda i, j: (0, j)),
            pl.BlockSpec((kb, tn), lambda i, j: (0, j)),
            pl.BlockSpec((tm, tn), lambda i, j: (i, ga_col // tn + j)),
            pl.BlockSpec((tm, tn), lambda i, j: (i, gb_col // tn + j)),
        ],
        out_specs=pl.BlockSpec((tm, tn), lambda i, j: (i, j)),
        out_shape=jax.ShapeDtypeStruct((t, d_model), _BF16),
        compiler_params=_params("parallel", "parallel"),
        name="merge",
    )(oa, ob, wa, wb, proj, proj)


def _mm_residual_kernel(a_ref, b_ref, r_ref, o_ref):
    o_ref[...] = r_ref[...] + _dot(a_ref[...], b_ref[...])


def _matmul_residual(a, b, res):
    m, k = a.shape
    n = b.shape[1]
    tm = min(m, 1024)
    tn = _tile_n(n)
    return pl.pallas_call(
        _mm_residual_kernel,
        grid=(m // tm, n // tn),
        in_specs=[
            pl.BlockSpec((tm, k), lambda i, j: (i, 0)),
            pl.BlockSpec((k, tn), lambda i, j: (0, j)),
            pl.BlockSpec((tm, tn), lambda i, j: (i, j)),
        ],
        out_specs=pl.BlockSpec((tm, tn), lambda i, j: (i, j)),
        out_shape=jax.ShapeDtypeStruct((m, n), _F32),
        compiler_params=_params("parallel", "parallel"),
        name="matmul_residual",
    )(a, b, res)


def _hgrn_inputs(q, fpre, v, logits, *, layer, rows):
    c_len = HGRN_CHUNK

    def pad(x):
        if rows == c_len:
            return x
        return jnp.concatenate([x, jnp.zeros((c_len - rows, x.shape[1]), x.dtype)], axis=0)

    ex = jnp.exp(logits - jnp.max(logits, axis=0, keepdims=True))
    lb = jnp.sum(ex[: layer + 1], axis=0, keepdims=True) / jnp.sum(ex, axis=0, keepdims=True)

    forget = lb + (1.0 - lb) * jax.nn.sigmoid(fpre)
    b = pad(jnp.log(forget))
    row = lax.broadcasted_iota(jnp.int32, (c_len, LANES), 0)
    shift = 1
    while shift < c_len:
        b = b + jnp.where(row >= shift, pltpu.roll(b, shift, 0), 0.0)
        shift *= 2
    return pad(q), pad(1.0 - forget), pad(v), b


def _hgrn_start(b, i):
    return b[i * HGRN_SUB - 1:i * HGRN_SUB] if i > 0 else jnp.zeros((1, LANES), _F32)


def _hgrn_att_pairwise(q, kk, b, n_sub):
    c_len, sub = HGRN_CHUNK, HGRN_SUB
    rowc = lax.broadcasted_iota(jnp.int32, (sub, LANES), 0)
    lane = lax.broadcasted_iota(jnp.int32, (sub, LANES), 1)
    att_rows = []
    for i in range(n_sub):
        lo = i * sub
        qi, bi, ki = q[lo:lo + sub], b[lo:lo + sub], kk[lo:lo + sub]
        att = jnp.zeros((sub, LANES), _F32)
        for s in range(sub):
            dec = jnp.exp(jnp.minimum(bi - bi[s:s + 1], 0.0))
            x = jnp.where(rowc >= s, qi * dec * ki[s:s + 1], 0.0)
            att = jnp.where(lane == lo + s, jnp.sum(x, axis=-1, keepdims=True), att)
        if i > 0:
            ref_b = _hgrn_start(b, i)
            qt = (qi * jnp.exp(bi - ref_b)).astype(_BF16)
            kp = kk[:lo] * jnp.exp(ref_b - b[:lo])
            kp = jnp.concatenate([kp, jnp.zeros((c_len - lo, LANES), _F32)], axis=0).astype(_BF16)
            att = att + _dot_nt(qt, kp)
        att_rows.append(att)
    if n_sub * sub < c_len:
        att_rows.append(jnp.zeros((c_len - n_sub * sub, LANES), _F32))
    return jnp.concatenate(att_rows, axis=0)


def _hgrn_att_factored(q, kk, b, n_sub):
    c_len, sub = HGRN_CHUNK, HGRN_SUB
    rowc = lax.broadcasted_iota(jnp.int32, (sub, LANES), 0)
    lane = lax.broadcasted_iota(jnp.int32, (sub, LANES), 1)
    att_rows = []
    for i in range(n_sub):
        lo, hi = i * sub, (i + 1) * sub
        ref_b = _hgrn_start(b, i)
        qt = (q[lo:hi] * jnp.exp(b[lo:hi] - ref_b)).astype(_BF16)
        kp = kk[:hi] * jnp.exp(ref_b - b[:hi])
        if hi < c_len:
            kp = jnp.concatenate([kp, jnp.zeros((c_len - hi, LANES), _F32)], axis=0)
        att_rows.append(jnp.where(lane <= rowc + lo, _dot_nt(qt, kp.astype(_BF16)), 0.0))
    if n_sub * sub < c_len:
        att_rows.append(jnp.zeros((c_len - n_sub * sub, LANES), _F32))
    return jnp.concatenate(att_rows, axis=0)


def _hgrn_outputs(q, kk, v, b, att, st, ag, gn, rows):
    c_len = HGRN_CHUNK
    o = _dot_nt((q * jnp.exp(b)).astype(_BF16), st.astype(_BF16))
    o = o + _dot(att.astype(_BF16), v.astype(_BF16))
    b_last = b[c_len - 1:c_len]
    k_hat = (kk * jnp.exp(b_last - b)).astype(_BF16)
    st_new = st * jnp.exp(b_last) + _dot(v.T.astype(_BF16), k_hat)
    oo = o[:rows]
    on = oo * lax.rsqrt(jnp.mean(oo * oo, axis=-1, keepdims=True) + RMS_EPS) * gn
    return on * (ag * jax.nn.sigmoid(ag)), st_new


def _hgrn_kernel(q_ref, f_ref, i_ref, g_ref, lbl_ref, gn_ref, s0_ref, o_ref, sfin_ref, st_ref, *,
                 layer, rows, hb):
    ci = pl.program_id(2)
    sub = HGRN_SUB
    n_sub = -(-rows // sub)

    @pl.when(ci == 0)
    def _():
        for h in range(hb):
            st_ref[h] = s0_ref[0, h].T

    cols = [slice(h * LANES, (h + 1) * LANES) for h in range(hb)]
    heads = [_hgrn_inputs(q_ref[:, c], f_ref[:, c], i_ref[:, c], lbl_ref[:, c], layer=layer, rows=rows)
             for c in cols]
    spans = [_hgrn_start(b, i) - b[(i + 1) * sub - 1:(i + 1) * sub] for _, _, _, b in heads for i in range(n_sub)]
    worst = jnp.max(jnp.concatenate(spans, axis=0))
    atts = lax.cond(worst < HGRN_SAFE_SPAN,
                    lambda: tuple(_hgrn_att_factored(q, kk, b, n_sub) for q, kk, _, b in heads),
                    lambda: tuple(_hgrn_att_pairwise(q, kk, b, n_sub) for q, kk, _, b in heads))
    for h, c in enumerate(cols):
        q, kk, v, b = heads[h]
        out, st_new = _hgrn_outputs(q, kk, v, b, atts[h], st_ref[h], g_ref[:, c], gn_ref[:, c], rows)
        o_ref[:, c] = out.astype(o_ref.dtype)
        st_ref[h] = st_new

    @pl.when(ci == pl.num_programs(2) - 1)
    def _():
        for h in range(hb):
            sfin_ref[0, h] = st_ref[h].T


def _hgrn(proj, lb_logits, gnorm, s0, layer, bsz, length, width, hb=8):
    heads = A_HEADS
    dk = width // heads
    if length >= HGRN_CHUNK:
        assert length % HGRN_CHUNK == 0
        rows, nc = HGRN_CHUNK, length // HGRN_CHUNK
    else:
        assert length % SUBLANES == 0
        rows, nc = length, 1
        hb = heads
    hb = min(hb, heads)
    assert dk == LANES and heads % hb == 0
    sec = heads // hb
    wb = hb * dk

    def col(section):
        return lambda b, h, c: (b * nc + c, section * sec + h)

    return pl.pallas_call(
        functools.partial(_hgrn_kernel, layer=layer, rows=rows, hb=hb),
        grid=(bsz, heads // hb, nc),
        in_specs=[
            pl.BlockSpec((rows, wb), col(0)),
            pl.BlockSpec((rows, wb), col(1)),
            pl.BlockSpec((rows, wb), col(2)),
            pl.BlockSpec((rows, wb), col(3)),
            pl.BlockSpec((lb_logits.shape[0], wb), lambda b, h, c: (0, h)),
            pl.BlockSpec((1, wb), lambda b, h, c: (0, h)),
            pl.BlockSpec((1, hb, dk, dk), lambda b, h, c: (b, h, 0, 0)),
        ],
        out_specs=[
            pl.BlockSpec((rows, wb), lambda b, h, c: (b * nc + c, h)),
            pl.BlockSpec((1, hb, dk, dk), lambda b, h, c: (b, h, 0, 0)),
        ],
        out_shape=[
            jax.ShapeDtypeStruct((bsz * length, width), _BF16 if rows % (2 * SUBLANES) == 0 else _F32),
            jax.ShapeDtypeStruct((bsz, heads, dk, dk), _F32),
        ],
        scratch_shapes=[pltpu.VMEM((hb, dk, dk), _F32)],
        compiler_params=_params("parallel", "parallel", "arbitrary"),
        name="hgrn2",
    )(proj, proj, proj, proj, lb_logits.astype(_F32), gnorm.reshape(1, width).astype(_F32), s0.astype(_F32))


def _moba_prompt_block(q_ref, k_ref, v_ref, o_ref, *, cols, own, nblk):
    blk = MOBA_BLOCK
    hd = cols.stop - cols.start
    n_sel = min(MOBA_TOPK, nblk, own)
    q = q_ref[:, cols].astype(_BF16)
    k = k_ref[:(own + 1) * blk, cols]
    s = _dot_nt(q, k.astype(_BF16)) * (hd ** -0.5)

    sel = [None] * own
    if own > n_sel:
        k_mean = jnp.mean(k[:own * blk].reshape(own, blk, hd), axis=1)
        k_mean = jnp.concatenate([k_mean, jnp.zeros((LANES - own, hd), _F32)], axis=0)
        gate = _dot_nt(k_mean.astype(_BF16), q)
        nrow = -(-own // SUBLANES) * SUBLANES
        blk_id = lax.broadcasted_iota(jnp.int32, (nrow, blk), 0)
        g = jnp.where(blk_id < own, gate[:nrow], NEG_INF)
        rank = jnp.zeros((nrow, blk), jnp.int32)
        for jp in range(own):
            other = g[jp:jp + 1]
            ahead = (other > g) | ((other == g) & (blk_id > jp))
            rank = rank + ahead.astype(jnp.int32)
        picked = jnp.where((rank < n_sel) & (blk_id < own), 1.0, 0.0)
        picked = jnp.concatenate([picked, jnp.zeros((LANES - nrow, blk), _F32)], axis=0).T
        sel = [picked[:, j:j + 1] > 0.0 for j in range(own)]

    row = lax.broadcasted_iota(jnp.int32, (blk, blk), 0)
    col = lax.broadcasted_iota(jnp.int32, (blk, blk), 1)
    masked = []
    for j in range(own + 1):
        sj = s[:, j * blk:(j + 1) * blk]
        if j == own:
            sj = jnp.where(col <= row, sj, NEG_INF)
        elif sel[j] is not None:
            sj = jnp.where(sel[j], sj, NEG_INF)
        masked.append(sj)
    m = masked[0].max(axis=-1, keepdims=True)
    for sj in masked[1:]:
        m = jnp.maximum(m, sj.max(axis=-1, keepdims=True))
    l = jnp.zeros((blk, 1), _F32)
    acc = jnp.zeros((blk, hd), _F32)
    for j, sj in enumerate(masked):
        p = jnp.exp(sj - m)
        l = l + p.sum(axis=-1, keepdims=True)
        acc = acc + _dot(p.astype(_BF16), v_ref[j * blk:(j + 1) * blk, cols].astype(_BF16))
    o_ref[:, cols] = (acc / l).astype(o_ref.dtype)


def _moba_prompt_kernel(q_ref, k_ref, v_ref, o_ref, kout_ref, vout_ref, *, nblk, hp):
    i = pl.program_id(2)

    @pl.when(i == 0)
    def _():
        kout_ref[...] = k_ref[...]
        vout_ref[...] = v_ref[...]

    def blocks(own):
        for h in range(hp):
            _moba_prompt_block(q_ref, k_ref, v_ref, o_ref, cols=slice(h * LANES, (h + 1) * LANES), own=own, nblk=nblk)

    for own in range(nblk):
        pl.when(i == own)(functools.partial(blocks, own))


def _moba_prompt(proj, bsz, length, q_col, k_col, v_col, width, hp=4):
    heads = B_HEADS
    hd = width // heads
    hp = min(hp, heads)
    assert hd == LANES and length % MOBA_BLOCK == 0 and heads % hp == 0
    nblk = length // MOBA_BLOCK
    assert nblk <= LANES
    wb = hp * hd
    kv_spec = pl.BlockSpec((length, wb), lambda b, h, i: (b, h))
    kv_shape = jax.ShapeDtypeStruct((bsz * length, width), _F32)
    return pl.pallas_call(
        functools.partial(_moba_prompt_kernel, nblk=nblk, hp=hp),
        grid=(bsz, heads // hp, nblk),
        in_specs=[
            pl.BlockSpec((MOBA_BLOCK, wb), lambda b, h, i: (b * nblk + i, q_col // wb + h)),
            pl.BlockSpec((length, wb), lambda b, h, i: (b, k_col // wb + h)),
            pl.BlockSpec((length, wb), lambda b, h, i: (b, v_col // wb + h)),
        ],
        out_specs=[pl.BlockSpec((MOBA_BLOCK, wb), lambda b, h, i: (b * nblk + i, h)), kv_spec, kv_spec],
        out_shape=[jax.ShapeDtypeStruct((bsz * length, width), _BF16), kv_shape, kv_shape],
        compiler_params=_params("parallel", "parallel", "arbitrary"),
        name="moba_prompt",
    )(proj, proj, proj)


def _moba_sample_kernel(pt_ref, q_ref, kn_ref, vn_ref, *refs, npages, heads, lq, group):
    kc_hbm, vc_hbm, o_ref, qbd_ref, ks_ref, km_ref, m_ref, l_ref, acc_ref, kbuf, vbuf, sem = refs
    b, p = pl.program_id(0), pl.program_id(1)
    nsteps = npages // group
    nbuf = kbuf.shape[0]
    hd = kbuf.shape[3]
    page = kbuf.shape[2] // heads
    width = heads * hd
    rows = heads * lq
    blk_pages = MOBA_BLOCK // page
    nb = npages // blk_pages
    scale = hd ** -0.5

    total = pl.num_programs(0) * nsteps
    this = b * nsteps + p

    def page_copies(step, slot):
        seq, first = step // nsteps, (step % nsteps) * group
        copies = []
        for j in range(group):
            page_id = pt_ref[seq, first + j]
            copies.append(pltpu.make_async_copy(kc_hbm.at[page_id], kbuf.at[slot, j], sem.at[slot]))
            copies.append(pltpu.make_async_copy(vc_hbm.at[page_id], vbuf.at[slot, j], sem.at[slot]))
        return copies

    @pl.when(this == 0)
    def _():
        for step in range(nbuf - 1):
            for copy in page_copies(step, step):
                copy.start()

    ahead = this + nbuf - 1

    @pl.when(ahead < total)
    def _():
        for copy in page_copies(ahead, ahead % nbuf):
            copy.start()

    slot = this % nbuf
    for copy in page_copies(this, slot):
        copy.wait()
    kc_refs = [kbuf.at[slot, j] for j in range(group)]
    vc_refs = [vbuf.at[slot, j] for j in range(group)]

    def by_token(ref):
        return jnp.concatenate([ref[pl.ds(h, page, stride=heads), :] for h in range(heads)], axis=1)

    def own_head(x):
        return jnp.concatenate([x[h * lq:(h + 1) * lq, h * hd:(h + 1) * hd] for h in range(heads)], axis=0)

    @pl.when(p == 0)
    def _():
        qt = jnp.concatenate([q_ref[...]] * heads, axis=0)
        rh = _div(lax.broadcasted_iota(jnp.int32, (rows, width), 0), lq)
        ch = _div(lax.broadcasted_iota(jnp.int32, (rows, width), 1), hd)
        qbd_ref[...] = jnp.where(rh == ch, qt, 0.0).astype(_BF16)

    qbd = qbd_ref[...]
    for g0 in range(0, group, blk_pages):
        blk = (p * group + g0) // blk_pages
        s = [_dot_nt(qbd, by_token(kc_refs[g0 + g]).astype(_BF16)) * scale for g in range(blk_pages)]
        m = s[0].max(axis=1, keepdims=True)
        for sg in s[1:]:
            m = jnp.maximum(m, sg.max(axis=1, keepdims=True))
        l = jnp.zeros((rows, 1), _F32)
        acc = jnp.zeros((rows, width), _F32)
        for g in range(blk_pages):
            w = jnp.exp(s[g] - m)
            l = l + w.sum(axis=1, keepdims=True)
            acc = acc + _dot(w.astype(_BF16), by_token(vc_refs[g0 + g]).astype(_BF16))
            ks_ref[p * group + g0 + g] = kc_refs[g0 + g][...].reshape(page, heads, hd).sum(axis=0)
        m_ref[blk] = m
        l_ref[blk] = l
        acc_ref[blk] = own_head(acc)

    @pl.when(p == nsteps - 1)
    def _():
        k_sum = ks_ref[...].reshape(nb, blk_pages, heads, hd).sum(axis=1)
        km_ref[...] = k_sum.reshape(nb * heads, hd) * (1.0 / MOBA_BLOCK)
        k_mean = jnp.concatenate([km_ref[pl.ds(h, nb, stride=heads), :] for h in range(heads)], axis=1)
        gate = _dot_nt(qbd, k_mean.astype(_BF16))
        lane = lax.broadcasted_iota(jnp.int32, (rows, nb), 1)
        sel = jnp.zeros((rows, nb), _F32)
        for _ in range(MOBA_TOPK):
            best = jnp.max(gate, axis=1, keepdims=True)
            idx = jnp.min(jnp.where(gate == best, lane, nb), axis=1, keepdims=True)
            hit = lane == idx
            sel = jnp.where(hit, 1.0, sel)
            gate = jnp.where(hit, -jnp.inf, gate)

        pad = 2 * SUBLANES - lq
        kn = jnp.concatenate([kn_ref[...], jnp.zeros((pad, width), _F32)], axis=0).astype(_BF16)
        vn = jnp.concatenate([vn_ref[...], jnp.zeros((pad, width), _F32)], axis=0).astype(_BF16)
        s_own = _dot_nt(qbd, kn) * scale
        qi = _mod(lax.broadcasted_iota(jnp.int32, s_own.shape, 0), lq)
        kj = lax.broadcasted_iota(jnp.int32, s_own.shape, 1)
        s_own = jnp.where((kj <= qi) & (kj < lq), s_own, NEG_INF)

        picked = [sel[:, b:b + 1] > 0.0 for b in range(nb)]
        top = jnp.max(s_own, axis=1, keepdims=True)
        for b in range(nb):
            top = jnp.maximum(top, jnp.where(picked[b], m_ref[b], NEG_INF))
        p_own = jnp.exp(s_own - top)
        l = jnp.sum(p_own, axis=1, keepdims=True)
        out = own_head(_dot(p_own.astype(_BF16), vn))
        for b in range(nb):
            c = jnp.where(picked[b], jnp.exp(jnp.minimum(m_ref[b] - top, 0.0)), 0.0)
            l = l + c * l_ref[b]
            out = out + c * acc_ref[b]
        out = out / l
        for h in range(heads):
            o_ref[:, h * hd:(h + 1) * hd] = out[h * lq:(h + 1) * lq, :]


def _moba_sample(proj, cache_k, cache_v, page_table, bsz, lq, q_col, k_col, v_col, width):
    heads = B_HEADS
    hd = width // heads
    n_phys, page = cache_k.shape[0], cache_k.shape[1]
    npages = page_table.shape[1]
    past = npages * page
    assert MOBA_BLOCK % page == 0 and past % MOBA_BLOCK == 0 and lq % SUBLANES == 0 and lq <= 2 * SUBLANES
    assert past // MOBA_BLOCK >= MOBA_TOPK and heads % SUBLANES == 0
    blk_pages = MOBA_BLOCK // page
    nb = past // MOBA_BLOCK
    group = blk_pages * max(g for g in (1, 2, 4) if nb % g == 0)
    nsteps = npages // group
    nbuf = min(4, bsz * nsteps + 1)
    rows = heads * lq
    kc = cache_k.reshape(n_phys, page * heads, hd)
    vc = cache_v.reshape(n_phys, page * heads, hd)
    grid_spec = pltpu.PrefetchScalarGridSpec(
        num_scalar_prefetch=1,
        grid=(bsz, nsteps),
        in_specs=[
            pl.BlockSpec((lq, width), lambda b, p, pt: (b, q_col // width)),
            pl.BlockSpec((lq, width), lambda b, p, pt: (b, k_col // width)),
            pl.BlockSpec((lq, width), lambda b, p, pt: (b, v_col // width)),
            pl.BlockSpec(memory_space=pl.ANY),
            pl.BlockSpec(memory_space=pl.ANY),
        ],
        out_specs=pl.BlockSpec((lq, width), lambda b, p, pt: (b, 0)),
        scratch_shapes=[
            pltpu.VMEM((rows, width), _BF16),
            pltpu.VMEM((npages, heads, hd), _F32),
            pltpu.VMEM((nb * heads, hd), _F32),
            pltpu.VMEM((nb, rows, 1), _F32),
            pltpu.VMEM((nb, rows, 1), _F32),
            pltpu.VMEM((nb, rows, hd), _F32),
            pltpu.VMEM((nbuf, group, page * heads, hd), _F32),
            pltpu.VMEM((nbuf, group, page * heads, hd), _F32),
            pltpu.SemaphoreType.DMA((nbuf,)),
        ],
    )
    return pl.pallas_call(
        functools.partial(_moba_sample_kernel, npages=npages, heads=heads, lq=lq, group=group),
        grid_spec=grid_spec,
        out_shape=jax.ShapeDtypeStruct((bsz * lq, width), _F32),
        compiler_params=_params("arbitrary", "arbitrary"),
        name="moba_sample",
    )(page_table.astype(jnp.int32), proj, proj, proj, kc, vc)


def _top_rows(x, k):
    r, t = x.shape
    rowi = lax.broadcasted_iota(jnp.int32, (r, t), 0).astype(_F32)
    slot = lax.broadcasted_iota(jnp.int32, (k, t), 0)
    vals = jnp.zeros((k, t), _F32)
    idxs = jnp.zeros((k, t), _F32)
    for n in range(k):
        best = jnp.max(x, axis=0, keepdims=True)
        idx = jnp.min(jnp.where(x == best, rowi, float(r)), axis=0, keepdims=True)
        vals = jnp.where(slot == n, best, vals)
        idxs = jnp.where(slot == n, idx, idxs)
        x = jnp.where(rowi == idx, -jnp.inf, x)
    return vals, idxs


def _candidate_pieces(topk):
    pieces, start = [], 0
    a = 0
    while topk // (a + 1) > 1:
        nb = topk // (a + 1)
        rows = -(-nb // SUBLANES) * SUBLANES
        pieces.append((start, rows, a, nb))
        start += rows
        a += 1
    return pieces, (start, a)


def _peer_retrieve_kernel(x_ref, gain_ref, wq_ref, sk_ref, *refs):
    if len(refs) > 5:
        u_ref, v_ref, g_ref, i1_ref, i2_ref, xn_ref, ub_ref, vb_ref, q_ref = refs
        ub_ref[...] = u_ref[...].astype(_BF16)
        vb_ref[...] = v_ref[...].astype(_BF16)
    else:
        g_ref, i1_ref, i2_ref, xn_ref, q_ref = refs
    h = pl.program_id(1)
    heads, qw = q_ref.shape[0], q_ref.shape[2]

    @pl.when(h == 0)
    def _():
        x = x_ref[...]
        xn = (x * lax.rsqrt(jnp.mean(x * x, axis=-1, keepdims=True) + RMS_EPS) * gain_ref[...]).astype(_BF16)
        xn_ref[...] = xn
        for hh in range(heads):
            q_ref[hh] = _dot(xn, wq_ref[:, hh * qw:(hh + 1) * qw]).astype(_BF16)

    q = q_ref[h]

    topk = PEER_TOPK
    dh = sk_ref.shape[-1]
    tm = x_ref.shape[0]
    tops = []
    for a in range(2):
        qa = q[:, a * dh:(a + 1) * dh]
        tops.append(_top_rows(_dot_nt(sk_ref[0, a], qa), topk))
    (s1, k1), (s2, k2) = tops

    pieces, (tail_start, tail_a) = _candidate_pieces(topk)
    parts = []
    for _, rows, a, nb in pieces:
        part = s1[a:a + 1] + s2[:rows]
        if nb < rows:
            part = jnp.where(lax.broadcasted_iota(jnp.int32, (rows, tm), 0) < nb, part, -jnp.inf)
        parts.append(part)
    tail = s1[tail_a:] + s2[0:1]
    tail_rows = -(-(topk - tail_a) // SUBLANES) * SUBLANES
    if tail_rows > topk - tail_a:
        tail = jnp.concatenate([tail, jnp.full((tail_rows - (topk - tail_a), tm), -jnp.inf, _F32)], axis=0)
    cand = jnp.concatenate(parts + [tail], axis=0)
    best_s, pos = _top_rows(cand, topk)

    a_sel = pos - float(tail_start - tail_a)
    b_sel = jnp.zeros_like(pos)
    for start, rows, a, _ in pieces:
        inside = (pos >= float(start)) & (pos < float(start + rows))
        a_sel = jnp.where(inside, float(a), a_sel)
        b_sel = jnp.where(inside, pos - float(start), b_sel)
    i1 = jnp.zeros_like(pos)
    i2 = jnp.zeros_like(pos)
    for a in range(topk):
        i1 = jnp.where(a_sel == float(a), k1[a:a + 1], i1)
        i2 = jnp.where(b_sel == float(a), k2[a:a + 1], i2)
    e = jnp.exp(best_s - best_s[0:1])
    g_ref[...] = e / jnp.sum(e, axis=0, keepdims=True)
    i1_ref[...] = i1
    i2_ref[...] = i2


def _peer_retrieve(x, gain, w_query, sub_keys, tables=None, tm=512):
    t, d_model = x.shape
    heads, _, nkeys, dh = sub_keys.shape
    tm = min(tm, t)
    steps = (t // tm) * heads
    spec = pl.BlockSpec((PEER_TOPK, tm), lambda i, h: (h, i))
    shape = jax.ShapeDtypeStruct((heads * PEER_TOPK, t), _F32)
    in_specs = [
        pl.BlockSpec((tm, d_model), lambda i, h: (i, 0)),
        pl.BlockSpec((1, d_model), lambda i, h: (0, 0)),
        pl.BlockSpec((d_model, heads * 2 * dh), lambda i, h: (0, 0)),
        pl.BlockSpec((1, 2, nkeys, dh), lambda i, h: (h, 0, 0, 0)),
    ]
    out_specs = [spec, spec, spec, pl.BlockSpec((tm, d_model), lambda i, h: (i, 0))]
    out_shape = [shape, shape, shape, jax.ShapeDtypeStruct((t, d_model), _BF16)]
    operands = [x, gain.reshape(1, d_model).astype(_F32), w_query, sub_keys.astype(_BF16)]
    if tables is not None:
        n_exp, d = tables[0].shape
        slab = n_exp // steps
        assert slab * steps == n_exp and slab % (2 * SUBLANES) == 0, (n_exp, steps)
        slab_spec = pl.BlockSpec((slab, d), lambda i, h: (i * heads + h, 0))
        in_specs += [slab_spec, slab_spec]
        out_specs += [slab_spec, slab_spec]
        out_shape += [jax.ShapeDtypeStruct((n_exp, d), _BF16)] * 2
        operands += list(tables)
    return pl.pallas_call(
        _peer_retrieve_kernel,
        grid=(t // tm, heads),
        in_specs=in_specs,
        out_specs=out_specs,
        out_shape=out_shape,
        scratch_shapes=[pltpu.VMEM((heads, tm, 2 * dh), _BF16)],
        compiler_params=_params("parallel", "arbitrary"),
        name="peer_retrieve",
    )(*operands)


def _peer_weights_kernel(g_ref, i1_ref, i2_ref, w_ref, gt_ref, i1t_ref, i2t_ref, scr_ref, *, tm, stride):
    nk = LANES
    gt_ref[...] = g_ref[...].T
    i1t_ref[...] = i1_ref[...].T
    i2t_ref[...] = i2_ref[...].T
    sub = lax.broadcasted_iota(jnp.int32, (nk, nk), 0).astype(_F32)

    zeros = jnp.zeros((nk, nk), _F32)

    def one_hots(n):
        r1 = jnp.broadcast_to(i1t_ref[pl.ds(n, 1), :], (nk, nk))
        r2 = jnp.broadcast_to(i2t_ref[pl.ds(n, 1), :], (nk, nk))
        rg = jnp.broadcast_to(gt_ref[pl.ds(n, 1), :], (nk, nk))
        return jnp.where(sub == r1, 1.0, 0.0), jnp.where(sub == r2, rg, 0.0)

    def body(t, carry):
        n = 2 * t
        a1, a2 = one_hots(n)
        b1, b2 = one_hots(n + 1)
        m1 = jnp.concatenate([a1, b1], axis=1).astype(_BF16)
        m2 = jnp.concatenate([jnp.concatenate([a2, zeros], axis=1),
                              jnp.concatenate([zeros, b2], axis=1)], axis=0).astype(_BF16)
        planes = _dot_nt(m1, m2)
        scr_ref[pl.ds(n, nk, stride=stride), :] = planes[:, :nk]
        scr_ref[pl.ds(n + 1, nk, stride=stride), :] = planes[:, nk:]
        return carry

    lax.fori_loop(0, tm // 2, body, 0, unroll=4 * SUBLANES)
    for j in range(nk):
        w_ref[:, j * nk:(j + 1) * nk] = scr_ref[j * stride:j * stride + tm, :].astype(w_ref.dtype)


def _peer_weights(g, i1, i2, nkeys, tm=256):
    picks, t = g.shape
    assert picks == LANES and nkeys == LANES
    tm = min(tm, t)
    stride = tm + SUBLANES
    spec = pl.BlockSpec((picks, tm), lambda i: (0, i))
    return pl.pallas_call(
        functools.partial(_peer_weights_kernel, tm=tm, stride=stride),
        grid=(t // tm,),
        in_specs=[spec, spec, spec],
        out_specs=pl.BlockSpec((tm, nkeys * nkeys), lambda i: (i, 0)),
        out_shape=jax.ShapeDtypeStruct((t, nkeys * nkeys), _BF16),
        scratch_shapes=[
            pltpu.VMEM((tm, picks), _F32),
            pltpu.VMEM((tm, picks), _F32),
            pltpu.VMEM((tm, picks), _F32),
            pltpu.VMEM((nkeys * stride, nkeys), _F32),
        ],
        compiler_params=_params("parallel"),
        name="peer_weights",
    )(g, i1, i2)


def _peer_ffn_kernel(xn_ref, w_ref, u_ref, v_ref, o_ref, *, tn):
    j = pl.program_id(1)

    @pl.when(j == 0)
    def _():
        o_ref[...] = jnp.zeros_like(o_ref)

    xn = xn_ref[...]
    parts = []
    for e in range(0, u_ref.shape[0], tn):
        h = _dot_nt(xn, u_ref[e:e + tn, :])
        w = w_ref[:, e:e + tn].astype(_F32)
        parts.append(jnp.where(w != 0.0, w * jax.nn.gelu(h), 0.0).astype(_BF16))
    coef = jnp.concatenate(parts, axis=1)
    for n in range(0, o_ref.shape[1], tn):
        o_ref[:, n:n + tn] += _dot(coef, v_ref[:, n:n + tn])


def _peer_ffn(xn, w, u, v, tm=1024, te=1024):
    t, d = xn.shape
    n_exp = u.shape[0]
    tm = min(tm, t)
    tn = _tile_n(d)
    blocks = 2 * (2 * (tm * d + tm * te + 2 * te * d) + 4 * tm * d)
    temps = tm * te * 2 + 4 * tm * tn * 4
    return pl.pallas_call(
        functools.partial(_peer_ffn_kernel, tn=tn),
        grid=(t // tm, n_exp // te),
        in_specs=[
            pl.BlockSpec((tm, d), lambda i, j: (i, 0)),
            pl.BlockSpec((tm, te), lambda i, j: (i, j)),
            pl.BlockSpec((te, d), lambda i, j: (j, 0)),
            pl.BlockSpec((te, d), lambda i, j: (j, 0)),
        ],
        out_specs=pl.BlockSpec((tm, d), lambda i, j: (i, 0)),
        out_shape=jax.ShapeDtypeStruct((t, d), _F32),
        compiler_params=_params("parallel", "arbitrary", vmem=max(VMEM_LIMIT, blocks + temps)),
        name="peer_ffn",
    )(xn, w, u, v)


def _residual_norm_kernel(x_ref, y_ref, g_ref, o_ref, *, final):
    x = x_ref[...] + y_ref[...]
    if final:
        x = x * lax.rsqrt(jnp.mean(x * x, axis=-1, keepdims=True) + RMS_EPS) * g_ref[...]
    o_ref[...] = x


def _residual_norm(x, y, gain, final):
    t, d = x.shape
    tm = min(t, 512)
    spec = pl.BlockSpec((tm, d), lambda i: (i, 0))
    return pl.pallas_call(
        functools.partial(_residual_norm_kernel, final=final),
        grid=(t // tm,),
        in_specs=[spec, spec, pl.BlockSpec((1, d), lambda i: (0, 0))],
        out_specs=spec,
        out_shape=jax.ShapeDtypeStruct((t, d), _F32),
        compiler_params=_params("parallel"),
        name="residual_norm",
    )(x, y, gain.reshape(1, d).astype(_F32))


def _layer(x, bsz, length, s0, past, lw, layer, final):
    d_model = x.shape[1]
    a_width = lw["gnorm"].shape[-1]
    b_width = lw["w_b"].shape[0]
    proj = _norm_matmul(x, lw["norm_mix"], lw["w_in"])
    b_q = 4 * a_width
    b_k, b_v = b_q + b_width, b_q + 2 * b_width
    g_a = b_q + 3 * b_width
    g_b = g_a + d_model

    o_a, s_new = _hgrn(proj, lw["lb_logits"], lw["gnorm"], s0, layer, bsz, length, a_width)
    if past is None:
        o_b, k_new, v_new = _moba_prompt(proj, bsz, length, b_q, b_k, b_v, b_width)
    else:
        o_b = _moba_sample(proj, past[0], past[1], past[2], bsz, length, b_q, b_k, b_v, b_width)
        k_new, v_new = proj[:, b_k:b_k + b_width], proj[:, b_v:b_v + b_width]
    merged = _merge(o_a, o_b, lw["w_a"], lw["w_b"], proj, g_a, g_b, d_model)
    x = _matmul_residual(merged, lw["w_out"], x)

    if "peer_u_bf16" in lw:
        gates, i1, i2, xn = _peer_retrieve(x, lw["norm_ffn"], lw["w_query"], lw["sub_keys"])
    else:
        gates, i1, i2, xn, lw["peer_u_bf16"], lw["peer_v_bf16"] = _peer_retrieve(
            x, lw["norm_ffn"], lw["w_query"], lw["sub_keys"], tables=(lw["peer_u"], lw["peer_v"]))
    w = _peer_weights(gates, i1, i2, lw["sub_keys"].shape[2])
    x = _residual_norm(x, _peer_ffn(xn, w, lw["peer_u_bf16"], lw["peer_v_bf16"]), lw["norm_final"], final)

    kv_shape = (bsz, length, B_HEADS, b_width // B_HEADS)
    return x, s_new, k_new.reshape(kv_shape), v_new.reshape(kv_shape)


def kernel(x_prompt, x_sample, cache_k, cache_v, state_hgrn, page_table, norm_mix, w_in, hgrn_lb_logits,
           hgrn_gnorm, w_a_proj, w_b_proj, w_out, norm_ffn, peer_w_query, peer_sub_keys, peer_u, peer_v,
           norm_final):
    depth = w_in.shape[0]
    bp, lp, d_model = x_prompt.shape
    bs, ls, _ = x_sample.shape
    x_p = x_prompt.reshape(bp * lp, d_model)
    x_s = x_sample.reshape(bs * ls, d_model)
    outs = [[] for _ in range(6)]
    for layer in range(depth):
        lw = {
            "norm_mix": norm_mix[layer], "w_in": w_in[layer].astype(_BF16), "lb_logits": hgrn_lb_logits,
            "gnorm": hgrn_gnorm[layer], "w_a": w_a_proj[layer].astype(_BF16), "w_b": w_b_proj[layer].astype(_BF16),
            "w_out": w_out[layer].astype(_BF16), "norm_ffn": norm_ffn[layer],
            "w_query": peer_w_query[layer].astype(_BF16), "sub_keys": peer_sub_keys[layer],
            "peer_u": peer_u[layer], "peer_v": peer_v[layer], "norm_final": norm_final,
        }
        final = layer == depth - 1
        s0_p = jnp.zeros((bp,) + state_hgrn.shape[2:], _F32)
        x_p, s_p, k_p, v_p = _layer(x_p, bp, lp, s0_p, None, lw, layer, final)
        past = (cache_k[layer], cache_v[layer], page_table)
        x_s, s_s, k_s, v_s = _layer(x_s, bs, ls, state_hgrn[layer], past, lw, layer, final)
        for lst, val in zip(outs, (s_p, s_s, k_p, v_p, k_s, v_s)):
            lst.append(val)
    sp, ss, kp, vp, ks, vs = (jnp.stack(lst, axis=0) for lst in outs)
    return (x_p.reshape(bp, lp, d_model), x_s.reshape(bs, ls, d_model),
            sp.astype(state_hgrn.dtype), ss.astype(state_hgrn.dtype),
            kp.astype(cache_k.dtype), vp.astype(cache_v.dtype), ks.astype(cache_k.dtype), vs.astype(cache_v.dtype))
```
